```python
import math
import jax, jax.numpy as jnp
from jax import lax
import numpy as np

D_MODEL = 1024
BATCH = 16
SEQ = 256
DEPTH = 2
DEC_BATCH = 8
DEC_SEQ = 2048
PAST_LEN = 256

GRID_W = 64
FOURIER_GROUPS = 4
FOURIER_WIDTH = D_MODEL // 4
FOURIER_GROUP_DIM = FOURIER_WIDTH // FOURIER_GROUPS
POOL_WINDOWS = (2, 4, 8, 16)
POOL_WIDTH = D_MODEL // 4
POOL_GROUP_DIM = POOL_WIDTH // len(POOL_WINDOWS)
MLSTM_HEADS = 4
MLSTM_WIDTH = D_MODEL // 2
MLSTM_HEAD_DIM = MLSTM_WIDTH // MLSTM_HEADS
MLSTM_CHUNK = 64
N_DIR = 2
N_BRANCH = 3
N_GATE_COLS = N_DIR * 2 * MLSTM_HEADS
D_FF = -(-8 * D_MODEL // (3 * 256)) * 256
IN_WIDTHS = (FOURIER_WIDTH, POOL_WIDTH, MLSTM_WIDTH, MLSTM_WIDTH, MLSTM_WIDTH, MLSTM_WIDTH, N_GATE_COLS, N_BRANCH * D_MODEL)
IN_SPLITS = tuple(sum(IN_WIDTHS[:i + 1]) for i in range(len(IN_WIDTHS) - 1))
D_IN = sum(IN_WIDTHS)
RMS_EPS = 1e-6
POS_BASE = 10000.0

kernel_name = 'hybrid_fourier_pool_mlstm_flow_step'


def rms_norm(x, g):
    xf = x.astype(jnp.float32)
    y = xf * lax.rsqrt(jnp.mean(xf * xf, axis=-1, keepdims=True) + RMS_EPS)
    return (y * g.astype(jnp.float32)).astype(x.dtype)


def grid_pos_embed(rows, dtype):
    quarter = D_MODEL // 4
    omega = 1.0 / (POS_BASE ** (jnp.arange(quarter, dtype=jnp.float32) / quarter))
    t = jnp.arange(rows * GRID_W)
    r = (t // GRID_W).astype(jnp.float32)
    col = (t % GRID_W).astype(jnp.float32)
    ar = r[:, None] * omega[None, :]
    ac = col[:, None] * omega[None, :]
    return jnp.concatenate([jnp.sin(ar), jnp.cos(ar), jnp.sin(ac), jnp.cos(ac)], axis=-1).astype(dtype)


def fourier_mix(u):
    B, L, _ = u.shape
    ug = u.astype(jnp.float32).reshape(B, L, FOURIER_GROUPS, FOURIER_GROUP_DIM)
    y = jnp.fft.fftn(ug, axes=(1, 3), norm='ortho').real
    return y.reshape(B, L, FOURIER_WIDTH).astype(u.dtype)


def pool_mix(u, rows, w_pool_l, pool_scale_l):
    B, L, _ = u.shape
    W = L // rows
    ur = u.astype(jnp.float32).reshape(B, rows, W, len(POOL_WINDOWS), POOL_GROUP_DIM)
    cs = jnp.concatenate([jnp.zeros_like(ur[:, :, :1]), jnp.cumsum(ur, axis=2)], axis=2)
    pos = jnp.arange(W)
    outs = []
    for g, w in enumerate(POOL_WINDOWS):
        left = w // 2
        right = w - 1 - left
        lo = jnp.clip(pos - left, 0, W - 1)
        hi = jnp.clip(pos + right, 0, W - 1)
        cnt = (hi - lo + 1).astype(jnp.float32)
        csg = cs[:, :, :, g]
        s = jnp.take(csg, hi + 1, axis=2) - jnp.take(csg, lo, axis=2)
        outs.append(s / cnt[:, None] - ur[:, :, :, g])
    p = jnp.stack(outs, axis=3).astype(u.dtype)
    p = jnp.einsum('brwgc,gcd->brwgd', p, w_pool_l)
    return p.reshape(B, L, POOL_WIDTH) * pool_scale_l


def mlstm_dir(q, k, v, ig, lf, C0, n0, m0):
    B, H, L, D = q.shape
    nc = L // MLSTM_CHUNK

    def to_chunks(a):
        a = a.reshape((B, H, nc, MLSTM_CHUNK) + a.shape[3:])
        return jnp.moveaxis(a, 2, 0)

    lower = jnp.tril(jnp.ones((MLSTM_CHUNK, MLSTM_CHUNK), dtype=bool))

    def step(carry, xs):
        C, n, m = carry
        qc, kc, vc, ic, fc = xs
        b = jnp.cumsum(fc, axis=-1)
        state_log = b + m[..., None]
        log_d = jnp.where(lower, b[..., :, None] - b[..., None, :] + ic[..., None, :], -jnp.inf)
        m_t = jnp.maximum(state_log, jnp.max(log_d, axis=-1))
        d = jnp.exp(log_d - m_t[..., None])
        sdec = jnp.exp(state_log - m_t)
        s = jnp.einsum('bhtd,bhsd->bhts', qc, kc) * d
        num = sdec[..., None] * jnp.einsum('bhtd,bhde->bhte', qc, C) + jnp.einsum('bhts,bhse->bhte', s, vc)
        den = sdec * jnp.einsum('bhtd,bhd->bht', qc, n) + jnp.sum(s, axis=-1)
        h = num / jnp.maximum(jnp.abs(den), jnp.exp(-m_t))[..., None]
        b_last = b[..., -1]
        w_log = b_last[..., None] - b + ic
        m_new = jnp.maximum(b_last + m, jnp.max(w_log, axis=-1))
        w = jnp.exp(w_log - m_new[..., None])
        carry_dec = jnp.exp(b_last + m - m_new)
        C_new = carry_dec[..., None, None] * C + jnp.einsum('bhs,bhsd,bhse->bhde', w, kc, vc)
        n_new = carry_dec[..., None] * n + jnp.einsum('bhs,bhsd->bhd', w, kc)
        return (C_new, n_new, m_new), h

    xs = (to_chunks(q), to_chunks(k), to_chunks(v), to_chunks(ig), to_chunks(lf))
    init = (C0.astype(jnp.float32), n0.astype(jnp.float32), m0.astype(jnp.float32))
    (C, n, m), h = lax.scan(step, init, xs)
    h = jnp.moveaxis(h, 0, 2).reshape(B, H, L, D)
    return h, C, n, m


def parallel_mixer(h, rows, C0, n0, m0, w_in_l, b_gate_l, w_proj_f_l, w_pool_l, pool_scale_l, w_proj_p_l, w_proj_m_l, w_out_l):
    B, L, _ = h.shape
    z = h @ w_in_l
    u_f, u_p, q, k, v, o, g_pre, merge = jnp.split(z, IN_SPLITS, axis=-1)
    y_f = fourier_mix(u_f) @ w_proj_f_l
    y_p = pool_mix(u_p, rows, w_pool_l, pool_scale_l) @ w_proj_p_l
    def heads(a):
        return a.astype(jnp.float32).reshape(B, L, MLSTM_HEADS, MLSTM_HEAD_DIM).transpose(0, 2, 1, 3)
    qh = heads(q)
    kh = heads(k) * (MLSTM_HEAD_DIM ** -0.5)
    vh = heads(v)
    gp = (g_pre.astype(jnp.float32).reshape(B, L, N_DIR, 2, MLSTM_HEADS) + b_gate_l.astype(jnp.float32)).transpose(2, 3, 0, 4, 1)
    ig = gp[:, 0]
    lf = jax.nn.log_sigmoid(gp[:, 1])
    hf, Cf, nf, mf = mlstm_dir(qh, kh, vh, ig[0], lf[0], C0[:, 0], n0[:, 0], m0[:, 0])
    def rev(a):
        return jnp.flip(a, axis=2)
    hb, Cb, nb, mb = mlstm_dir(rev(qh), rev(kh), rev(vh), rev(ig[1]), rev(lf[1]), C0[:, 1], n0[:, 1], m0[:, 1])
    hm = (hf + rev(hb)).transpose(0, 2, 1, 3).reshape(B, L, MLSTM_WIDTH).astype(h.dtype)
    y_m = (jax.nn.sigmoid(o) * hm) @ w_proj_m_l
    g_f, g_p, g_m = jnp.split(jax.nn.sigmoid(merge), N_BRANCH, axis=-1)
    y = (g_f * y_f + g_p * y_p + g_m * y_m) @ w_out_l
    C_out = jnp.stack([Cf, Cb], axis=1).astype(h.dtype)
    n_out = jnp.stack([nf, nb], axis=1).astype(h.dtype)
    m_out = jnp.stack([mf, mb], axis=1).astype(h.dtype)
    return y, C_out, n_out, m_out


def block(x, mod, rows, C0, n0, m0, norm_g_l, w_in_l, b_gate_l, w_proj_f_l, w_pool_l, pool_scale_l, w_proj_p_l, w_proj_m_l, w_out_l, w_ffn_in_l, w_ffn_out_l):
    shift1, scale1, gate1, shift2, scale2, gate2 = jnp.split(mod, 6, axis=-1)
    h = rms_norm(x, norm_g_l[0]) * (1 + scale1) + shift1
    y, C_out, n_out, m_out = parallel_mixer(h, rows, C0, n0, m0, w_in_l, b_gate_l, w_proj_f_l, w_pool_l, pool_scale_l, w_proj_p_l, w_proj_m_l, w_out_l)
    x = x + gate1 * rms_norm(y, norm_g_l[1])
    h = rms_norm(x, norm_g_l[2]) * (1 + scale2) + shift2
    a, b = jnp.split(h @ w_ffn_in_l, 2, axis=-1)
    x = x + gate2 * rms_norm((jax.nn.silu(a) * b) @ w_ffn_out_l, norm_g_l[3])
    return x, C_out, n_out, m_out


def setup_inputs(seed: int = 0) -> dict:
    key = jax.random.key(seed)
    ks = jax.random.split(key, 20)
    H, DH = MLSTM_HEADS, MLSTM_HEAD_DIM
    nrm = jax.random.normal
    gate_base = jnp.stack([jnp.zeros((H,), jnp.float32), jnp.linspace(3.0, 6.0, H, dtype=jnp.float32)])
    return {
        'x_prompt': nrm(ks[0], (BATCH, SEQ, D_MODEL), jnp.float32),
        'x_sample': nrm(ks[1], (DEC_BATCH, DEC_SEQ, D_MODEL), jnp.float32),
        'state_C': 0.3 * nrm(ks[2], (DEC_BATCH, DEPTH, N_DIR, H, DH, DH), jnp.float32),
        'state_n': 0.3 * nrm(ks[3], (DEC_BATCH, DEPTH, N_DIR, H, DH), jnp.float32),
        'state_m': nrm(ks[4], (DEC_BATCH, DEPTH, N_DIR, H), jnp.float32),
        'c': nrm(ks[5], (DEC_BATCH, D_MODEL), jnp.float32),
        'c_ctx': nrm(ks[6], (D_MODEL,), jnp.float32),
        'w_ada': 0.5 * D_MODEL ** -0.5 * nrm(ks[7], (DEPTH, D_MODEL, 6 * D_MODEL), jnp.float32),
        'b_ada': 0.02 * nrm(ks[8], (DEPTH, 6 * D_MODEL), jnp.float32),
        'norm_g': 1.0 + 0.05 * nrm(ks[9], (DEPTH, 4, D_MODEL), jnp.float32),
        'w_in': D_MODEL ** -0.5 * nrm(ks[10], (DEPTH, D_MODEL, D_IN), jnp.float32),
        'b_gate': gate_base[None, None] + 0.1 * nrm(ks[11], (DEPTH, N_DIR, 2, H), jnp.float32),
        'w_proj_f': FOURIER_WIDTH ** -0.5 * nrm(ks[12], (DEPTH, FOURIER_WIDTH, D_MODEL), jnp.float32),
        'w_pool': POOL_GROUP_DIM ** -0.5 * nrm(ks[13], (DEPTH, len(POOL_WINDOWS), POOL_GROUP_DIM, POOL_GROUP_DIM), jnp.float32),
        'pool_scale': 1.0 + 0.1 * nrm(ks[14], (DEPTH, POOL_WIDTH), jnp.float32),
        'w_proj_p': POOL_WIDTH ** -0.5 * nrm(ks[15], (DEPTH, POOL_WIDTH, D_MODEL), jnp.float32),
        'w_proj_m': MLSTM_WIDTH ** -0.5 * nrm(ks[16], (DEPTH, MLSTM_WIDTH, D_MODEL), jnp.float32),
        'w_out': D_MODEL ** -0.5 * nrm(ks[17], (DEPTH, D_MODEL, D_MODEL), jnp.float32),
        'w_ffn_in': D_MODEL ** -0.5 * nrm(ks[18], (DEPTH, D_MODEL, 2 * D_FF), jnp.float32),
        'w_ffn_out': D_FF ** -0.5 * nrm(ks[19], (DEPTH, D_FF, D_MODEL), jnp.float32),
    }


def reference(x_prompt, x_sample, state_C, state_n, state_m, c, c_ctx, w_ada, b_ada, norm_g, w_in, b_gate, w_proj_f, w_pool, pool_scale, w_proj_p, w_proj_m, w_out, w_ffn_in, w_ffn_out):
    def layer_params(i):
        return (norm_g[i], w_in[i], b_gate[i], w_proj_f[i], w_pool[i], pool_scale[i], w_proj_p[i], w_proj_m[i], w_out[i], w_ffn_in[i], w_ffn_out[i])

    Bp = x_prompt.shape[0]
    H, DH = MLSTM_HEADS, MLSTM_HEAD_DIM
    C_zero = jnp.zeros((Bp, N_DIR, H, DH, DH), jnp.float32)
    n_zero = jnp.zeros((Bp, N_DIR, H, DH), jnp.float32)
    m_zero = jnp.zeros((Bp, N_DIR, H), jnp.float32)
    xp = x_prompt
    Cs, ns, ms = [], [], []
    for i in range(DEPTH):
        mod_ctx = (jax.nn.silu(c_ctx) @ w_ada[i] + b_ada[i])[None, None, :]
        xp, C_i, n_i, m_i = block(xp, mod_ctx, 1, C_zero, n_zero, m_zero, *layer_params(i))
        Cs.append(C_i)
        ns.append(n_i)
        ms.append(m_i)
    new_state_C = jnp.stack(Cs, axis=1)
    new_state_n = jnp.stack(ns, axis=1)
    new_state_m = jnp.stack(ms, axis=1)

    rows = x_sample.shape[1] // GRID_W
    xs = x_sample + grid_pos_embed(rows, x_sample.dtype)[None]
    for i in range(DEPTH):
        mod_lat = (jax.nn.silu(c) @ w_ada[i] + b_ada[i])[:, None, :]
        xs, _, _, _ = block(xs, mod_lat, rows, state_C[:, i], state_n[:, i], state_m[:, i], *layer_params(i))

    return (xp, xs, new_state_C, new_state_n, new_state_m)
```

```python
import functools

import numpy as np
import jax
import jax.numpy as jnp
from jax import lax
from jax.experimental import pallas as pl
from jax.experimental.pallas import tpu as pltpu

F32 = jnp.float32
BF16 = jnp.bfloat16

D_MODEL = 1024
DEPTH = 2
GRID_W = 64
FOURIER_GROUPS = 4
FOURIER_WIDTH = D_MODEL // 4
FOURIER_GROUP_DIM = FOURIER_WIDTH // FOURIER_GROUPS
POOL_WINDOWS = (2, 4, 8, 16)
POOL_WIDTH = D_MODEL // 4
POOL_GROUP_DIM = POOL_WIDTH // len(POOL_WINDOWS)
HEADS = 4
MLSTM_WIDTH = D_MODEL // 2
DH = MLSTM_WIDTH // HEADS
N_DIR = 2
N_GATE_COLS = N_DIR * 2 * HEADS
D_FF = -(-8 * D_MODEL // (3 * 256)) * 256
RMS_EPS = 1e-6
POS_BASE = 10000.0

OFF_F = 0
OFF_P = OFF_F + FOURIER_WIDTH
OFF_QKVO = OFF_P + POOL_WIDTH
OFF_G = OFF_QKVO + 4 * MLSTM_WIDTH
OFF_M = OFF_G + N_GATE_COLS
D_IN = OFF_M + 3 * D_MODEL

LANES = 128
MLSTM_CHUNK = 256
POOL_BLOCK = 256
COND_ROWS = 16
VMEM_LIMIT = 60 * 1024 * 1024


def _dot(a, b):
    return jnp.dot(a, b, preferred_element_type=F32)


def _rms(x, g):
    return x * lax.rsqrt(jnp.mean(x * x, axis=-1, keepdims=True) + RMS_EPS) * g


def _log_sigmoid(x):
    return jnp.minimum(x, 0.0) - jnp.log(1.0 + jnp.exp(-jnp.abs(x)))


@functools.lru_cache(maxsize=None)
def _dft_tables(n):
    idx = np.arange(n, dtype=np.int64)
    ang = 2.0 * np.pi * ((idx[:, None] * idx[None, :]) % n).astype(np.float64) / n
    return np.cos(ang).astype(np.float32), np.sin(ang).astype(np.float32)


@functools.lru_cache(maxsize=None)
def _group_dft_tables():
    c, s = _dft_tables(FOURIER_GROUP_DIM)
    bc = np.zeros((FOURIER_WIDTH, FOURIER_WIDTH), np.float32)
    bs = np.zeros((FOURIER_WIDTH, FOURIER_WIDTH), np.float32)
    for g in range(FOURIER_GROUPS):
        sl = slice(g * FOURIER_GROUP_DIM, (g + 1) * FOURIER_GROUP_DIM)
        bc[sl, sl] = c
        bs[sl, sl] = s
    return bc, bs


@functools.lru_cache(maxsize=None)
def _pool_tables(width):
    band = np.zeros((len(POOL_WINDOWS), POOL_BLOCK, POOL_BLOCK), np.float32)
    inv_cnt = np.zeros((POOL_BLOCK, POOL_WIDTH), np.float32)
    for g, w in enumerate(POOL_WINDOWS):
        left = w // 2
        right = w - 1 - left
        for t in range(POOL_BLOCK):
            row, pos = divmod(t, width)
            lo = min(max(pos - left, 0), width - 1)
            hi = min(max(pos + right, 0), width - 1)
            band[g, t, row * width + lo:row * width + hi + 1] = 1.0
            inv_cnt[t, g * POOL_GROUP_DIM:(g + 1) * POOL_GROUP_DIM] = 1.0 / (hi - lo + 1)
    return band, inv_cnt


@functools.lru_cache(maxsize=None)
def _pos_table(rows):
    quarter = D_MODEL // 4
    omega = 1.0 / (POS_BASE ** (np.arange(quarter, dtype=np.float64) / quarter))
    t = np.arange(rows * GRID_W)
    r = (t // GRID_W).astype(np.float64)
    col = (t % GRID_W).astype(np.float64)
    ar = r[:, None] * omega[None, :]
    ac = col[:, None] * omega[None, :]
    return np.concatenate([np.sin(ar), np.cos(ar), np.sin(ac), np.cos(ac)], axis=-1).astype(np.float32)


def _mod_kernel(c_ref, w_ref, b_ref, o_ref):
    c = c_ref[...]
    a = (c * jax.nn.sigmoid(c)).astype(BF16)
    o_ref[...] = _dot(a, w_ref[...].astype(BF16)) + b_ref[...]


def _modulation(cond, w_ada, b_ada):
    tn = 1536
    return pl.pallas_call(
        _mod_kernel,
        name="adaln_mod",
        grid=(DEPTH, 6 * D_MODEL // tn),
        in_specs=[
            pl.BlockSpec((COND_ROWS, D_MODEL), lambda l, j: (0, 0)),
            pl.BlockSpec((None, D_MODEL, tn), lambda l, j: (l, 0, j)),
            pl.BlockSpec((None, 1, tn), lambda l, j: (l, 0, j)),
        ],
        out_specs=pl.BlockSpec((None, COND_ROWS, tn), lambda l, j: (l, 0, j)),
        out_shape=jax.ShapeDtypeStruct((DEPTH, COND_ROWS, 6 * D_MODEL), F32),
        compiler_params=pltpu.CompilerParams(
            dimension_semantics=("arbitrary", "arbitrary"), vmem_limit_bytes=VMEM_LIMIT),
    )(cond, w_ada, b_ada.reshape(DEPTH, 1, 6 * D_MODEL))


def _inproj_kernel(*refs, add_pos):
    it = iter(refs)
    x_ref = next(it)
    pos_ref = next(it) if add_pos else None
    mod_ref, ng_ref, wa_ref, wg_ref, wm_ref = (next(it) for _ in range(5))
    uf_ref, up_ref, qkvo_ref, gate_ref, mg_ref = (next(it) for _ in range(5))
    x0_ref = next(it) if add_pos else None

    x = x_ref[...]
    if add_pos:
        x = x + pos_ref[...]
        x0_ref[...] = x
    h = (_rms(x, ng_ref[0:1, :]) * (1.0 + mod_ref[1:2, :]) + mod_ref[0:1, :]).astype(BF16)
    ck = 512
    za = _dot(h, wa_ref[:, 0:ck])
    uf_ref[...] = za[:, :FOURIER_WIDTH].astype(BF16)
    up_ref[...] = za[:, FOURIER_WIDTH:].astype(BF16)
    for j in range(4 * MLSTM_WIDTH // ck):
        qkvo_ref[:, j * ck:(j + 1) * ck] = _dot(h, wa_ref[:, (j + 1) * ck:(j + 2) * ck]).astype(BF16)
    gate_ref[...] = _dot(h, wg_ref[...])[:, :N_GATE_COLS]
    for j in range(3 * D_MODEL // ck):
        mg_ref[:, j * ck:(j + 1) * ck] = _dot(h, wm_ref[:, j * ck:(j + 1) * ck]).astype(BF16)


def _inproj(x, mod, norm_g, wa, wg, wm, tm, pos=None):
    b, l, _ = x.shape
    add_pos = pos is not None
    per_batch_mod = mod.shape[0] > 1
    tok = lambda w: pl.BlockSpec((None, tm, w), lambda i, j: (i, j, 0))
    full = lambda a: pl.BlockSpec(a.shape, lambda i, j: (0,) * a.ndim)
    in_specs = [tok(D_MODEL)]
    args = [x]
    if add_pos:
        in_specs.append(pl.BlockSpec((tm, D_MODEL), lambda i, j: (j, 0)))
        args.append(pos)
    in_specs += [
        pl.BlockSpec((None, 6, D_MODEL), (lambda i, j: (i, 0, 0)) if per_batch_mod else (lambda i, j: (0, 0, 0))),
        full(norm_g), full(wa), full(wg), full(wm)]
    args += [mod, norm_g, wa, wg, wm]
    out_specs = [tok(FOURIER_WIDTH), tok(POOL_WIDTH), tok(4 * MLSTM_WIDTH), tok(N_GATE_COLS), tok(3 * D_MODEL)]
    out_shape = [
        jax.ShapeDtypeStruct((b, l, FOURIER_WIDTH), BF16),
        jax.ShapeDtypeStruct((b, l, POOL_WIDTH), BF16),
        jax.ShapeDtypeStruct((b, l, 4 * MLSTM_WIDTH), BF16),
        jax.ShapeDtypeStruct((b, l, N_GATE_COLS), F32),
        jax.ShapeDtypeStruct((b, l, 3 * D_MODEL), BF16)]
    if add_pos:
        out_specs.append(tok(D_MODEL))
        out_shape.append(jax.ShapeDtypeStruct((b, l, D_MODEL), F32))
    return pl.pallas_call(
        functools.partial(_inproj_kernel, add_pos=add_pos),
        name="inproj",
        grid=(b, l // tm),
        in_specs=in_specs, out_specs=out_specs, out_shape=out_shape,
        compiler_params=pltpu.CompilerParams(
            dimension_semantics=("arbitrary", "arbitrary"), vmem_limit_bytes=VMEM_LIMIT),
    )(*args)


def _fourier_kernel(u_ref, cl_ref, sl_ref, bc_ref, bs_ref, o_ref, *, scale):
    u = u_ref[...]
    vc = _dot(u, bc_ref[...]).astype(BF16)
    vs = _dot(u, bs_ref[...]).astype(BF16)
    y = _dot(cl_ref[...], vc) - _dot(sl_ref[...], vs)
    o_ref[...] = (y * scale).astype(BF16)


def _fourier(uf):
    b, l, w = uf.shape
    cl, sl = (jnp.asarray(t).astype(BF16) for t in _dft_tables(l))
    bc, bs = (jnp.asarray(t).astype(BF16) for t in _group_dft_tables())
    full = lambda a: pl.BlockSpec(a.shape, lambda i: (0,) * a.ndim)
    seq = pl.BlockSpec((None, l, w), lambda i: (i, 0, 0))
    return pl.pallas_call(
        functools.partial(_fourier_kernel, scale=float((l * FOURIER_GROUP_DIM) ** -0.5)),
        name="fourier",
        grid=(b,),
        in_specs=[seq, full(cl), full(sl), full(bc), full(bs)],
        out_specs=seq,
        out_shape=jax.ShapeDtypeStruct((b, l, w), BF16),
        compiler_params=pltpu.CompilerParams(
            dimension_semantics=("arbitrary",), vmem_limit_bytes=VMEM_LIMIT),
    )(uf, cl, sl, bc, bs)


def _cumsum_lanes(x, reverse):
    n = x.shape[-1]
    lane = lax.broadcasted_iota(jnp.int32, x.shape, x.ndim - 1)
    s = 1
    while s < n:
        if reverse:
            x = x + jnp.where(lane < n - s, pltpu.roll(x, n - s, x.ndim - 1), 0.0)
        else:
            x = x + jnp.where(lane >= s, pltpu.roll(x, s, x.ndim - 1), 0.0)
        s *= 2
    return x


def _mlstm_kernel(*refs, seq, chunk, has_init, emit_state):
    it = iter(refs)
    bias_ref, q_ref, k_ref, v_ref, o_ref, gr_ref = (next(it) for _ in range(6))
    c0_ref, n0_ref, m0_ref = (next(it) for _ in range(3)) if has_init else (None, None, None)
    hm_ref = next(it)
    cs_ref, ns_ref, ms_ref = (next(it) for _ in range(3)) if emit_state else (None, None, None)
    rows_ref, ks_ref, hf_ref, hb_ref = (next(it) for _ in range(4))

    t = chunk
    nc = seq // t
    head = pl.program_id(1)

    ks_ref[...] = (k_ref[...].astype(F32) * (DH ** -0.5)).astype(BF16)
    rows_ref[0] = gr_ref[0] + bias_ref[head]
    rows_ref[1] = _cumsum_lanes(_log_sigmoid(gr_ref[1] + bias_ref[HEADS + head]), reverse=False)
    rows_ref[2] = gr_ref[2] + bias_ref[2 * HEADS + head]
    rows_ref[3] = _cumsum_lanes(_log_sigmoid(gr_ref[3] + bias_ref[3 * HEADS + head]), reverse=True)

    row_i = lax.broadcasted_iota(jnp.int32, (t, t), 0)
    col_i = lax.broadcasted_iota(jnp.int32, (t, t), 1)
    eye = row_i == col_i

    def to_col(r):
        return jnp.sum(jnp.where(eye, r, 0.0), axis=1, keepdims=True)

    def step(c, direction, C, n, m, h_ref):
        rows = pl.ds(pl.multiple_of(c * t, t), t)
        i_r = rows_ref[2 * direction, pl.ds(c, 1), :]
        b_r = rows_ref[2 * direction + 1, pl.ds(c, 1), :]
        if direction == 0:
            b_last = b_r[:, t - 1:t]
            mask = col_i <= row_i
        else:
            b_last = b_r[:, 0:1]
            mask = col_i >= row_i
        b_c = to_col(b_r)
        i_c = to_col(i_r)
        qc = q_ref[rows, :]
        kc = ks_ref[rows, :]
        vc = v_ref[rows, :]
        log_d = jnp.where(mask, b_c - b_r + i_r, -jnp.inf)
        state_log = b_c + m
        m_t = jnp.maximum(state_log, jnp.max(log_d, axis=1, keepdims=True))
        d = jnp.exp(log_d - m_t)
        sdec = jnp.exp(state_log - m_t)
        s = lax.dot_general(qc, kc, (((1,), (1,)), ((), ())), preferred_element_type=F32) * d
        num = sdec * _dot(qc, C.astype(BF16)) + _dot(s.astype(BF16), vc)
        den = sdec * jnp.sum(qc.astype(F32) * n, axis=1, keepdims=True) + jnp.sum(s, axis=1, keepdims=True)
        h_ref[rows, :] = num * (1.0 / jnp.maximum(jnp.abs(den), jnp.exp(-m_t)))
        m_new = jnp.maximum(b_last + m, jnp.max(b_last - b_r + i_r, axis=1, keepdims=True))
        w_c = jnp.exp(b_last - b_c + i_c - m_new)
        carry = jnp.exp(b_last + m - m_new)
        wv = (w_c * vc.astype(F32)).astype(BF16)
        C_new = carry * C + lax.dot_general(kc, wv, (((0,), (0,)), ((), ())), preferred_element_type=F32)
        n_new = carry * n + jnp.sum(w_c * kc.astype(F32), axis=0, keepdims=True)
        return C_new, n_new, m_new

    def body(ci, carry):
        Cf, nf, mf, Cb, nb, mb = carry
        Cf, nf, mf = step(ci, 0, Cf, nf, mf, hf_ref)
        Cb, nb, mb = step(nc - 1 - ci, 1, Cb, nb, mb, hb_ref)
        return Cf, nf, mf, Cb, nb, mb

    if has_init:
        init = (c0_ref[0], n0_ref[0], m0_ref[0][:, 0:1], c0_ref[1], n0_ref[1], m0_ref[1][:, 0:1])
    else:
        zc, zn, zm = jnp.zeros((DH, DH), F32), jnp.zeros((1, DH), F32), jnp.zeros((1, 1), F32)
        init = (zc, zn, zm, zc, zn, zm)
    Cf, nf, mf, Cb, nb, mb = lax.fori_loop(0, nc, body, init)

    og = jax.nn.sigmoid(o_ref[...].astype(F32))
    hm_ref[...] = (og * (hf_ref[...] + hb_ref[...])).astype(BF16)
    if emit_state:
        cs_ref[0] = Cf
        cs_ref[1] = Cb
        ns_ref[0] = nf
        ns_ref[1] = nb
        ms_ref[0] = jnp.broadcast_to(mf, (1, LANES))
        ms_ref[1] = jnp.broadcast_to(mb, (1, LANES))


def _mlstm(qkvo, gates, b_gate, init, emit_state):
    b, l, _ = qkvo.shape
    t = min(MLSTM_CHUNK, l)
    nc = l // t
    gr = gates.reshape(b, nc, t, N_DIR, 2, HEADS).transpose(0, 5, 3, 4, 1, 2).reshape(b, HEADS, 2 * N_DIR, nc, t)
    has_init = init is not None
    head_cols = lambda k: pl.BlockSpec((None, l, DH), lambda i, h: (i, 0, k * HEADS + h))
    state_c = pl.BlockSpec((None, N_DIR, None, DH, DH), lambda i, h: (i, 0, h, 0, 0))
    state_v = pl.BlockSpec((None, N_DIR, None, 1, LANES), lambda i, h: (i, 0, h, 0, 0))
    in_specs = [
        pl.BlockSpec(memory_space=pltpu.SMEM),
        head_cols(0), head_cols(1), head_cols(2), head_cols(3),
        pl.BlockSpec((None, None, 2 * N_DIR, nc, t), lambda i, h: (i, h, 0, 0, 0))]
    args = [b_gate.reshape(N_GATE_COLS), qkvo, qkvo, qkvo, qkvo, gr]
    if has_init:
        c0, n0, m0 = init
        in_specs += [state_c, state_v, state_v]
        args += [c0, n0.reshape(b, N_DIR, HEADS, 1, DH),
                 jnp.broadcast_to(m0[..., None, None], (b, N_DIR, HEADS, 1, LANES))]
    out_specs = [pl.BlockSpec((None, l, DH), lambda i, h: (i, 0, h))]
    out_shape = [jax.ShapeDtypeStruct((b, l, MLSTM_WIDTH), BF16)]
    if emit_state:
        out_specs += [state_c, state_v, state_v]
        out_shape += [jax.ShapeDtypeStruct((b, N_DIR, HEADS, DH, DH), F32),
                      jax.ShapeDtypeStruct((b, N_DIR, HEADS, 1, DH), F32),
                      jax.ShapeDtypeStruct((b, N_DIR, HEADS, 1, LANES), F32)]
    return pl.pallas_call(
        functools.partial(_mlstm_kernel, seq=l, chunk=t, has_init=has_init, emit_state=emit_state),
        name="mlstm",
        grid=(b, HEADS),
        in_specs=in_specs, out_specs=out_specs, out_shape=out_shape,
        scratch_shapes=[
            pltpu.VMEM((2 * N_DIR, nc, t), F32),
            pltpu.VMEM((l, DH), BF16),
            pltpu.VMEM((l, DH), F32),
            pltpu.VMEM((l, DH), F32)],
        compiler_params=pltpu.CompilerParams(
            dimension_semantics=("arbitrary", "arbitrary"), vmem_limit_bytes=VMEM_LIMIT),
    )(*args)


def _merge_ffn_kernel(x_ref, mod_ref, ng_ref, yf_ref, up_ref, hm_ref, mg_ref, band_ref, icnt_ref,
                      wpool_ref, pscale_ref, wpf_ref, wpp_ref, wpm_ref, wout_ref, wfi_ref, wfo_ref,
                      o_ref, *, tm):
    x = x_ref[...]

    lane_group = lax.broadcasted_iota(jnp.int32, (POOL_BLOCK, POOL_WIDTH), 1) // POOL_GROUP_DIM
    pooled = []
    for r in range(tm // POOL_BLOCK):
        u = up_ref[r * POOL_BLOCK:(r + 1) * POOL_BLOCK, :]
        acc = jnp.zeros((POOL_BLOCK, POOL_WIDTH), F32)
        for g in range(len(POOL_WINDOWS)):
            acc = jnp.where(lane_group == g, _dot(band_ref[g], u), acc)
        pooled.append((acc * icnt_ref[...] - u.astype(F32)).astype(BF16))
    p = pooled[0] if len(pooled) == 1 else jnp.concatenate(pooled, axis=0)
    pp = (_dot(p, wpool_ref[...]) * pscale_ref[...]).astype(BF16)

    y_f = _dot(yf_ref[...], wpf_ref[...])
    y_p = _dot(pp, wpp_ref[...])
    y_m = _dot(hm_ref[...], wpm_ref[...])
    g_f = jax.nn.sigmoid(mg_ref[:, 0:D_MODEL].astype(F32))
    g_p = jax.nn.sigmoid(mg_ref[:, D_MODEL:2 * D_MODEL].astype(F32))
    g_m = jax.nn.sigmoid(mg_ref[:, 2 * D_MODEL:3 * D_MODEL].astype(F32))
    y = (g_f * y_f + g_p * y_p + g_m * y_m).astype(BF16)
    x1 = x + mod_ref[2:3, :] * _rms(_dot(y, wout_ref[...]), ng_ref[1:2, :])

    h2 = (_rms(x1, ng_ref[2:3, :]) * (1.0 + mod_ref[4:5, :]) + mod_ref[3:4, :]).astype(BF16)
    ck = D_FF // 2
    acc = jnp.zeros((tm, D_MODEL), F32)
    for j in range(D_FF // ck):
        a = _dot(h2, wfi_ref[:, j * ck:(j + 1) * ck])
        bb = _dot(h2, wfi_ref[:, D_FF + j * ck:D_FF + (j + 1) * ck])
        act = (a * jax.nn.sigmoid(a) * bb).astype(BF16)
        acc = acc + _dot(act, wfo_ref[j * ck:(j + 1) * ck, :])
    o_ref[...] = x1 + mod_ref[5:6, :] * _rms(acc, ng_ref[3:4, :])


def _merge_ffn(x, mod, norm_g, yf, up, hm, mg, band, icnt, wpool, pscale, wpf, wpp, wpm, wout, wfi, wfo, tm):
    b, l, _ = x.shape
    per_batch_mod = mod.shape[0] > 1
    tok = lambda w: pl.BlockSpec((None, tm, w), lambda i, j: (i, j, 0))
    full = lambda a: pl.BlockSpec(a.shape, lambda i, j: (0,) * a.ndim)
    consts = [band, icnt, wpool, pscale, wpf, wpp, wpm, wout, wfi, wfo]
    in_specs = [
        tok(D_MODEL),
        pl.BlockSpec((None, 6, D_MODEL), (lambda i, j: (i, 0, 0)) if per_batch_mod else (lambda i, j: (0, 0, 0))),
        full(norm_g), tok(FOURIER_WIDTH), tok(POOL_WIDTH), tok(MLSTM_WIDTH), tok(3 * D_MODEL)]
    in_specs += [full(a) for a in consts]
    return pl.pallas_call(
        functools.partial(_merge_ffn_kernel, tm=tm),
        name="merge_ffn",
        grid=(b, l // tm),
        in_specs=in_specs,
        out_specs=tok(D_MODEL),
        out_shape=jax.ShapeDtypeStruct((b, l, D_MODEL), F32),
        compiler_params=pltpu.CompilerParams(
            dimension_semantics=("arbitrary", "arbitrary"), vmem_limit_bytes=VMEM_LIMIT),
    )(x, mod, norm_g, yf, up, hm, mg, *consts)


def _layer_weights(i, norm_g, w_in, b_gate, w_proj_f, w_pool, pool_scale, w_proj_p, w_proj_m, w_out, w_ffn_in, w_ffn_out):
    w = w_in[i]
    wg = jnp.pad(w[:, OFF_G:OFF_M], ((0, 0), (0, LANES - N_GATE_COLS)))
    return dict(
        norm_g=norm_g[i],
        wa=w[:, :OFF_G].astype(BF16), wg=wg.astype(BF16), wm=w[:, OFF_M:].astype(BF16),
        b_gate=b_gate[i],
        wpool=jax.scipy.linalg.block_diag(*w_pool[i]).astype(BF16),
        pscale=pool_scale[i].reshape(1, POOL_WIDTH),
        wpf=w_proj_f[i].astype(BF16), wpp=w_proj_p[i].astype(BF16), wpm=w_proj_m[i].astype(BF16),
        wout=w_out[i].astype(BF16), wfi=w_ffn_in[i].astype(BF16), wfo=w_ffn_out[i].astype(BF16))


def _block(x, mod, lw, width, tm, init, emit_state, pos=None):
    outs = _inproj(x, mod, lw["norm_g"], lw["wa"], lw["wg"], lw["wm"], tm, pos)
    uf, up, qkvo, gates, mg = outs[:5]
    if pos is not None:
        x = outs[5]
    yf = _fourier(uf)
    mouts = _mlstm(qkvo, gates, lw["b_gate"], init, emit_state)
    band, icnt = _pool_tables(width)
    x = _merge_ffn(x, mod, lw["norm_g"], yf, up, mouts[0], mg, jnp.asarray(band).astype(BF16), jnp.asarray(icnt),
                   lw["wpool"], lw["pscale"], lw["wpf"], lw["wpp"], lw["wpm"], lw["wout"], lw["wfi"], lw["wfo"], tm)
    return x, mouts[1:]


def kernel(x_prompt, x_sample, state_C, state_n, state_m, c, c_ctx, w_ada, b_ada, norm_g, w_in, b_gate, w_proj_f, w_pool, pool_scale, w_proj_p, w_proj_m, w_out, w_ffn_in, w_ffn_out):
    bp, lp, _ = x_prompt.shape
    bs, ls, _ = x_sample.shape
    cond = jnp.concatenate([c_ctx[None, :], c, jnp.zeros((COND_ROWS - 1 - bs, D_MODEL), F32)], axis=0)
    mod = _modulation(cond, w_ada, b_ada).reshape(DEPTH, COND_ROWS, 6, D_MODEL)
    weights = [_layer_weights(i, norm_g, w_in, b_gate, w_proj_f, w_pool, pool_scale, w_proj_p, w_proj_m,
                              w_out, w_ffn_in, w_ffn_out) for i in range(DEPTH)]

    xp = x_prompt
    cs, ns, ms = [], [], []
    for i in range(DEPTH):
        xp, (c_i, n_i, m_i) = _block(xp, mod[i, 0:1], weights[i], lp, min(lp, 512), None, True)
        cs.append(c_i)
        ns.append(n_i[:, :, :, 0, :])
        ms.append(m_i[:, :, :, 0, 0])

    xs = x_sample
    pos = jnp.asarray(_pos_table(ls // GRID_W))
    for i in range(DEPTH):
        init = (state_C[:, i], state_n[:, i], state_m[:, i])
        xs, _ = _block(xs, mod[i, 1:1 + bs], weights[i], GRID_W, 512, init, False, pos if i == 0 else None)

    return (xp, xs, jnp.stack(cs, axis=1), jnp.stack(ns, axis=1), jnp.stack(ms, axis=1))
```

```python
import functools

import numpy as np
import jax
import jax.numpy as jnp
from jax import lax
from jax.experimental import pallas as pl
from jax.experimental.pallas import tpu as pltpu

F32 = jnp.float32
BF16 = jnp.bfloat16

D_MODEL = 1024
DEPTH = 2
GRID_W = 64
FOURIER_GROUPS = 4
FOURIER_WIDTH = D_MODEL // 4
FOURIER_GROUP_DIM = FOURIER_WIDTH // FOURIER_GROUPS
POOL_WINDOWS = (2, 4, 8, 16)
POOL_WIDTH = D_MODEL // 4
POOL_GROUP_DIM = POOL_WIDTH // len(POOL_WINDOWS)
HEADS = 4
MLSTM_WIDTH = D_MODEL // 2
DH = MLSTM_WIDTH // HEADS
N_DIR = 2
N_GATE_COLS = N_DIR * 2 * HEADS
D_FF = -(-8 * D_MODEL // (3 * 256)) * 256
RMS_EPS = 1e-6
POS_BASE = 10000.0

OFF_F = 0
OFF_P = OFF_F + FOURIER_WIDTH
OFF_QKVO = OFF_P + POOL_WIDTH
OFF_G = OFF_QKVO + 4 * MLSTM_WIDTH
OFF_M = OFF_G + N_GATE_COLS
D_IN = OFF_M + 3 * D_MODEL

LANES = 128
MLSTM_CHUNK = 256
POOL_BLOCK = 256
COND_ROWS = 16
VMEM_LIMIT = 60 * 1024 * 1024


def _dot(a, b):
    return jnp.dot(a, b, preferred_element_type=F32)


def _rms(x, g):
    return x * lax.rsqrt(jnp.mean(x * x, axis=-1, keepdims=True) + RMS_EPS) * g


def _log_sigmoid(x):
    return jnp.minimum(x, 0.0) - jnp.log(1.0 + jnp.exp(-jnp.abs(x)))


@functools.lru_cache(maxsize=None)
def _dft_tables(n):
    idx = np.arange(n, dtype=np.int64)
    ang = 2.0 * np.pi * ((idx[:, None] * idx[None, :]) % n).astype(np.float64) / n
    return np.cos(ang).astype(np.float32), np.sin(ang).astype(np.float32)


@functools.lru_cache(maxsize=None)
def _group_dft_tables():
    c, s = _dft_tables(FOURIER_GROUP_DIM)
    bc = np.zeros((FOURIER_WIDTH, FOURIER_WIDTH), np.float32)
    bs = np.zeros((FOURIER_WIDTH, FOURIER_WIDTH), np.float32)
    for g in range(FOURIER_GROUPS):
        sl = slice(g * FOURIER_GROUP_DIM, (g + 1) * FOURIER_GROUP_DIM)
        bc[sl, sl] = c
        bs[sl, sl] = s
    return bc, bs


@functools.lru_cache(maxsize=None)
def _pool_tables(width):
    band = np.zeros((len(POOL_WINDOWS), POOL_BLOCK, POOL_BLOCK), np.float32)
    inv_cnt = np.zeros((POOL_BLOCK, POOL_WIDTH), np.float32)
    for g, w in enumerate(POOL_WINDOWS):
        left = w // 2
        right = w - 1 - left
        for t in range(POOL_BLOCK):
            row, pos = divmod(t, width)
            lo = min(max(pos - left, 0), width - 1)
            hi = min(max(pos + right, 0), width - 1)
            band[g, t, row * width + lo:row * width + hi + 1] = 1.0
            inv_cnt[t, g * POOL_GROUP_DIM:(g + 1) * POOL_GROUP_DIM] = 1.0 / (hi - lo + 1)
    return band, inv_cnt


@functools.lru_cache(maxsize=None)
def _pos_table(rows):
    quarter = D_MODEL // 4
    omega = 1.0 / (POS_BASE ** (np.arange(quarter, dtype=np.float64) / quarter))
    t = np.arange(rows * GRID_W)
    r = (t // GRID_W).astype(np.float64)
    col = (t % GRID_W).astype(np.float64)
    ar = r[:, None] * omega[None, :]
    ac = col[:, None] * omega[None, :]
    return np.concatenate([np.sin(ar), np.cos(ar), np.sin(ac), np.cos(ac)], axis=-1).astype(np.float32)


def _mod_kernel(c_ref, w_ref, b_ref, o_ref):
    c = c_ref[...]
    a = (c * jax.nn.sigmoid(c)).astype(BF16)
    o_ref[...] = _dot(a, w_ref[...].astype(BF16)) + b_ref[...]


def _modulation(cond, w_ada, b_ada):
    tn = 1536
    return pl.pallas_call(
        _mod_kernel,
        name="adaln_mod",
        grid=(DEPTH, 6 * D_MODEL // tn),
        in_specs=[
            pl.BlockSpec((COND_ROWS, D_MODEL), lambda l, j: (0, 0)),
            pl.BlockSpec((None, D_MODEL, tn), lambda l, j: (l, 0, j)),
            pl.BlockSpec((None, 1, tn), lambda l, j: (l, 0, j)),
        ],
        out_specs=pl.BlockSpec((None, COND_ROWS, tn), lambda l, j: (l, 0, j)),
        out_shape=jax.ShapeDtypeStruct((DEPTH, COND_ROWS, 6 * D_MODEL), F32),
        compiler_params=pltpu.CompilerParams(
            dimension_semantics=("arbitrary", "arbitrary"), vmem_limit_bytes=VMEM_LIMIT),
    )(cond, w_ada, b_ada.reshape(DEPTH, 1, 6 * D_MODEL))


def _gate_rows(g, bias_col, rows_ref, chunk):
    x = g.T[:N_GATE_COLS, :] + bias_col
    row = lax.broadcasted_iota(jnp.int32, x.shape, 0)
    fwd = row < 2 * HEADS
    lf = _log_sigmoid(x)
    b_f = jnp.where(fwd, _scan_lanes(lf, False, jnp.add, 0.0), _scan_lanes(lf, True, jnp.add, 0.0))
    b_i = pltpu.roll(b_f, N_GATE_COLS - HEADS, 0)
    a = x - b_i
    p = jnp.where(fwd, _scan_lanes(a, False, jnp.maximum, -jnp.inf), _scan_lanes(a, True, jnp.maximum, -jnp.inf))
    sl = slice(chunk * N_GATE_COLS, (chunk + 1) * N_GATE_COLS)
    rows_ref[0, sl, :] = a
    rows_ref[1, sl, :] = b_i
    rows_ref[2, sl, :] = b_i + p
    rows_ref[3, sl, :] = p


def _inproj_kernel(*refs, add_pos, chunk):
    it = iter(refs)
    x_ref = next(it)
    pos_ref = next(it) if add_pos else None
    mod_ref, ng_ref, wa_ref, wg_ref, bg_ref, wm_ref = (next(it) for _ in range(6))
    uf_ref, up_ref, qkvo_ref, rows_ref, mg_ref = (next(it) for _ in range(5))
    x0_ref = next(it) if add_pos else None

    x = x_ref[...]
    if add_pos:
        x = x + pos_ref[...]
        x0_ref[...] = x
    h = (_rms(x, ng_ref[0:1, :]) * (1.0 + mod_ref[1:2, :]) + mod_ref[0:1, :]).astype(BF16)
    ck = 512
    za = _dot(h, wa_ref[:, 0:ck])
    uf_ref[...] = za[:, :FOURIER_WIDTH].astype(BF16)
    up_ref[...] = za[:, FOURIER_WIDTH:].astype(BF16)
    for j in range(4 * MLSTM_WIDTH // ck):
        qkvo_ref[:, j * ck:(j + 1) * ck] = _dot(h, wa_ref[:, (j + 1) * ck:(j + 2) * ck]).astype(BF16)
    g = _dot(h, wg_ref[...])
    for c in range(g.shape[0] // chunk):
        _gate_rows(g[c * chunk:(c + 1) * chunk, :], bg_ref[...], rows_ref, c)
    for j in range(3 * D_MODEL // ck):
        mg_ref[:, j * ck:(j + 1) * ck] = _dot(h, wm_ref[:, j * ck:(j + 1) * ck]).astype(BF16)


def _inproj(x, mod, norm_g, wa, wg, b_gate, wm, tm, pos=None):
    b, l, _ = x.shape
    add_pos = pos is not None
    t = min(MLSTM_CHUNK, l)
    assert tm % t == 0 and l % tm == 0
    bias_col = b_gate.reshape(N_GATE_COLS, 1)
    per_batch_mod = mod.shape[0] > 1
    tok = lambda w: pl.BlockSpec((None, tm, w), lambda i, j: (i, j, 0))
    full = lambda a: pl.BlockSpec(a.shape, lambda i, j: (0,) * a.ndim)
    in_specs = [tok(D_MODEL)]
    args = [x]
    if add_pos:
        in_specs.append(pl.BlockSpec((tm, D_MODEL), lambda i, j: (j, 0)))
        args.append(pos)
    in_specs += [
        pl.BlockSpec((None, 6, D_MODEL), (lambda i, j: (i, 0, 0)) if per_batch_mod else (lambda i, j: (0, 0, 0))),
        full(norm_g), full(wa), full(wg), full(bias_col), full(wm)]
    args += [mod, norm_g, wa, wg, bias_col, wm]
    rows_per_tile = tm // t * N_GATE_COLS
    out_specs = [tok(FOURIER_WIDTH), tok(POOL_WIDTH), tok(4 * MLSTM_WIDTH),
                 pl.BlockSpec((None, 4, rows_per_tile, t), lambda i, j: (i, 0, j, 0)), tok(3 * D_MODEL)]
    out_shape = [
        jax.ShapeDtypeStruct((b, l, FOURIER_WIDTH), BF16),
        jax.ShapeDtypeStruct((b, l, POOL_WIDTH), BF16),
        jax.ShapeDtypeStruct((b, l, 4 * MLSTM_WIDTH), BF16),
        jax.ShapeDtypeStruct((b, 4, l // t * N_GATE_COLS, t), F32),
        jax.ShapeDtypeStruct((b, l, 3 * D_MODEL), BF16)]
    if add_pos:
        out_specs.append(tok(D_MODEL))
        out_shape.append(jax.ShapeDtypeStruct((b, l, D_MODEL), F32))
    return pl.pallas_call(
        functools.partial(_inproj_kernel, add_pos=add_pos, chunk=t),
        name="inproj",
        grid=(b, l // tm),
        in_specs=in_specs, out_specs=out_specs, out_shape=out_shape,
        compiler_params=pltpu.CompilerParams(
            dimension_semantics=("arbitrary", "arbitrary"), vmem_limit_bytes=VMEM_LIMIT),
    )(*args)


def _fourier_kernel(u_ref, cl_ref, sl_ref, bc_ref, bs_ref, o_ref, *, scale):
    u = u_ref[...]
    vc = _dot(u, bc_ref[...]).astype(BF16)
    vs = _dot(u, bs_ref[...]).astype(BF16)
    y = _dot(cl_ref[...], vc) - _dot(sl_ref[...], vs)
    o_ref[...] = (y * scale).astype(BF16)


def _fourier(uf):
    b, l, w = uf.shape
    cl, sl = (jnp.asarray(t).astype(BF16) for t in _dft_tables(l))
    bc, bs = (jnp.asarray(t).astype(BF16) for t in _group_dft_tables())
    full = lambda a: pl.BlockSpec(a.shape, lambda i: (0,) * a.ndim)
    seq = pl.BlockSpec((None, l, w), lambda i: (i, 0, 0))
    return pl.pallas_call(
        functools.partial(_fourier_kernel, scale=float((l * FOURIER_GROUP_DIM) ** -0.5)),
        name="fourier",
        grid=(b,),
        in_specs=[seq, full(cl), full(sl), full(bc), full(bs)],
        out_specs=seq,
        out_shape=jax.ShapeDtypeStruct((b, l, w), BF16),
        compiler_params=pltpu.CompilerParams(
            dimension_semantics=("arbitrary",), vmem_limit_bytes=VMEM_LIMIT),
    )(uf, cl, sl, bc, bs)


def _scan_lanes(x, reverse, op, fill):
    n = x.shape[-1]
    lane = lax.broadcasted_iota(jnp.int32, x.shape, x.ndim - 1)
    s = 1
    while s < n:
        if reverse:
            x = op(x, jnp.where(lane < n - s, pltpu.roll(x, n - s, x.ndim - 1), fill))
        else:
            x = op(x, jnp.where(lane >= s, pltpu.roll(x, s, x.ndim - 1), fill))
        s *= 2
    return x


SPLIT_ROWS = 16


def _mlstm_kernel(*refs, seq, chunk, hp, has_init, emit_state, unroll):
    it = iter(refs)
    q_ref, k_ref, v_ref, rows_ref = (next(it) for _ in range(4))
    c0_ref, n0_ref, m0_ref = (next(it) for _ in range(3)) if has_init else (None, None, None)
    hm_ref = next(it)
    cs_ref, ns_ref, ms_ref = (next(it) for _ in range(3)) if emit_state else (None, None, None)
    hf_ref, hb_ref = (next(it) for _ in range(2))

    t = chunk
    nc = seq // t
    rep = t // DH
    scale = DH ** -0.5

    row_i = lax.broadcasted_iota(jnp.int32, (t, t), 0)
    col_i = lax.broadcasted_iota(jnp.int32, (t, t), 1)
    eye_dh = jnp.where(lax.broadcasted_iota(jnp.int32, (DH, DH), 0) == lax.broadcasted_iota(jnp.int32, (DH, DH), 1),
                       1.0, 0.0).astype(BF16)
    sub = lax.broadcasted_iota(jnp.int32, (SPLIT_ROWS, t), 0)
    part = sub % 3
    p_sub = lax.broadcasted_iota(jnp.int32, (SPLIT_ROWS, 2 * DH), 0)
    p_lane = lax.broadcasted_iota(jnp.int32, (SPLIT_ROWS, 2 * DH), 1)
    gather_mat = jnp.where(((p_sub < 3) & (p_lane < DH)) | ((p_sub >= 3) & (p_sub < 6) & (p_lane >= DH)),
                           1.0, 0.0).astype(BF16)
    ones_cols = jnp.ones((t, DH), BF16)

    def step(c, hh, direction, Cn, m, h_ref):
        rows = pl.ds(pl.multiple_of(c * t, t), t)
        cols = slice(hh * DH, (hh + 1) * DH)
        r = pl.ds(c * N_GATE_COLS + direction * 2 * HEADS + pl.program_id(1) * hp + hh, 1)
        a_r = rows_ref[0, r, :]
        b_r = rows_ref[1, r, :]
        g_r = rows_ref[2, r, :]
        p_r = rows_ref[3, r, :]
        if direction == 0:
            end = t - 1
            mask = col_i <= row_i
        else:
            end = 0
            mask = col_i >= row_i
        b_end = b_r[:, end:end + 1]
        p_end = p_r[:, end:end + 1]

        x0 = jnp.where(sub < 3, b_r, jnp.where(sub < 6, g_r, 0.0))
        x1 = x0 - x0.astype(BF16).astype(F32)
        x2 = x1 - x1.astype(BF16).astype(F32)
        xs = jnp.where(part == 0, x0, jnp.where(part == 1, x1, x2)).astype(BF16)
        bg = lax.dot_general(xs, gather_mat, (((0,), (0,)), ((), ())), preferred_element_type=F32)
        b_c = bg[:, :DH]
        m_t = jnp.maximum(b_c + m, bg[:, DH:])
        mu = m_t - b_c
        sdec = jnp.exp(m - mu)

        qc = q_ref[rows, cols]
        kc = k_ref[rows, cols]
        v1 = jnp.concatenate([v_ref[rows, cols], ones_cols], axis=1)
        mu_t = jnp.concatenate([mu] * rep, axis=1)
        d = jnp.where(mask, jnp.exp((a_r + np.float32(np.log(scale))) - mu_t), 0.0)
        s = lax.dot_general(qc, kc, (((1,), (1,)), ((), ())), preferred_element_type=F32) * d
        qs = _dot(qc, Cn.astype(BF16))
        sv = _dot(s.astype(BF16), v1)
        num = sdec * qs[:, :DH] + sv[:, :DH]
        den = sdec * qs[:, DH:] + sv[:, DH:]
        h_ref[rows, cols] = num * (1.0 / jnp.maximum(jnp.abs(den), jnp.exp(-m_t)))

        mx = jnp.maximum(m, p_end)
        w_r = jnp.exp(a_r - mx) * scale
        kt = lax.dot_general(eye_dh, kc, (((1,), (1,)), ((), ())), preferred_element_type=F32)
        Cn_new = jnp.exp(m - mx) * Cn + _dot((kt * w_r).astype(BF16), v1)
        return Cn_new, b_end + mx

    def body(ci, carry):
        out = []
        for hh in range(hp):
            Cf, mf, Cb, mb = carry[4 * hh:4 * hh + 4]
            Cf, mf = step(ci, hh, 0, Cf, mf, hf_ref)
            Cb, mb = step(nc - 1 - ci, hh, 1, Cb, mb, hb_ref)
            out += [Cf, mf, Cb, mb]
        return tuple(out)

    init = []
    for hh in range(hp):
        for d in range(N_DIR):
            if has_init:
                n_col = jnp.sum(eye_dh.astype(F32) * n0_ref[d, hh], axis=1, keepdims=True)
                init += [jnp.concatenate([c0_ref[d, hh], jnp.broadcast_to(n_col, (DH, DH))], axis=1),
                         m0_ref[d, hh][:, 0:1]]
            else:
                init += [jnp.zeros((DH, 2 * DH), F32), jnp.zeros((1, 1), F32)]
    final = lax.fori_loop(0, nc, body, tuple(init), unroll=unroll)

    hm_ref[...] = (hf_ref[...] + hb_ref[...]).astype(BF16)
    if emit_state:
        eye_f = eye_dh.astype(F32)
        for hh in range(hp):
            for d in range(N_DIR):
                Cn, m = final[2 * (hh * N_DIR + d):2 * (hh * N_DIR + d) + 2]
                cs_ref[d, hh] = Cn[:, :DH]
                ns_ref[d, hh] = jnp.sum(eye_f * Cn[:, DH:], axis=0, keepdims=True)
                ms_ref[d, hh] = jnp.broadcast_to(m, (1, LANES))


def _mlstm(qkvo, grows, init, emit_state, hp, unroll):
    b, l, _ = qkvo.shape
    t = grows.shape[-1]
    nc = l // t
    assert l % t == 0 and t % DH == 0 and HEADS % hp == 0 and nc % unroll == 0
    has_init = init is not None
    groups = HEADS // hp
    head_cols = lambda k: pl.BlockSpec((None, l, hp * DH), lambda i, h: (i, 0, k * groups + h))
    state_c = pl.BlockSpec((None, N_DIR, hp, DH, DH), lambda i, h: (i, 0, h, 0, 0))
    state_v = pl.BlockSpec((None, N_DIR, hp, 1, LANES), lambda i, h: (i, 0, h, 0, 0))
    in_specs = [
        head_cols(0), head_cols(1), head_cols(2),
        pl.BlockSpec((None,) + grows.shape[1:], lambda i, h: (i, 0, 0, 0))]
    args = [qkvo, qkvo, qkvo, grows]
    if has_init:
        c0, n0, m0 = init
        in_specs += [state_c, state_v, state_v]
        args += [c0, n0.reshape(b, N_DIR, HEADS, 1, DH),
                 jnp.broadcast_to(m0[..., None, None], (b, N_DIR, HEADS, 1, LANES))]
    out_specs = [pl.BlockSpec((None, l, hp * DH), lambda i, h: (i, 0, h))]
    out_shape = [jax.ShapeDtypeStruct((b, l, MLSTM_WIDTH), BF16)]
    if emit_state:
        out_specs += [state_c, state_v, state_v]
        out_shape += [jax.ShapeDtypeStruct((b, N_DIR, HEADS, DH, DH), F32),
                      jax.ShapeDtypeStruct((b, N_DIR, HEADS, 1, DH), F32),
                      jax.ShapeDtypeStruct((b, N_DIR, HEADS, 1, LANES), F32)]
    return pl.pallas_call(
        functools.partial(_mlstm_kernel, seq=l, chunk=t, hp=hp, has_init=has_init, emit_state=emit_state,
                          unroll=unroll),
        name="mlstm",
        grid=(b, groups),
        in_specs=in_specs, out_specs=out_specs, out_shape=out_shape,
        scratch_shapes=[
            pltpu.VMEM((l, hp * DH), F32),
            pltpu.VMEM((l, hp * DH), F32)],
        compiler_params=pltpu.CompilerParams(
            dimension_semantics=("arbitrary", "arbitrary"), vmem_limit_bytes=VMEM_LIMIT),
    )(*args)


def _merge_ffn_kernel(x_ref, mod_ref, ng_ref, yf_ref, up_ref, hm_ref, og_ref, mg_ref, band_ref, icnt_ref,
                      wpool_ref, pscale_ref, wpf_ref, wpp_ref, wpm_ref, wout_ref, wfi_ref, wfo_ref,
                      o_ref, *, tm):
    x = x_ref[...]

    lane_group = lax.broadcasted_iota(jnp.int32, (POOL_BLOCK, POOL_WIDTH), 1) // POOL_GROUP_DIM
    pooled = []
    for r in range(tm // POOL_BLOCK):
        u = up_ref[r * POOL_BLOCK:(r + 1) * POOL_BLOCK, :]
        acc = jnp.zeros((POOL_BLOCK, POOL_WIDTH), F32)
        for g in range(len(POOL_WINDOWS)):
            acc = jnp.where(lane_group == g, _dot(band_ref[g], u), acc)
        pooled.append((acc * icnt_ref[...] - u.astype(F32)).astype(BF16))
    p = pooled[0] if len(pooled) == 1 else jnp.concatenate(pooled, axis=0)
    pp = (_dot(p, wpool_ref[...]) * pscale_ref[...]).astype(BF16)

    y_f = _dot(yf_ref[...], wpf_ref[...])
    y_p = _dot(pp, wpp_ref[...])
    hg = (jax.nn.sigmoid(og_ref[...].astype(F32)) * hm_ref[...].astype(F32)).astype(BF16)
    y_m = _dot(hg, wpm_ref[...])
    g_f = jax.nn.sigmoid(mg_ref[:, 0:D_MODEL].astype(F32))
    g_p = jax.nn.sigmoid(mg_ref[:, D_MODEL:2 * D_MODEL].astype(F32))
    g_m = jax.nn.sigmoid(mg_ref[:, 2 * D_MODEL:3 * D_MODEL].astype(F32))
    y = (g_f * y_f + g_p * y_p + g_m * y_m).astype(BF16)
    x1 = x + mod_ref[2:3, :] * _rms(_dot(y, wout_ref[...]), ng_ref[1:2, :])

    h2 = (_rms(x1, ng_ref[2:3, :]) * (1.0 + mod_ref[4:5, :]) + mod_ref[3:4, :]).astype(BF16)
    ck = D_FF // 2
    acc = jnp.zeros((tm, D_MODEL), F32)
    for j in range(D_FF // ck):
        a = _dot(h2, wfi_ref[:, j * ck:(j + 1) * ck])
        bb = _dot(h2, wfi_ref[:, D_FF + j * ck:D_FF + (j + 1) * ck])
        act = (a * jax.nn.sigmoid(a) * bb).astype(BF16)
        acc = acc + _dot(act, wfo_ref[j * ck:(j + 1) * ck, :])
    o_ref[...] = x1 + mod_ref[5:6, :] * _rms(acc, ng_ref[3:4, :])


def _merge_ffn(x, mod, norm_g, yf, up, hm, qkvo, mg, band, icnt, wpool, pscale, wpf, wpp, wpm, wout, wfi, wfo, tm):
    b, l, _ = x.shape
    ogate = pl.BlockSpec((None, tm, MLSTM_WIDTH), lambda i, j: (i, j, 3))
    per_batch_mod = mod.shape[0] > 1
    tok = lambda w: pl.BlockSpec((None, tm, w), lambda i, j: (i, j, 0))
    full = lambda a: pl.BlockSpec(a.shape, lambda i, j: (0,) * a.ndim)
    consts = [band, icnt, wpool, pscale, wpf, wpp, wpm, wout, wfi, wfo]
    in_specs = [
        tok(D_MODEL),
        pl.BlockSpec((None, 6, D_MODEL), (lambda i, j: (i, 0, 0)) if per_batch_mod else (lambda i, j: (0, 0, 0))),
        full(norm_g), tok(FOURIER_WIDTH), tok(POOL_WIDTH), tok(MLSTM_WIDTH), ogate, tok(3 * D_MODEL)]
    in_specs += [full(a) for a in consts]
    return pl.pallas_call(
        functools.partial(_merge_ffn_kernel, tm=tm),
        name="merge_ffn",
        grid=(b, l // tm),
        in_specs=in_specs,
        out_specs=tok(D_MODEL),
        out_shape=jax.ShapeDtypeStruct((b, l, D_MODEL), F32),
        compiler_params=pltpu.CompilerParams(
            dimension_semantics=("arbitrary", "arbitrary"), vmem_limit_bytes=VMEM_LIMIT),
    )(x, mod, norm_g, yf, up, hm, qkvo, mg, *consts)


def _layer_weights(i, norm_g, w_in, b_gate, w_proj_f, w_pool, pool_scale, w_proj_p, w_proj_m, w_out, w_ffn_in, w_ffn_out):
    w = w_in[i]
    wg = jnp.pad(w[:, OFF_G:OFF_M], ((0, 0), (0, LANES - N_GATE_COLS)))
    return dict(
        norm_g=norm_g[i],
        wa=w[:, :OFF_G].astype(BF16), wg=wg.astype(BF16), wm=w[:, OFF_M:].astype(BF16),
        b_gate=b_gate[i],
        wpool=jax.scipy.linalg.block_diag(*w_pool[i]).astype(BF16),
        pscale=pool_scale[i].reshape(1, POOL_WIDTH),
        wpf=w_proj_f[i].astype(BF16), wpp=w_proj_p[i].astype(BF16), wpm=w_proj_m[i].astype(BF16),
        wout=w_out[i].astype(BF16), wfi=w_ffn_in[i].astype(BF16), wfo=w_ffn_out[i].astype(BF16))


def _block(x, mod, lw, width, tm, init, emit_state, hp, unroll, pos=None):
    outs = _inproj(x, mod, lw["norm_g"], lw["wa"], lw["wg"], lw["b_gate"], lw["wm"], tm, pos)
    uf, up, qkvo, grows, mg = outs[:5]
    if pos is not None:
        x = outs[5]
    yf = _fourier(uf)
    mouts = _mlstm(qkvo, grows, init, emit_state, hp, unroll)
    band, icnt = _pool_tables(width)
    x = _merge_ffn(x, mod, lw["norm_g"], yf, up, mouts[0], qkvo, mg, jnp.asarray(band).astype(BF16), jnp.asarray(icnt),
                   lw["wpool"], lw["pscale"], lw["wpf"], lw["wpp"], lw["wpm"], lw["wout"], lw["wfi"], lw["wfo"], tm)
    return x, mouts[1:]


def kernel(x_prompt, x_sample, state_C, state_n, state_m, c, c_ctx, w_ada, b_ada, norm_g, w_in, b_gate, w_proj_f, w_pool, pool_scale, w_proj_p, w_proj_m, w_out, w_ffn_in, w_ffn_out):
    bp, lp, _ = x_prompt.shape
    bs, ls, _ = x_sample.shape
    cond = jnp.concatenate([c_ctx[None, :], c, jnp.zeros((COND_ROWS - 1 - bs, D_MODEL), F32)], axis=0)
    mod = _modulation(cond, w_ada, b_ada).reshape(DEPTH, COND_ROWS, 6, D_MODEL)
    weights = [_layer_weights(i, norm_g, w_in, b_gate, w_proj_f, w_pool, pool_scale, w_proj_p, w_proj_m,
                              w_out, w_ffn_in, w_ffn_out) for i in range(DEPTH)]

    xp = x_prompt
    cs, ns, ms = [], [], []
    for i in range(DEPTH):
        xp, (c_i, n_i, m_i) = _block(xp, mod[i, 0:1], weights[i], lp, min(lp, 512), None, True, HEADS, 1)
        cs.append(c_i)
        ns.append(n_i[:, :, :, 0, :])
        ms.append(m_i[:, :, :, 0, 0])

    xs = x_sample
    pos = jnp.asarray(_pos_table(ls // GRID_W))
    for i in range(DEPTH):
        init = (state_C[:, i], state_n[:, i], state_m[:, i])
        xs, _ = _block(xs, mod[i, 1:1 + bs], weights[i], GRID_W, 512, init, False, 1, 2, pos if i == 0 else None)

    return (xp, xs, jnp.stack(cs, axis=1), jnp.stack(ns, axis=1), jnp.stack(ms, axis=1))
```

```python
import functools

import numpy as np
import jax
import jax.numpy as jnp
from jax import lax
from jax.experimental import pallas as pl
from jax.experimental.pallas import tpu as pltpu

F32 = jnp.float32
BF16 = jnp.bfloat16

D_MODEL = 1024
DEPTH = 2
GRID_W = 64
FOURIER_GROUPS = 4
FOURIER_WIDTH = D_MODEL // 4
FOURIER_GROUP_DIM = FOURIER_WIDTH // FOURIER_GROUPS
POOL_WINDOWS = (2, 4, 8, 16)
POOL_WIDTH = D_MODEL // 4
POOL_GROUP_DIM = POOL_WIDTH // len(POOL_WINDOWS)
HEADS = 4
MLSTM_WIDTH = D_MODEL // 2
DH = MLSTM_WIDTH // HEADS
N_DIR = 2
N_GATE_COLS = N_DIR * 2 * HEADS
D_FF = -(-8 * D_MODEL // (3 * 256)) * 256
RMS_EPS = 1e-6
POS_BASE = 10000.0

OFF_F = 0
OFF_P = OFF_F + FOURIER_WIDTH
OFF_QKVO = OFF_P + POOL_WIDTH
OFF_G = OFF_QKVO + 4 * MLSTM_WIDTH
OFF_M = OFF_G + N_GATE_COLS
D_IN = OFF_M + 3 * D_MODEL

LANES = 128
MLSTM_CHUNK = 256
POOL_BLOCK = 256
COND_ROWS = 16
VMEM_LIMIT = 60 * 1024 * 1024


def _dot(a, b):
    return jnp.dot(a, b, preferred_element_type=F32)


def _rms(x, g):
    return x * lax.rsqrt(jnp.mean(x * x, axis=-1, keepdims=True) + RMS_EPS) * g


def _log_sigmoid(x):
    return jnp.minimum(x, 0.0) - jnp.log(1.0 + jnp.exp(-jnp.abs(x)))


def _sigmoid(x):
    return 0.5 * jnp.tanh(0.5 * x) + 0.5


@functools.lru_cache(maxsize=None)
def _dft_tables(n):
    idx = np.arange(n, dtype=np.int64)
    ang = 2.0 * np.pi * ((idx[:, None] * idx[None, :]) % n).astype(np.float64) / n
    return np.cos(ang).astype(np.float32), np.sin(ang).astype(np.float32)


@functools.lru_cache(maxsize=None)
def _group_dft_tables():
    c, s = _dft_tables(FOURIER_GROUP_DIM)
    bc = np.zeros((FOURIER_WIDTH, FOURIER_WIDTH), np.float32)
    bs = np.zeros((FOURIER_WIDTH, FOURIER_WIDTH), np.float32)
    for g in range(FOURIER_GROUPS):
        sl = slice(g * FOURIER_GROUP_DIM, (g + 1) * FOURIER_GROUP_DIM)
        bc[sl, sl] = c
        bs[sl, sl] = s
    return bc, bs


@functools.lru_cache(maxsize=None)
def _pool_tables(width):
    band = np.zeros((len(POOL_WINDOWS), POOL_BLOCK, POOL_BLOCK), np.float32)
    inv_cnt = np.zeros((POOL_BLOCK, POOL_WIDTH), np.float32)
    for g, w in enumerate(POOL_WINDOWS):
        left = w // 2
        right = w - 1 - left
        for t in range(POOL_BLOCK):
            row, pos = divmod(t, width)
            lo = min(max(pos - left, 0), width - 1)
            hi = min(max(pos + right, 0), width - 1)
            band[g, t, row * width + lo:row * width + hi + 1] = 1.0
            inv_cnt[t, g * POOL_GROUP_DIM:(g + 1) * POOL_GROUP_DIM] = 1.0 / (hi - lo + 1)
    return band, inv_cnt


@functools.lru_cache(maxsize=None)
def _pos_table(rows):
    quarter = D_MODEL // 4
    omega = 1.0 / (POS_BASE ** (np.arange(quarter, dtype=np.float64) / quarter))
    t = np.arange(rows * GRID_W)
    r = (t // GRID_W).astype(np.float64)
    col = (t % GRID_W).astype(np.float64)
    ar = r[:, None] * omega[None, :]
    ac = col[:, None] * omega[None, :]
    return np.concatenate([np.sin(ar), np.cos(ar), np.sin(ac), np.cos(ac)], axis=-1).astype(np.float32)


def _mod_kernel(c_ref, w_ref, b_ref, o_ref):
    c = c_ref[...]
    a = (c * jax.nn.sigmoid(c)).astype(BF16)
    o_ref[...] = _dot(a, w_ref[...].astype(BF16)) + b_ref[...]


def _modulation(cond, w_ada, b_ada):
    tn = 1536
    return pl.pallas_call(
        _mod_kernel,
        name="adaln_mod",
        grid=(DEPTH, 6 * D_MODEL // tn),
        in_specs=[
            pl.BlockSpec((COND_ROWS, D_MODEL), lambda l, j: (0, 0)),
            pl.BlockSpec((None, D_MODEL, tn), lambda l, j: (l, 0, j)),
            pl.BlockSpec((None, 1, tn), lambda l, j: (l, 0, j)),
        ],
        out_specs=pl.BlockSpec((None, COND_ROWS, tn), lambda l, j: (l, 0, j)),
        out_shape=jax.ShapeDtypeStruct((DEPTH, COND_ROWS, 6 * D_MODEL), F32),
        compiler_params=pltpu.CompilerParams(
            dimension_semantics=("arbitrary", "arbitrary"), vmem_limit_bytes=VMEM_LIMIT),
    )(cond, w_ada, b_ada.reshape(DEPTH, 1, 6 * D_MODEL))


def _gate_rows(g, bias_col, rows_ref, chunk):
    x = g.T[:N_GATE_COLS, :] + bias_col
    row = lax.broadcasted_iota(jnp.int32, x.shape, 0)
    fwd = row < 2 * HEADS
    lf = _log_sigmoid(x)
    b_f = jnp.where(fwd, _scan_lanes(lf, False, jnp.add, 0.0), _scan_lanes(lf, True, jnp.add, 0.0))
    b_i = pltpu.roll(b_f, N_GATE_COLS - HEADS, 0)
    a = x - b_i
    p = jnp.where(fwd, _scan_lanes(a, False, jnp.maximum, -jnp.inf), _scan_lanes(a, True, jnp.maximum, -jnp.inf))
    sl = slice(chunk * N_GATE_COLS, (chunk + 1) * N_GATE_COLS)
    rows_ref[0, sl, :] = a
    rows_ref[1, sl, :] = b_i
    rows_ref[2, sl, :] = b_i + p
    rows_ref[3, sl, :] = p


def _inproj_kernel(*refs, add_pos, chunk):
    it = iter(refs)
    x_ref = next(it)
    pos_ref = next(it) if add_pos else None
    mod_ref, ng_ref, wa_ref, wg_ref, bg_ref, wm_ref = (next(it) for _ in range(6))
    uf_ref, up_ref, qkvo_ref, rows_ref, mg_ref = (next(it) for _ in range(5))
    x0_ref = next(it) if add_pos else None

    x = x_ref[...]
    if add_pos:
        x = x + pos_ref[...]
        x0_ref[...] = x
    h = (_rms(x, ng_ref[0:1, :]) * (1.0 + mod_ref[1:2, :]) + mod_ref[0:1, :]).astype(BF16)
    ck = 512
    za = _dot(h, wa_ref[:, 0:ck])
    uf_ref[...] = za[:, :FOURIER_WIDTH].astype(BF16)
    up_ref[...] = za[:, FOURIER_WIDTH:].astype(BF16)
    for j in range(4 * MLSTM_WIDTH // ck):
        z = _dot(h, wa_ref[:, (j + 1) * ck:(j + 2) * ck])
        if j * ck >= 3 * MLSTM_WIDTH:
            z = _sigmoid(z)
        qkvo_ref[:, j * ck:(j + 1) * ck] = z.astype(BF16)
    g = _dot(h, wg_ref[...])
    for c in range(g.shape[0] // chunk):
        _gate_rows(g[c * chunk:(c + 1) * chunk, :], bg_ref[...], rows_ref, c)
    for j in range(3 * D_MODEL // ck):
        mg_ref[:, j * ck:(j + 1) * ck] = _sigmoid(_dot(h, wm_ref[:, j * ck:(j + 1) * ck])).astype(BF16)


def _inproj(x, mod, norm_g, wa, wg, b_gate, wm, tm, pos=None):
    b, l, _ = x.shape
    add_pos = pos is not None
    t = min(MLSTM_CHUNK, l)
    assert tm % t == 0 and l % tm == 0
    bias_col = b_gate.reshape(N_GATE_COLS, 1)
    per_batch_mod = mod.shape[0] > 1
    tok = lambda w: pl.BlockSpec((None, tm, w), lambda i, j: (i, j, 0))
    full = lambda a: pl.BlockSpec(a.shape, lambda i, j: (0,) * a.ndim)
    in_specs = [tok(D_MODEL)]
    args = [x]
    if add_pos:
        in_specs.append(pl.BlockSpec((tm, D_MODEL), lambda i, j: (j, 0)))
        args.append(pos)
    in_specs += [
        pl.BlockSpec((None, 6, D_MODEL), (lambda i, j: (i, 0, 0)) if per_batch_mod else (lambda i, j: (0, 0, 0))),
        full(norm_g), full(wa), full(wg), full(bias_col), full(wm)]
    args += [mod, norm_g, wa, wg, bias_col, wm]
    rows_per_tile = tm // t * N_GATE_COLS
    out_specs = [tok(FOURIER_WIDTH), tok(POOL_WIDTH), tok(4 * MLSTM_WIDTH),
                 pl.BlockSpec((None, 4, rows_per_tile, t), lambda i, j: (i, 0, j, 0)), tok(3 * D_MODEL)]
    out_shape = [
        jax.ShapeDtypeStruct((b, l, FOURIER_WIDTH), BF16),
        jax.ShapeDtypeStruct((b, l, POOL_WIDTH), BF16),
        jax.ShapeDtypeStruct((b, l, 4 * MLSTM_WIDTH), BF16),
        jax.ShapeDtypeStruct((b, 4, l // t * N_GATE_COLS, t), F32),
        jax.ShapeDtypeStruct((b, l, 3 * D_MODEL), BF16)]
    if add_pos:
        out_specs.append(tok(D_MODEL))
        out_shape.append(jax.ShapeDtypeStruct((b, l, D_MODEL), F32))
    return pl.pallas_call(
        functools.partial(_inproj_kernel, add_pos=add_pos, chunk=t),
        name="inproj",
        grid=(b, l // tm),
        in_specs=in_specs, out_specs=out_specs, out_shape=out_shape,
        compiler_params=pltpu.CompilerParams(
            dimension_semantics=("arbitrary", "arbitrary"), vmem_limit_bytes=VMEM_LIMIT),
    )(*args)


def _fourier_kernel(u_ref, cl_ref, sl_ref, bc_ref, bs_ref, o_ref, *, scale):
    u = u_ref[...]
    vc = _dot(u, bc_ref[...]).astype(BF16)
    vs = _dot(u, bs_ref[...]).astype(BF16)
    y = _dot(cl_ref[...], vc) - _dot(sl_ref[...], vs)
    o_ref[...] = (y * scale).astype(BF16)


def _fourier(uf):
    b, l, w = uf.shape
    cl, sl = (jnp.asarray(t).astype(BF16) for t in _dft_tables(l))
    bc, bs = (jnp.asarray(t).astype(BF16) for t in _group_dft_tables())
    full = lambda a: pl.BlockSpec(a.shape, lambda i: (0,) * a.ndim)
    seq = pl.BlockSpec((None, l, w), lambda i: (i, 0, 0))
    return pl.pallas_call(
        functools.partial(_fourier_kernel, scale=float((l * FOURIER_GROUP_DIM) ** -0.5)),
        name="fourier",
        grid=(b,),
        in_specs=[seq, full(cl), full(sl), full(bc), full(bs)],
        out_specs=seq,
        out_shape=jax.ShapeDtypeStruct((b, l, w), BF16),
        compiler_params=pltpu.CompilerParams(
            dimension_semantics=("arbitrary",), vmem_limit_bytes=VMEM_LIMIT),
    )(uf, cl, sl, bc, bs)


def _scan_lanes(x, reverse, op, fill):
    n = x.shape[-1]
    lane = lax.broadcasted_iota(jnp.int32, x.shape, x.ndim - 1)
    s = 1
    while s < n:
        if reverse:
            x = op(x, jnp.where(lane < n - s, pltpu.roll(x, n - s, x.ndim - 1), fill))
        else:
            x = op(x, jnp.where(lane >= s, pltpu.roll(x, s, x.ndim - 1), fill))
        s *= 2
    return x


SPLIT_ROWS = 16


def _mlstm_kernel(*refs, seq, chunk, hp, has_init, emit_state, unroll):
    it = iter(refs)
    q_ref, k_ref, v_ref, rows_ref = (next(it) for _ in range(4))
    c0_ref, n0_ref, m0_ref = (next(it) for _ in range(3)) if has_init else (None, None, None)
    hm_ref = next(it)
    cs_ref, ns_ref, ms_ref = (next(it) for _ in range(3)) if emit_state else (None, None, None)
    hf_ref, hb_ref = (next(it) for _ in range(2))

    t = chunk
    nc = seq // t
    rep = t // DH
    scale = DH ** -0.5

    row_i = lax.broadcasted_iota(jnp.int32, (t, t), 0)
    col_i = lax.broadcasted_iota(jnp.int32, (t, t), 1)
    eye_dh = jnp.where(lax.broadcasted_iota(jnp.int32, (DH, DH), 0) == lax.broadcasted_iota(jnp.int32, (DH, DH), 1),
                       1.0, 0.0).astype(BF16)
    sub = lax.broadcasted_iota(jnp.int32, (SPLIT_ROWS, t), 0)
    part = sub % 3
    p_sub = lax.broadcasted_iota(jnp.int32, (SPLIT_ROWS, 2 * DH), 0)
    p_lane = lax.broadcasted_iota(jnp.int32, (SPLIT_ROWS, 2 * DH), 1)
    gather_mat = jnp.where(((p_sub < 3) & (p_lane < DH)) | ((p_sub >= 3) & (p_sub < 6) & (p_lane >= DH)),
                           1.0, 0.0).astype(BF16)
    ones_cols = jnp.ones((t, DH), BF16)

    def step(c, hh, direction, Cn, m, h_ref):
        rows = pl.ds(pl.multiple_of(c * t, t), t)
        cols = slice(hh * DH, (hh + 1) * DH)
        r = pl.ds(c * N_GATE_COLS + direction * 2 * HEADS + pl.program_id(1) * hp + hh, 1)
        a_r = rows_ref[0, r, :]
        b_r = rows_ref[1, r, :]
        g_r = rows_ref[2, r, :]
        p_r = rows_ref[3, r, :]
        if direction == 0:
            end = t - 1
            mask = col_i <= row_i
        else:
            end = 0
            mask = col_i >= row_i
        b_end = b_r[:, end:end + 1]
        p_end = p_r[:, end:end + 1]

        x0 = jnp.where(sub < 3, b_r, jnp.where(sub < 6, g_r, 0.0))
        x1 = x0 - x0.astype(BF16).astype(F32)
        x2 = x1 - x1.astype(BF16).astype(F32)
        xs = jnp.where(part == 0, x0, jnp.where(part == 1, x1, x2)).astype(BF16)
        bg = lax.dot_general(xs, gather_mat, (((0,), (0,)), ((), ())), preferred_element_type=F32)
        b_c = bg[:, :DH]
        m_t = jnp.maximum(b_c + m, bg[:, DH:])
        mu = m_t - b_c
        sdec = jnp.exp(m - mu)

        qc = q_ref[rows, cols]
        kc = k_ref[rows, cols]
        v1 = jnp.concatenate([v_ref[rows, cols], ones_cols], axis=1)
        mu_t = jnp.concatenate([mu] * rep, axis=1)
        d = jnp.where(mask, jnp.exp((a_r + np.float32(np.log(scale))) - mu_t), 0.0)
        s = lax.dot_general(qc, kc, (((1,), (1,)), ((), ())), preferred_element_type=F32) * d
        qs = _dot(qc, Cn.astype(BF16))
        sv = _dot(s.astype(BF16), v1)
        num = sdec * qs[:, :DH] + sv[:, :DH]
        den = sdec * qs[:, DH:] + sv[:, DH:]
        h_ref[rows, cols] = num * (1.0 / jnp.maximum(jnp.abs(den), jnp.exp(-m_t)))

        mx = jnp.maximum(m, p_end)
        w_r = jnp.exp(a_r - mx) * scale
        kt = lax.dot_general(eye_dh, kc, (((1,), (1,)), ((), ())), preferred_element_type=F32)
        Cn_new = jnp.exp(m - mx) * Cn + _dot((kt * w_r).astype(BF16), v1)
        return Cn_new, b_end + mx

    def body(ci, carry):
        out = []
        for hh in range(hp):
            Cf, mf, Cb, mb = carry[4 * hh:4 * hh + 4]
            Cf, mf = step(ci, hh, 0, Cf, mf, hf_ref)
            Cb, mb = step(nc - 1 - ci, hh, 1, Cb, mb, hb_ref)
            out += [Cf, mf, Cb, mb]
        return tuple(out)

    init = []
    for hh in range(hp):
        for d in range(N_DIR):
            if has_init:
                n_col = jnp.sum(eye_dh.astype(F32) * n0_ref[d, hh], axis=1, keepdims=True)
                init += [jnp.concatenate([c0_ref[d, hh], jnp.broadcast_to(n_col, (DH, DH))], axis=1),
                         m0_ref[d, hh][:, 0:1]]
            else:
                init += [jnp.zeros((DH, 2 * DH), F32), jnp.zeros((1, 1), F32)]
    final = lax.fori_loop(0, nc, body, tuple(init), unroll=unroll)

    hm_ref[...] = (hf_ref[...] + hb_ref[...]).astype(BF16)
    if emit_state:
        eye_f = eye_dh.astype(F32)
        for hh in range(hp):
            for d in range(N_DIR):
                Cn, m = final[2 * (hh * N_DIR + d):2 * (hh * N_DIR + d) + 2]
                cs_ref[d, hh] = Cn[:, :DH]
                ns_ref[d, hh] = jnp.sum(eye_f * Cn[:, DH:], axis=0, keepdims=True)
                ms_ref[d, hh] = jnp.broadcast_to(m, (1, LANES))


def _mlstm(qkvo, grows, init, emit_state, hp, unroll):
    b, l, _ = qkvo.shape
    t = grows.shape[-1]
    nc = l // t
    assert l % t == 0 and t % DH == 0 and HEADS % hp == 0 and nc % unroll == 0
    has_init = init is not None
    groups = HEADS // hp
    head_cols = lambda k: pl.BlockSpec((None, l, hp * DH), lambda i, h: (i, 0, k * groups + h))
    state_c = pl.BlockSpec((None, N_DIR, hp, DH, DH), lambda i, h: (i, 0, h, 0, 0))
    state_v = pl.BlockSpec((None, N_DIR, hp, 1, LANES), lambda i, h: (i, 0, h, 0, 0))
    in_specs = [
        head_cols(0), head_cols(1), head_cols(2),
        pl.BlockSpec((None,) + grows.shape[1:], lambda i, h: (i, 0, 0, 0))]
    args = [qkvo, qkvo, qkvo, grows]
    if has_init:
        c0, n0, m0, layer = init
        in_specs += [
            pl.BlockSpec((None, None, N_DIR, hp, DH, DH), lambda i, h: (i, layer, 0, h, 0, 0)),
            pl.BlockSpec((None, None, N_DIR, hp, 1, DH), lambda i, h: (i, layer, 0, h, 0, 0)),
            pl.BlockSpec((None, None, N_DIR, hp, 1, LANES), lambda i, h: (i, layer, 0, h, 0, 0))]
        args += [c0, n0, m0]
    out_specs = [pl.BlockSpec((None, l, hp * DH), lambda i, h: (i, 0, h))]
    out_shape = [jax.ShapeDtypeStruct((b, l, MLSTM_WIDTH), BF16)]
    if emit_state:
        out_specs += [state_c, state_v, state_v]
        out_shape += [jax.ShapeDtypeStruct((b, N_DIR, HEADS, DH, DH), F32),
                      jax.ShapeDtypeStruct((b, N_DIR, HEADS, 1, DH), F32),
                      jax.ShapeDtypeStruct((b, N_DIR, HEADS, 1, LANES), F32)]
    return pl.pallas_call(
        functools.partial(_mlstm_kernel, seq=l, chunk=t, hp=hp, has_init=has_init, emit_state=emit_state,
                          unroll=unroll),
        name="mlstm",
        grid=(b, groups),
        in_specs=in_specs, out_specs=out_specs, out_shape=out_shape,
        scratch_shapes=[
            pltpu.VMEM((l, hp * DH), F32),
            pltpu.VMEM((l, hp * DH), F32)],
        compiler_params=pltpu.CompilerParams(
            dimension_semantics=("arbitrary", "arbitrary"), vmem_limit_bytes=VMEM_LIMIT),
    )(*args)


def _merge_ffn_kernel(x_ref, mod_ref, ng_ref, yf_ref, up_ref, hm_ref, og_ref, mg_ref, band_ref, icnt_ref,
                      wpool_ref, pscale_ref, wpf_ref, wpp_ref, wpm_ref, wout_ref, wfi_ref, wfo_ref,
                      o_ref, *, tm):
    x = x_ref[...]

    lane_group = lax.broadcasted_iota(jnp.int32, (POOL_BLOCK, POOL_WIDTH), 1) // POOL_GROUP_DIM
    pooled = []
    for r in range(tm // POOL_BLOCK):
        u = up_ref[r * POOL_BLOCK:(r + 1) * POOL_BLOCK, :]
        acc = jnp.zeros((POOL_BLOCK, POOL_WIDTH), F32)
        for g in range(len(POOL_WINDOWS)):
            acc = jnp.where(lane_group == g, _dot(band_ref[g], u), acc)
        pooled.append((acc * icnt_ref[...] - u.astype(F32)).astype(BF16))
    p = pooled[0] if len(pooled) == 1 else jnp.concatenate(pooled, axis=0)
    pp = (_dot(p, wpool_ref[...]) * pscale_ref[...]).astype(BF16)

    y_f = _dot(yf_ref[...], wpf_ref[...])
    y_p = _dot(pp, wpp_ref[...])
    hg = (og_ref[...].astype(F32) * hm_ref[...].astype(F32)).astype(BF16)
    y_m = _dot(hg, wpm_ref[...])
    g_f = mg_ref[:, 0:D_MODEL].astype(F32)
    g_p = mg_ref[:, D_MODEL:2 * D_MODEL].astype(F32)
    g_m = mg_ref[:, 2 * D_MODEL:3 * D_MODEL].astype(F32)
    y = (g_f * y_f + g_p * y_p + g_m * y_m).astype(BF16)
    x1 = x + mod_ref[2:3, :] * _rms(_dot(y, wout_ref[...]), ng_ref[1:2, :])

    h2 = (_rms(x1, ng_ref[2:3, :]) * (1.0 + mod_ref[4:5, :]) + mod_ref[3:4, :]).astype(BF16)
    ck = D_FF // 2
    acc = jnp.zeros((tm, D_MODEL), F32)
    for j in range(D_FF // ck):
        a = _dot(h2, wfi_ref[:, j * ck:(j + 1) * ck])
        bb = _dot(h2, wfi_ref[:, D_FF + j * ck:D_FF + (j + 1) * ck])
        act = (a * _sigmoid(a) * bb).astype(BF16)
        acc = acc + _dot(act, wfo_ref[j * ck:(j + 1) * ck, :])
    o_ref[...] = x1 + mod_ref[5:6, :] * _rms(acc, ng_ref[3:4, :])


def _merge_ffn(x, mod, norm_g, yf, up, hm, qkvo, mg, band, icnt, wpool, pscale, wpf, wpp, wpm, wout, wfi, wfo, tm):
    b, l, _ = x.shape
    ogate = pl.BlockSpec((None, tm, MLSTM_WIDTH), lambda i, j: (i, j, 3))
    per_batch_mod = mod.shape[0] > 1
    tok = lambda w: pl.BlockSpec((None, tm, w), lambda i, j: (i, j, 0))
    full = lambda a: pl.BlockSpec(a.shape, lambda i, j: (0,) * a.ndim)
    consts = [band, icnt, wpool, pscale, wpf, wpp, wpm, wout, wfi, wfo]
    in_specs = [
        tok(D_MODEL),
        pl.BlockSpec((None, 6, D_MODEL), (lambda i, j: (i, 0, 0)) if per_batch_mod else (lambda i, j: (0, 0, 0))),
        full(norm_g), tok(FOURIER_WIDTH), tok(POOL_WIDTH), tok(MLSTM_WIDTH), ogate, tok(3 * D_MODEL)]
    in_specs += [full(a) for a in consts]
    return pl.pallas_call(
        functools.partial(_merge_ffn_kernel, tm=tm),
        name="merge_ffn",
        grid=(b, l // tm),
        in_specs=in_specs,
        out_specs=tok(D_MODEL),
        out_shape=jax.ShapeDtypeStruct((b, l, D_MODEL), F32),
        compiler_params=pltpu.CompilerParams(
            dimension_semantics=("arbitrary", "arbitrary"), vmem_limit_bytes=VMEM_LIMIT),
    )(x, mod, norm_g, yf, up, hm, qkvo, mg, *consts)


def _layer_weights(i, norm_g, w_in, b_gate, w_proj_f, w_pool, pool_scale, w_proj_p, w_proj_m, w_out, w_ffn_in, w_ffn_out):
    w = w_in[i]
    wg = jnp.pad(w[:, OFF_G:OFF_M], ((0, 0), (0, LANES - N_GATE_COLS)))
    return dict(
        norm_g=norm_g[i],
        wa=w[:, :OFF_G].astype(BF16), wg=wg.astype(BF16), wm=w[:, OFF_M:].astype(BF16),
        b_gate=b_gate[i],
        wpool=jax.scipy.linalg.block_diag(*w_pool[i]).astype(BF16),
        pscale=pool_scale[i].reshape(1, POOL_WIDTH),
        wpf=w_proj_f[i].astype(BF16), wpp=w_proj_p[i].astype(BF16), wpm=w_proj_m[i].astype(BF16),
        wout=w_out[i].astype(BF16), wfi=w_ffn_in[i].astype(BF16), wfo=w_ffn_out[i].astype(BF16))


def _block(x, mod, lw, width, tm, init, emit_state, hp, unroll, pos=None):
    outs = _inproj(x, mod, lw["norm_g"], lw["wa"], lw["wg"], lw["b_gate"], lw["wm"], tm, pos)
    uf, up, qkvo, grows, mg = outs[:5]
    if pos is not None:
        x = outs[5]
    yf = _fourier(uf)
    mouts = _mlstm(qkvo, grows, init, emit_state, hp, unroll)
    band, icnt = _pool_tables(width)
    x = _merge_ffn(x, mod, lw["norm_g"], yf, up, mouts[0], qkvo, mg, jnp.asarray(band).astype(BF16), jnp.asarray(icnt),
                   lw["wpool"], lw["pscale"], lw["wpf"], lw["wpp"], lw["wpm"], lw["wout"], lw["wfi"], lw["wfo"], tm)
    return x, mouts[1:]


def kernel(x_prompt, x_sample, state_C, state_n, state_m, c, c_ctx, w_ada, b_ada, norm_g, w_in, b_gate, w_proj_f, w_pool, pool_scale, w_proj_p, w_proj_m, w_out, w_ffn_in, w_ffn_out):
    bp, lp, _ = x_prompt.shape
    bs, ls, _ = x_sample.shape
    cond = jnp.concatenate([c_ctx[None, :], c, jnp.zeros((COND_ROWS - 1 - bs, D_MODEL), F32)], axis=0)
    mod = _modulation(cond, w_ada, b_ada).reshape(DEPTH, COND_ROWS, 6, D_MODEL)
    weights = [_layer_weights(i, norm_g, w_in, b_gate, w_proj_f, w_pool, pool_scale, w_proj_p, w_proj_m,
                              w_out, w_ffn_in, w_ffn_out) for i in range(DEPTH)]

    xp = x_prompt
    cs, ns, ms = [], [], []
    for i in range(DEPTH):
        xp, (c_i, n_i, m_i) = _block(xp, mod[i, 0:1], weights[i], lp, min(lp, 512), None, True, HEADS, 1)
        cs.append(c_i)
        ns.append(n_i[:, :, :, 0, :])
        ms.append(m_i[:, :, :, 0, 0])

    xs = x_sample
    pos = jnp.asarray(_pos_table(ls // GRID_W))
    n0 = state_n.reshape(bs, DEPTH, N_DIR, HEADS, 1, DH)
    m0 = jnp.broadcast_to(state_m[..., None, None], (bs, DEPTH, N_DIR, HEADS, 1, LANES))
    for i in range(DEPTH):
        init = (state_C, n0, m0, i)
        xs, _ = _block(xs, mod[i, 1:1 + bs], weights[i], GRID_W, 512, init, False, 1, 2, pos if i == 0 else None)

    return (xp, xs, jnp.stack(cs, axis=1), jnp.stack(ns, axis=1), jnp.stack(ms, axis=1))
```

```python
import functools

import numpy as np
import jax
import jax.numpy as jnp
from jax import lax
from jax.experimental import pallas as pl
from jax.experimental.pallas import tpu as pltpu

F32 = jnp.float32
BF16 = jnp.bfloat16

D_MODEL = 1024
DEPTH = 2
GRID_W = 64
FOURIER_GROUPS = 4
FOURIER_WIDTH = D_MODEL // 4
FOURIER_GROUP_DIM = FOURIER_WIDTH // FOURIER_GROUPS
POOL_WINDOWS = (2, 4, 8, 16)
POOL_WIDTH = D_MODEL // 4
POOL_GROUP_DIM = POOL_WIDTH // len(POOL_WINDOWS)
HEADS = 4
MLSTM_WIDTH = D_MODEL // 2
DH = MLSTM_WIDTH // HEADS
N_DIR = 2
N_GATE_COLS = N_DIR * 2 * HEADS
D_FF = -(-8 * D_MODEL // (3 * 256)) * 256
RMS_EPS = 1e-6
POS_BASE = 10000.0

OFF_F = 0
OFF_P = OFF_F + FOURIER_WIDTH
OFF_QKVO = OFF_P + POOL_WIDTH
OFF_G = OFF_QKVO + 4 * MLSTM_WIDTH
OFF_M = OFF_G + N_GATE_COLS
D_IN = OFF_M + 3 * D_MODEL

LANES = 128
MLSTM_CHUNK = 256
POOL_BLOCK = 256
COND_ROWS = 16
VMEM_LIMIT = 60 * 1024 * 1024


def _dot(a, b):
    return jnp.dot(a, b, preferred_element_type=F32)


def _rms(x, g):
    return x * lax.rsqrt(jnp.mean(x * x, axis=-1, keepdims=True) + RMS_EPS) * g


def _log_sigmoid(x):
    return jnp.minimum(x, 0.0) - jnp.log(1.0 + jnp.exp(-jnp.abs(x)))


def _sigmoid(x):
    return 0.5 * jnp.tanh(0.5 * x) + 0.5


@functools.lru_cache(maxsize=None)
def _dft_tables(n):
    idx = np.arange(n, dtype=np.int64)
    ang = 2.0 * np.pi * ((idx[:, None] * idx[None, :]) % n).astype(np.float64) / n
    return np.cos(ang).astype(np.float32), np.sin(ang).astype(np.float32)


@functools.lru_cache(maxsize=None)
def _dft_half_tables(n):
    f = np.arange(n // 2, dtype=np.int64)[:, None]
    j = np.arange(n // 2, dtype=np.int64)[None, :]
    out = []
    for l in (2 * j, 2 * j + 1):
        ang = 2.0 * np.pi * ((f * l) % n).astype(np.float64) / n
        out += [np.cos(ang).astype(np.float32), np.sin(ang).astype(np.float32)]
    return tuple(out)


@functools.lru_cache(maxsize=None)
def _group_dft_tables():
    c, s = _dft_tables(FOURIER_GROUP_DIM)
    bc = np.zeros((FOURIER_WIDTH, FOURIER_WIDTH), np.float32)
    bs = np.zeros((FOURIER_WIDTH, FOURIER_WIDTH), np.float32)
    for g in range(FOURIER_GROUPS):
        sl = slice(g * FOURIER_GROUP_DIM, (g + 1) * FOURIER_GROUP_DIM)
        bc[sl, sl] = c
        bs[sl, sl] = s
    return bc, bs


@functools.lru_cache(maxsize=None)
def _pool_tables(width):
    band = np.zeros((len(POOL_WINDOWS), POOL_BLOCK, POOL_BLOCK), np.float32)
    inv_cnt = np.zeros((POOL_BLOCK, POOL_WIDTH), np.float32)
    for g, w in enumerate(POOL_WINDOWS):
        left = w // 2
        right = w - 1 - left
        for t in range(POOL_BLOCK):
            row, pos = divmod(t, width)
            lo = min(max(pos - left, 0), width - 1)
            hi = min(max(pos + right, 0), width - 1)
            band[g, t, row * width + lo:row * width + hi + 1] = 1.0
            inv_cnt[t, g * POOL_GROUP_DIM:(g + 1) * POOL_GROUP_DIM] = 1.0 / (hi - lo + 1)
    return band, inv_cnt


@functools.lru_cache(maxsize=None)
def _pos_table(rows):
    quarter = D_MODEL // 4
    omega = 1.0 / (POS_BASE ** (np.arange(quarter, dtype=np.float64) / quarter))
    t = np.arange(rows * GRID_W)
    r = (t // GRID_W).astype(np.float64)
    col = (t % GRID_W).astype(np.float64)
    ar = r[:, None] * omega[None, :]
    ac = col[:, None] * omega[None, :]
    return np.concatenate([np.sin(ar), np.cos(ar), np.sin(ac), np.cos(ac)], axis=-1).astype(np.float32)


def _mod_kernel(c_ref, w_ref, b_ref, o_ref):
    c = c_ref[...]
    a = (c * jax.nn.sigmoid(c)).astype(BF16)
    o_ref[...] = _dot(a, w_ref[...].astype(BF16)) + b_ref[...]


def _modulation(cond, w_ada, b_ada):
    tn = 3072
    return pl.pallas_call(
        _mod_kernel,
        name="adaln_mod",
        grid=(DEPTH, 6 * D_MODEL // tn),
        in_specs=[
            pl.BlockSpec((COND_ROWS, D_MODEL), lambda l, j: (0, 0)),
            pl.BlockSpec((None, D_MODEL, tn), lambda l, j: (l, 0, j)),
            pl.BlockSpec((None, 1, tn), lambda l, j: (l, 0, j)),
        ],
        out_specs=pl.BlockSpec((None, COND_ROWS, tn), lambda l, j: (l, 0, j)),
        out_shape=jax.ShapeDtypeStruct((DEPTH, COND_ROWS, 6 * D_MODEL), F32),
        compiler_params=pltpu.CompilerParams(
            dimension_semantics=("arbitrary", "arbitrary"), vmem_limit_bytes=VMEM_LIMIT),
    )(cond, w_ada, b_ada.reshape(DEPTH, 1, 6 * D_MODEL))


def _gate_rows(g, bias_col, rows_ref, chunk):
    x = g.T[:N_GATE_COLS, :] + bias_col
    row = lax.broadcasted_iota(jnp.int32, x.shape, 0)
    fwd = row < 2 * HEADS
    lf = _log_sigmoid(x)
    b_f = jnp.where(fwd, _scan_lanes(lf, False, jnp.add, 0.0), _scan_lanes(lf, True, jnp.add, 0.0))
    b_i = pltpu.roll(b_f, N_GATE_COLS - HEADS, 0)
    a = x - b_i
    p = jnp.where(fwd, _scan_lanes(a, False, jnp.maximum, -jnp.inf), _scan_lanes(a, True, jnp.maximum, -jnp.inf))
    sl = slice(chunk * N_GATE_COLS, (chunk + 1) * N_GATE_COLS)
    rows_ref[0, sl, :] = a
    rows_ref[1, sl, :] = b_i
    rows_ref[2, sl, :] = b_i + p
    rows_ref[3, sl, :] = p


def _inproj_kernel(*refs, add_pos, chunk):
    it = iter(refs)
    x_ref = next(it)
    pos_ref = next(it) if add_pos else None
    mod_ref, ng_ref, wa_ref, wg_ref, bg_ref, wm_ref = (next(it) for _ in range(6))
    uf_ref, up_ref, qkvo_ref, rows_ref, mg_ref = (next(it) for _ in range(5))
    x0_ref = next(it) if add_pos else None

    x = x_ref[...]
    if add_pos:
        x = x + pos_ref[...]
        x0_ref[...] = x
    h = (_rms(x, ng_ref[0:1, :]) * (1.0 + mod_ref[1:2, :]) + mod_ref[0:1, :]).astype(BF16)
    ck = 512
    za = _dot(h, wa_ref[:, 0:ck])
    uf_ref[...] = za[:, :FOURIER_WIDTH].astype(BF16)
    up_ref[...] = za[:, FOURIER_WIDTH:].astype(BF16)
    for j in range(4 * MLSTM_WIDTH // ck):
        z = _dot(h, wa_ref[:, (j + 1) * ck:(j + 2) * ck])
        if j * ck >= 3 * MLSTM_WIDTH:
            z = _sigmoid(z)
        qkvo_ref[:, j * ck:(j + 1) * ck] = z.astype(BF16)
    g = _dot(h, wg_ref[...])
    for c in range(g.shape[0] // chunk):
        _gate_rows(g[c * chunk:(c + 1) * chunk, :], bg_ref[...], rows_ref, c)
    for j in range(3 * D_MODEL // ck):
        mg_ref[:, j * ck:(j + 1) * ck] = _sigmoid(_dot(h, wm_ref[:, j * ck:(j + 1) * ck])).astype(BF16)


def _inproj(x, mod, norm_g, wa, wg, b_gate, wm, tm, pos=None):
    b, l, _ = x.shape
    add_pos = pos is not None
    t = min(MLSTM_CHUNK, l)
    assert tm % t == 0 and l % tm == 0
    bias_col = b_gate.reshape(N_GATE_COLS, 1)
    per_batch_mod = mod.shape[0] > 1
    tok = lambda w: pl.BlockSpec((None, tm, w), lambda i, j: (i, j, 0))
    full = lambda a: pl.BlockSpec(a.shape, lambda i, j: (0,) * a.ndim)
    in_specs = [tok(D_MODEL)]
    args = [x]
    if add_pos:
        in_specs.append(pl.BlockSpec((tm, D_MODEL), lambda i, j: (j, 0)))
        args.append(pos)
    in_specs += [
        pl.BlockSpec((None, 6, D_MODEL), (lambda i, j: (i, 0, 0)) if per_batch_mod else (lambda i, j: (0, 0, 0))),
        full(norm_g), full(wa), full(wg), full(bias_col), full(wm)]
    args += [mod, norm_g, wa, wg, bias_col, wm]
    rows_per_tile = tm // t * N_GATE_COLS
    out_specs = [tok(FOURIER_WIDTH), tok(POOL_WIDTH), tok(4 * MLSTM_WIDTH),
                 pl.BlockSpec((None, 4, rows_per_tile, t), lambda i, j: (i, 0, j, 0)), tok(3 * D_MODEL)]
    out_shape = [
        jax.ShapeDtypeStruct((b, l, FOURIER_WIDTH), BF16),
        jax.ShapeDtypeStruct((b, l, POOL_WIDTH), BF16),
        jax.ShapeDtypeStruct((b, l, 4 * MLSTM_WIDTH), BF16),
        jax.ShapeDtypeStruct((b, 4, l // t * N_GATE_COLS, t), F32),
        jax.ShapeDtypeStruct((b, l, 3 * D_MODEL), BF16)]
    if add_pos:
        out_specs.append(tok(D_MODEL))
        out_shape.append(jax.ShapeDtypeStruct((b, l, D_MODEL), F32))
    return pl.pallas_call(
        functools.partial(_inproj_kernel, add_pos=add_pos, chunk=t),
        name="inproj",
        grid=(b, l // tm),
        in_specs=in_specs, out_specs=out_specs, out_shape=out_shape,
        compiler_params=pltpu.CompilerParams(
            dimension_semantics=("arbitrary", "arbitrary"), vmem_limit_bytes=VMEM_LIMIT),
    )(*args)


def _fourier_kernel(u_ref, ce_ref, se_ref, co_ref, so_ref, bc_ref, bs_ref, o_ref, vc_ref, vs_ref, *, scale, half):
    u = u_ref[...]
    nblk = u.shape[1] // LANES
    for ref, tab in ((vc_ref, bc_ref), (vs_ref, bs_ref)):
        v = _dot(u, tab[...])
        for k in range(nblk):
            ref[k] = v[:, k * LANES:(k + 1) * LANES]

    def rows(ref, start):
        return jnp.concatenate([ref[k, pl.ds(start, half, stride=2), :] for k in range(nblk)], axis=1).astype(BF16)

    e = _dot(ce_ref[...], rows(vc_ref, 0)) - _dot(se_ref[...], rows(vs_ref, 0))
    o = _dot(co_ref[...], rows(vc_ref, 1)) - _dot(so_ref[...], rows(vs_ref, 1))
    o_ref[0:half, :] = ((e + o) * scale).astype(BF16)
    o_ref[half:, :] = ((e - o) * scale).astype(BF16)


def _fourier(uf):
    b, l, w = uf.shape
    half = l // 2
    bc, bs = (jnp.asarray(t).astype(BF16) for t in _group_dft_tables())
    tables = [jnp.asarray(t).astype(BF16) for t in _dft_half_tables(l)]
    full = lambda a: pl.BlockSpec(a.shape, lambda i: (0,) * a.ndim)
    seq = pl.BlockSpec((None, l, w), lambda i: (i, 0, 0))
    return pl.pallas_call(
        functools.partial(_fourier_kernel, scale=float((l * FOURIER_GROUP_DIM) ** -0.5), half=half),
        name="fourier",
        grid=(b,),
        in_specs=[seq] + [full(t) for t in tables] + [full(bc), full(bs)],
        out_specs=seq,
        out_shape=jax.ShapeDtypeStruct((b, l, w), BF16),
        scratch_shapes=[pltpu.VMEM((w // LANES, l, LANES), F32), pltpu.VMEM((w // LANES, l, LANES), F32)],
        compiler_params=pltpu.CompilerParams(
            dimension_semantics=("arbitrary",), vmem_limit_bytes=VMEM_LIMIT),
    )(uf, *tables, bc, bs)


def _scan_lanes(x, reverse, op, fill):
    n = x.shape[-1]
    lane = lax.broadcasted_iota(jnp.int32, x.shape, x.ndim - 1)
    s = 1
    while s < n:
        if reverse:
            x = op(x, jnp.where(lane < n - s, pltpu.roll(x, n - s, x.ndim - 1), fill))
        else:
            x = op(x, jnp.where(lane >= s, pltpu.roll(x, s, x.ndim - 1), fill))
        s *= 2
    return x


SPLIT_ROWS = 16


def _mlstm_kernel(*refs, seq, chunk, hp, has_init, emit_state, unroll):
    it = iter(refs)
    q_ref, k_ref, v_ref, rows_ref = (next(it) for _ in range(4))
    c0_ref, n0_ref, m0_ref = (next(it) for _ in range(3)) if has_init else (None, None, None)
    hm_ref = next(it)
    cs_ref, ns_ref, ms_ref = (next(it) for _ in range(3)) if emit_state else (None, None, None)
    hf_ref, hb_ref = (next(it) for _ in range(2))

    t = chunk
    nc = seq // t
    rep = t // DH
    scale = DH ** -0.5

    row_i = lax.broadcasted_iota(jnp.int32, (t, t), 0)
    col_i = lax.broadcasted_iota(jnp.int32, (t, t), 1)
    eye_dh = jnp.where(lax.broadcasted_iota(jnp.int32, (DH, DH), 0) == lax.broadcasted_iota(jnp.int32, (DH, DH), 1),
                       1.0, 0.0).astype(BF16)
    sub = lax.broadcasted_iota(jnp.int32, (SPLIT_ROWS, t), 0)
    part = sub % 3
    p_sub = lax.broadcasted_iota(jnp.int32, (SPLIT_ROWS, 2 * DH), 0)
    p_lane = lax.broadcasted_iota(jnp.int32, (SPLIT_ROWS, 2 * DH), 1)
    gather_mat = jnp.where(((p_sub < 3) & (p_lane < DH)) | ((p_sub >= 3) & (p_sub < 6) & (p_lane >= DH)),
                           1.0, 0.0).astype(BF16)
    ones_cols = jnp.ones((t, DH), BF16)

    def step(c, hh, direction, Cn, m, h_ref):
        rows = pl.ds(pl.multiple_of(c * t, t), t)
        cols = slice(hh * DH, (hh + 1) * DH)
        r = pl.ds(c * N_GATE_COLS + direction * 2 * HEADS + pl.program_id(1) * hp + hh, 1)
        a_r = rows_ref[0, r, :]
        b_r = rows_ref[1, r, :]
        g_r = rows_ref[2, r, :]
        p_r = rows_ref[3, r, :]
        if direction == 0:
            end = t - 1
            mask = col_i <= row_i
        else:
            end = 0
            mask = col_i >= row_i
        b_end = b_r[:, end:end + 1]
        p_end = p_r[:, end:end + 1]

        x0 = jnp.where(sub < 3, b_r, jnp.where(sub < 6, g_r, 0.0))
        x1 = x0 - x0.astype(BF16).astype(F32)
        x2 = x1 - x1.astype(BF16).astype(F32)
        xs = jnp.where(part == 0, x0, jnp.where(part == 1, x1, x2)).astype(BF16)
        bg = lax.dot_general(xs, gather_mat, (((0,), (0,)), ((), ())), preferred_element_type=F32)
        b_c = bg[:, :DH]
        m_t = jnp.maximum(b_c + m, bg[:, DH:])
        mu = m_t - b_c
        sdec = jnp.exp(m - mu)

        qc = q_ref[rows, cols]
        kc = k_ref[rows, cols]
        v1 = jnp.concatenate([v_ref[rows, cols], ones_cols], axis=1)
        mu_t = jnp.concatenate([mu] * rep, axis=1)
        d = jnp.where(mask, jnp.exp((a_r + np.float32(np.log(scale))) - mu_t), 0.0)
        s = lax.dot_general(qc, kc, (((1,), (1,)), ((), ())), preferred_element_type=F32) * d
        qs = _dot(qc, Cn.astype(BF16))
        sv = _dot(s.astype(BF16), v1)
        num = sdec * qs[:, :DH] + sv[:, :DH]
        den = sdec * qs[:, DH:] + sv[:, DH:]
        h_ref[rows, cols] = num * (1.0 / jnp.maximum(jnp.abs(den), jnp.exp(-m_t)))

        mx = jnp.maximum(m, p_end)
        w_r = jnp.exp(a_r - mx) * scale
        kt = lax.dot_general(eye_dh, kc, (((1,), (1,)), ((), ())), preferred_element_type=F32)
        Cn_new = jnp.exp(m - mx) * Cn + _dot((kt * w_r).astype(BF16), v1)
        return Cn_new, b_end + mx

    def body(ci, carry):
        out = []
        for hh in range(hp):
            Cf, mf, Cb, mb = carry[4 * hh:4 * hh + 4]
            Cf, mf = step(ci, hh, 0, Cf, mf, hf_ref)
            Cb, mb = step(nc - 1 - ci, hh, 1, Cb, mb, hb_ref)
            out += [Cf, mf, Cb, mb]
        return tuple(out)

    init = []
    for hh in range(hp):
        for d in range(N_DIR):
            if has_init:
                n_col = jnp.sum(eye_dh.astype(F32) * n0_ref[d, hh], axis=1, keepdims=True)
                init += [jnp.concatenate([c0_ref[d, hh], jnp.broadcast_to(n_col, (DH, DH))], axis=1),
                         m0_ref[d, hh][:, 0:1]]
            else:
                init += [jnp.zeros((DH, 2 * DH), F32), jnp.zeros((1, 1), F32)]
    final = lax.fori_loop(0, nc, body, tuple(init), unroll=unroll)

    hm_ref[...] = (hf_ref[...] + hb_ref[...]).astype(BF16)
    if emit_state:
        eye_f = eye_dh.astype(F32)
        for hh in range(hp):
            for d in range(N_DIR):
                Cn, m = final[2 * (hh * N_DIR + d):2 * (hh * N_DIR + d) + 2]
                cs_ref[d, hh] = Cn[:, :DH]
                ns_ref[d, hh] = jnp.sum(eye_f * Cn[:, DH:], axis=0, keepdims=True)
                ms_ref[d, hh] = jnp.broadcast_to(m, (1, LANES))


def _mlstm(qkvo, grows, init, emit_state, hp, unroll):
    b, l, _ = qkvo.shape
    t = grows.shape[-1]
    nc = l // t
    assert l % t == 0 and t % DH == 0 and HEADS % hp == 0 and nc % unroll == 0
    has_init = init is not None
    groups = HEADS // hp
    head_cols = lambda k: pl.BlockSpec((None, l, hp * DH), lambda i, h: (i, 0, k * groups + h))
    state_c = pl.BlockSpec((None, N_DIR, hp, DH, DH), lambda i, h: (i, 0, h, 0, 0))
    state_v = pl.BlockSpec((None, N_DIR, hp, 1, LANES), lambda i, h: (i, 0, h, 0, 0))
    in_specs = [
        head_cols(0), head_cols(1), head_cols(2),
        pl.BlockSpec((None,) + grows.shape[1:], lambda i, h: (i, 0, 0, 0))]
    args = [qkvo, qkvo, qkvo, grows]
    if has_init:
        c0, n0, m0, layer = init
        in_specs += [
            pl.BlockSpec((None, None, N_DIR, hp, DH, DH), lambda i, h: (i, layer, 0, h, 0, 0)),
            pl.BlockSpec((None, None, N_DIR, hp, 1, DH), lambda i, h: (i, layer, 0, h, 0, 0)),
            pl.BlockSpec((None, None, N_DIR, hp, 1, LANES), lambda i, h: (i, layer, 0, h, 0, 0))]
        args += [c0, n0, m0]
    out_specs = [pl.BlockSpec((None, l, hp * DH), lambda i, h: (i, 0, h))]
    out_shape = [jax.ShapeDtypeStruct((b, l, MLSTM_WIDTH), BF16)]
    if emit_state:
        out_specs += [state_c, state_v, state_v]
        out_shape += [jax.ShapeDtypeStruct((b, N_DIR, HEADS, DH, DH), F32),
                      jax.ShapeDtypeStruct((b, N_DIR, HEADS, 1, DH), F32),
                      jax.ShapeDtypeStruct((b, N_DIR, HEADS, 1, LANES), F32)]
    return pl.pallas_call(
        functools.partial(_mlstm_kernel, seq=l, chunk=t, hp=hp, has_init=has_init, emit_state=emit_state,
                          unroll=unroll),
        name="mlstm",
        grid=(b, groups),
        in_specs=in_specs, out_specs=out_specs, out_shape=out_shape,
        scratch_shapes=[
            pltpu.VMEM((l, hp * DH), F32),
            pltpu.VMEM((l, hp * DH), F32)],
        compiler_params=pltpu.CompilerParams(
            dimension_semantics=("arbitrary", "arbitrary"), vmem_limit_bytes=VMEM_LIMIT),
    )(*args)


def _merge_ffn_kernel(x_ref, mod_ref, ng_ref, yf_ref, up_ref, hm_ref, og_ref, mg_ref, band_ref, icnt_ref,
                      wpool_ref, pscale_ref, wpf_ref, wpp_ref, wpm_ref, wout_ref, wfi_ref, wfo_ref,
                      o_ref, *, tm):
    x = x_ref[...]

    lane_group = lax.broadcasted_iota(jnp.int32, (POOL_BLOCK, POOL_WIDTH), 1) // POOL_GROUP_DIM
    pooled = []
    for r in range(tm // POOL_BLOCK):
        u = up_ref[r * POOL_BLOCK:(r + 1) * POOL_BLOCK, :]
        acc = jnp.zeros((POOL_BLOCK, POOL_WIDTH), F32)
        for g in range(len(POOL_WINDOWS)):
            acc = jnp.where(lane_group == g, _dot(band_ref[g], u), acc)
        pooled.append((acc * icnt_ref[...] - u.astype(F32)).astype(BF16))
    p = pooled[0] if len(pooled) == 1 else jnp.concatenate(pooled, axis=0)
    pp = (_dot(p, wpool_ref[...]) * pscale_ref[...]).astype(BF16)

    y_f = _dot(yf_ref[...], wpf_ref[...])
    y_p = _dot(pp, wpp_ref[...])
    hg = (og_ref[...].astype(F32) * hm_ref[...].astype(F32)).astype(BF16)
    y_m = _dot(hg, wpm_ref[...])
    g_f = mg_ref[:, 0:D_MODEL].astype(F32)
    g_p = mg_ref[:, D_MODEL:2 * D_MODEL].astype(F32)
    g_m = mg_ref[:, 2 * D_MODEL:3 * D_MODEL].astype(F32)
    y = (g_f * y_f + g_p * y_p + g_m * y_m).astype(BF16)
    x1 = x + mod_ref[2:3, :] * _rms(_dot(y, wout_ref[...]), ng_ref[1:2, :])

    h2 = (_rms(x1, ng_ref[2:3, :]) * (1.0 + mod_ref[4:5, :]) + mod_ref[3:4, :]).astype(BF16)
    ck = D_FF // 2
    acc = jnp.zeros((tm, D_MODEL), F32)
    for j in range(D_FF // ck):
        a = _dot(h2, wfi_ref[:, j * ck:(j + 1) * ck])
        bb = _dot(h2, wfi_ref[:, D_FF + j * ck:D_FF + (j + 1) * ck])
        act = (a * _sigmoid(a) * bb).astype(BF16)
        acc = acc + _dot(act, wfo_ref[j * ck:(j + 1) * ck, :])
    o_ref[...] = x1 + mod_ref[5:6, :] * _rms(acc, ng_ref[3:4, :])


def _merge_ffn(x, mod, norm_g, yf, up, hm, qkvo, mg, band, icnt, wpool, pscale, wpf, wpp, wpm, wout, wfi, wfo, tm):
    b, l, _ = x.shape
    ogate = pl.BlockSpec((None, tm, MLSTM_WIDTH), lambda i, j: (i, j, 3))
    per_batch_mod = mod.shape[0] > 1
    tok = lambda w: pl.BlockSpec((None, tm, w), lambda i, j: (i, j, 0))
    full = lambda a: pl.BlockSpec(a.shape, lambda i, j: (0,) * a.ndim)
    consts = [band, icnt, wpool, pscale, wpf, wpp, wpm, wout, wfi, wfo]
    in_specs = [
        tok(D_MODEL),
        pl.BlockSpec((None, 6, D_MODEL), (lambda i, j: (i, 0, 0)) if per_batch_mod else (lambda i, j: (0, 0, 0))),
        full(norm_g), tok(FOURIER_WIDTH), tok(POOL_WIDTH), tok(MLSTM_WIDTH), ogate, tok(3 * D_MODEL)]
    in_specs += [full(a) for a in consts]
    return pl.pallas_call(
        functools.partial(_merge_ffn_kernel, tm=tm),
        name="merge_ffn",
        grid=(b, l // tm),
        in_specs=in_specs,
        out_specs=tok(D_MODEL),
        out_shape=jax.ShapeDtypeStruct((b, l, D_MODEL), F32),
        compiler_params=pltpu.CompilerParams(
            dimension_semantics=("arbitrary", "arbitrary"), vmem_limit_bytes=VMEM_LIMIT),
    )(x, mod, norm_g, yf, up, hm, qkvo, mg, *consts)


def _layer_weights(i, norm_g, w_in, b_gate, w_proj_f, w_pool, pool_scale, w_proj_p, w_proj_m, w_out, w_ffn_in, w_ffn_out):
    w = w_in[i]
    wg = jnp.pad(w[:, OFF_G:OFF_M], ((0, 0), (0, LANES - N_GATE_COLS)))
    return dict(
        norm_g=norm_g[i],
        wa=w[:, :OFF_G].astype(BF16), wg=wg.astype(BF16), wm=w[:, OFF_M:].astype(BF16),
        b_gate=b_gate[i],
        wpool=jax.scipy.linalg.block_diag(*w_pool[i]).astype(BF16),
        pscale=pool_scale[i].reshape(1, POOL_WIDTH),
        wpf=w_proj_f[i].astype(BF16), wpp=w_proj_p[i].astype(BF16), wpm=w_proj_m[i].astype(BF16),
        wout=w_out[i].astype(BF16), wfi=w_ffn_in[i].astype(BF16), wfo=w_ffn_out[i].astype(BF16))


def _block(x, mod, lw, width, tm, init, emit_state, hp, unroll, pos=None):
    outs = _inproj(x, mod, lw["norm_g"], lw["wa"], lw["wg"], lw["b_gate"], lw["wm"], tm, pos)
    uf, up, qkvo, grows, mg = outs[:5]
    if pos is not None:
        x = outs[5]
    yf = _fourier(uf)
    mouts = _mlstm(qkvo, grows, init, emit_state, hp, unroll)
    band, icnt = _pool_tables(width)
    x = _merge_ffn(x, mod, lw["norm_g"], yf, up, mouts[0], qkvo, mg, jnp.asarray(band).astype(BF16), jnp.asarray(icnt),
                   lw["wpool"], lw["pscale"], lw["wpf"], lw["wpp"], lw["wpm"], lw["wout"], lw["wfi"], lw["wfo"], tm)
    return x, mouts[1:]


def kernel(x_prompt, x_sample, state_C, state_n, state_m, c, c_ctx, w_ada, b_ada, norm_g, w_in, b_gate, w_proj_f, w_pool, pool_scale, w_proj_p, w_proj_m, w_out, w_ffn_in, w_ffn_out):
    bp, lp, _ = x_prompt.shape
    bs, ls, _ = x_sample.shape
    cond = jnp.concatenate([c_ctx[None, :], c, jnp.zeros((COND_ROWS - 1 - bs, D_MODEL), F32)], axis=0)
    mod = _modulation(cond, w_ada, b_ada).reshape(DEPTH, COND_ROWS, 6, D_MODEL)
    weights = [_layer_weights(i, norm_g, w_in, b_gate, w_proj_f, w_pool, pool_scale, w_proj_p, w_proj_m,
                              w_out, w_ffn_in, w_ffn_out) for i in range(DEPTH)]

    xp = x_prompt
    cs, ns, ms = [], [], []
    for i in range(DEPTH):
        xp, (c_i, n_i, m_i) = _block(xp, mod[i, 0:1], weights[i], lp, min(lp, 512), None, True, HEADS, 1)
        cs.append(c_i)
        ns.append(n_i[:, :, :, 0, :])
        ms.append(m_i[:, :, :, 0, 0])

    xs = x_sample
    pos = jnp.asarray(_pos_table(ls // GRID_W))
    n0 = state_n.reshape(bs, DEPTH, N_DIR, HEADS, 1, DH)
    m0 = jnp.broadcast_to(state_m[..., None, None], (bs, DEPTH, N_DIR, HEADS, 1, LANES))
    for i in range(DEPTH):
        init = (state_C, n0, m0, i)
        xs, _ = _block(xs, mod[i, 1:1 + bs], weights[i], GRID_W, 512, init, False, 1, 2, pos if i == 0 else None)

    return (xp, xs, jnp.stack(cs, axis=1), jnp.stack(ns, axis=1), jnp.stack(ms, axis=1))
```

```python
import functools

import numpy as np
import jax
import jax.numpy as jnp
from jax import lax
from jax.experimental import pallas as pl
from jax.experimental.pallas import tpu as pltpu

F32 = jnp.float32
BF16 = jnp.bfloat16

D_MODEL = 1024
DEPTH = 2
GRID_W = 64
FOURIER_GROUPS = 4
FOURIER_WIDTH = D_MODEL // 4
FOURIER_GROUP_DIM = FOURIER_WIDTH // FOURIER_GROUPS
POOL_WINDOWS = (2, 4, 8, 16)
POOL_WIDTH = D_MODEL // 4
POOL_GROUP_DIM = POOL_WIDTH // len(POOL_WINDOWS)
HEADS = 4
MLSTM_WIDTH = D_MODEL // 2
DH = MLSTM_WIDTH // HEADS
N_DIR = 2
N_GATE_COLS = N_DIR * 2 * HEADS
D_FF = -(-8 * D_MODEL // (3 * 256)) * 256
RMS_EPS = 1e-6
POS_BASE = 10000.0

OFF_F = 0
OFF_P = OFF_F + FOURIER_WIDTH
OFF_QKVO = OFF_P + POOL_WIDTH
OFF_G = OFF_QKVO + 4 * MLSTM_WIDTH
OFF_M = OFF_G + N_GATE_COLS
D_IN = OFF_M + 3 * D_MODEL

LANES = 128
MLSTM_CHUNK = 256
POOL_BLOCK = 256
COND_ROWS = 16
VMEM_LIMIT = 60 * 1024 * 1024
MXU_COLS = 256
FFN_CHUNKS = ((0, 5 * MXU_COLS), (5 * MXU_COLS, D_FF))


def _dot(a, b):
    return jnp.dot(a, b, preferred_element_type=F32)


def _rms(x, g):
    return x * lax.rsqrt(jnp.mean(x * x, axis=-1, keepdims=True) + RMS_EPS) * g


def _log_sigmoid(x):
    return jnp.minimum(x, 0.0) - jnp.log(1.0 + jnp.exp(-jnp.abs(x)))


def _sigmoid(x):
    return 0.5 * jnp.tanh(0.5 * x) + 0.5


@functools.lru_cache(maxsize=None)
def _dft_tables(n):
    idx = np.arange(n, dtype=np.int64)
    ang = 2.0 * np.pi * ((idx[:, None] * idx[None, :]) % n).astype(np.float64) / n
    return np.cos(ang).astype(np.float32), np.sin(ang).astype(np.float32)


@functools.lru_cache(maxsize=None)
def _dft_half_tables(n):
    f = np.arange(n // 2, dtype=np.int64)[:, None]
    j = np.arange(n // 2, dtype=np.int64)[None, :]
    out = []
    for l in (2 * j, 2 * j + 1):
        ang = 2.0 * np.pi * ((f * l) % n).astype(np.float64) / n
        out += [np.cos(ang).astype(np.float32), np.sin(ang).astype(np.float32)]
    return tuple(out)


@functools.lru_cache(maxsize=None)
def _group_dft_tables():
    c, s = _dft_tables(FOURIER_GROUP_DIM)
    bc = np.zeros((FOURIER_WIDTH, FOURIER_WIDTH), np.float32)
    bs = np.zeros((FOURIER_WIDTH, FOURIER_WIDTH), np.float32)
    for g in range(FOURIER_GROUPS):
        sl = slice(g * FOURIER_GROUP_DIM, (g + 1) * FOURIER_GROUP_DIM)
        bc[sl, sl] = c
        bs[sl, sl] = s
    return bc, bs


@functools.lru_cache(maxsize=None)
def _pool_tables(width):
    band = np.zeros((len(POOL_WINDOWS), POOL_BLOCK, POOL_BLOCK), np.float32)
    inv_cnt = np.zeros((POOL_BLOCK, POOL_WIDTH), np.float32)
    for g, w in enumerate(POOL_WINDOWS):
        left = w // 2
        right = w - 1 - left
        for t in range(POOL_BLOCK):
            row, pos = divmod(t, width)
            lo = min(max(pos - left, 0), width - 1)
            hi = min(max(pos + right, 0), width - 1)
            band[g, t, row * width + lo:row * width + hi + 1] = 1.0
            inv_cnt[t, g * POOL_GROUP_DIM:(g + 1) * POOL_GROUP_DIM] = 1.0 / (hi - lo + 1)
    return band, inv_cnt


@functools.lru_cache(maxsize=None)
def _pos_table(rows):
    quarter = D_MODEL // 4
    omega = 1.0 / (POS_BASE ** (np.arange(quarter, dtype=np.float64) / quarter))
    t = np.arange(rows * GRID_W)
    r = (t // GRID_W).astype(np.float64)
    col = (t % GRID_W).astype(np.float64)
    ar = r[:, None] * omega[None, :]
    ac = col[:, None] * omega[None, :]
    return np.concatenate([np.sin(ar), np.cos(ar), np.sin(ac), np.cos(ac)], axis=-1).astype(np.float32)


def _mod_kernel(c_ref, w_ref, b_ref, o_ref):
    @pl.when(pl.program_id(1) == 0)
    def _():
        o_ref[...] = jnp.broadcast_to(b_ref[...], o_ref.shape)

    c = c_ref[...]
    a = (c * jax.nn.sigmoid(c)).astype(BF16)
    o_ref[...] += _dot(a, w_ref[...].astype(BF16))


def _modulation(cond, w_ada, b_ada):
    tk = 256
    n = 6 * D_MODEL
    return pl.pallas_call(
        _mod_kernel,
        name="adaln_mod",
        grid=(DEPTH, D_MODEL // tk),
        in_specs=[
            pl.BlockSpec((COND_ROWS, tk), lambda l, k: (0, k)),
            pl.BlockSpec((None, tk, n), lambda l, k: (l, k, 0)),
            pl.BlockSpec((None, 1, n), lambda l, k: (l, 0, 0)),
        ],
        out_specs=pl.BlockSpec((None, COND_ROWS, n), lambda l, k: (l, 0, 0)),
        out_shape=jax.ShapeDtypeStruct((DEPTH, COND_ROWS, 6 * D_MODEL), F32),
        compiler_params=pltpu.CompilerParams(
            dimension_semantics=("arbitrary", "arbitrary"), vmem_limit_bytes=VMEM_LIMIT),
    )(cond, w_ada, b_ada.reshape(DEPTH, 1, 6 * D_MODEL))


def _gate_rows(g, bias_col, rows_ref, chunk):
    x = g.T[:N_GATE_COLS, :] + bias_col
    row = lax.broadcasted_iota(jnp.int32, x.shape, 0)
    fwd = row < 2 * HEADS
    lf = _log_sigmoid(x)
    b_f = jnp.where(fwd, _scan_lanes(lf, False, jnp.add, 0.0), _scan_lanes(lf, True, jnp.add, 0.0))
    b_i = pltpu.roll(b_f, N_GATE_COLS - HEADS, 0)
    a = x - b_i
    p = jnp.where(fwd, _scan_lanes(a, False, jnp.maximum, -jnp.inf), _scan_lanes(a, True, jnp.maximum, -jnp.inf))
    sl = slice(chunk * N_GATE_COLS, (chunk + 1) * N_GATE_COLS)
    rows_ref[0, sl, :] = a
    rows_ref[1, sl, :] = b_i
    rows_ref[2, sl, :] = b_i + p
    rows_ref[3, sl, :] = p


def _inproj_kernel(*refs, add_pos, chunk):
    it = iter(refs)
    x_ref = next(it)
    pos_ref = next(it) if add_pos else None
    mod_ref, ng_ref, wa_ref, wg_ref, bg_ref, wm_ref = (next(it) for _ in range(6))
    uf_ref, up_ref, qkvo_ref, rows_ref, mg_ref = (next(it) for _ in range(5))
    x0_ref = next(it) if add_pos else None

    x = x_ref[...]
    if add_pos:
        x = x + pos_ref[...]
        x0_ref[...] = x
    h = (_rms(x, ng_ref[0:1, :] * (1.0 + mod_ref[1:2, :])) + mod_ref[0:1, :]).astype(BF16)
    ck = 512
    za = _dot(h, wa_ref[:, 0:ck])
    uf_ref[...] = za[:, :FOURIER_WIDTH].astype(BF16)
    up_ref[...] = za[:, FOURIER_WIDTH:].astype(BF16)
    for j in range(4 * MLSTM_WIDTH // ck):
        z = _dot(h, wa_ref[:, (j + 1) * ck:(j + 2) * ck])
        if j * ck >= 3 * MLSTM_WIDTH:
            z = _sigmoid(z)
        qkvo_ref[:, j * ck:(j + 1) * ck] = z.astype(BF16)
    g = _dot(h, wg_ref[...])
    for c in range(g.shape[0] // chunk):
        _gate_rows(g[c * chunk:(c + 1) * chunk, :], bg_ref[...], rows_ref, c)
    for j in range(3 * D_MODEL // ck):
        mg_ref[:, j * ck:(j + 1) * ck] = _sigmoid(_dot(h, wm_ref[:, j * ck:(j + 1) * ck])).astype(BF16)


def _inproj(x, mod, norm_g, wa, wg, b_gate, wm, tm, pos=None):
    b, l, _ = x.shape
    add_pos = pos is not None
    t = min(MLSTM_CHUNK, l)
    assert tm % t == 0 and l % tm == 0
    bias_col = b_gate.reshape(N_GATE_COLS, 1)
    per_batch_mod = mod.shape[0] > 1
    tok = lambda w: pl.BlockSpec((None, tm, w), lambda i, j: (i, j, 0))
    full = lambda a: pl.BlockSpec(a.shape, lambda i, j: (0,) * a.ndim)
    in_specs = [tok(D_MODEL)]
    args = [x]
    if add_pos:
        in_specs.append(pl.BlockSpec((tm, D_MODEL), lambda i, j: (j, 0)))
        args.append(pos)
    in_specs += [
        pl.BlockSpec((None, 6, D_MODEL), (lambda i, j: (i, 0, 0)) if per_batch_mod else (lambda i, j: (0, 0, 0))),
        full(norm_g), full(wa), full(wg), full(bias_col), full(wm)]
    args += [mod, norm_g, wa, wg, bias_col, wm]
    rows_per_tile = tm // t * N_GATE_COLS
    out_specs = [tok(FOURIER_WIDTH), tok(POOL_WIDTH), tok(4 * MLSTM_WIDTH),
                 pl.BlockSpec((None, 4, rows_per_tile, t), lambda i, j: (i, 0, j, 0)), tok(3 * D_MODEL)]
    out_shape = [
        jax.ShapeDtypeStruct((b, l, FOURIER_WIDTH), BF16),
        jax.ShapeDtypeStruct((b, l, POOL_WIDTH), BF16),
        jax.ShapeDtypeStruct((b, l, 4 * MLSTM_WIDTH), BF16),
        jax.ShapeDtypeStruct((b, 4, l // t * N_GATE_COLS, t), F32),
        jax.ShapeDtypeStruct((b, l, 3 * D_MODEL), BF16)]
    if add_pos:
        out_specs.append(tok(D_MODEL))
        out_shape.append(jax.ShapeDtypeStruct((b, l, D_MODEL), F32))
    return pl.pallas_call(
        functools.partial(_inproj_kernel, add_pos=add_pos, chunk=t),
        name="inproj",
        grid=(b, l // tm),
        in_specs=in_specs, out_specs=out_specs, out_shape=out_shape,
        compiler_params=pltpu.CompilerParams(
            dimension_semantics=("arbitrary", "arbitrary"), vmem_limit_bytes=VMEM_LIMIT),
    )(*args)


def _fourier_kernel(u_ref, ce_ref, se_ref, co_ref, so_ref, bc_ref, bs_ref, o_ref, vc_ref, vs_ref, *, scale, half):
    u = u_ref[...]
    nblk = u.shape[1] // LANES
    for ref, tab in ((vc_ref, bc_ref), (vs_ref, bs_ref)):
        v = _dot(u, tab[...])
        for k in range(nblk):
            ref[k] = v[:, k * LANES:(k + 1) * LANES]

    def rows(ref, start):
        return jnp.concatenate([ref[k, pl.ds(start, half, stride=2), :] for k in range(nblk)], axis=1).astype(BF16)

    e = _dot(ce_ref[...], rows(vc_ref, 0)) - _dot(se_ref[...], rows(vs_ref, 0))
    o = _dot(co_ref[...], rows(vc_ref, 1)) - _dot(so_ref[...], rows(vs_ref, 1))
    o_ref[0:half, :] = ((e + o) * scale).astype(BF16)
    o_ref[half:, :] = ((e - o) * scale).astype(BF16)


def _fourier(uf):
    b, l, w = uf.shape
    half = l // 2
    bc, bs = (jnp.asarray(t).astype(BF16) for t in _group_dft_tables())
    tables = [jnp.asarray(t).astype(BF16) for t in _dft_half_tables(l)]
    full = lambda a: pl.BlockSpec(a.shape, lambda i: (0,) * a.ndim)
    seq = pl.BlockSpec((None, l, w), lambda i: (i, 0, 0))
    return pl.pallas_call(
        functools.partial(_fourier_kernel, scale=float((l * FOURIER_GROUP_DIM) ** -0.5), half=half),
        name="fourier",
        grid=(b,),
        in_specs=[seq] + [full(t) for t in tables] + [full(bc), full(bs)],
        out_specs=seq,
        out_shape=jax.ShapeDtypeStruct((b, l, w), BF16),
        scratch_shapes=[pltpu.VMEM((w // LANES, l, LANES), F32), pltpu.VMEM((w // LANES, l, LANES), F32)],
        compiler_params=pltpu.CompilerParams(
            dimension_semantics=("arbitrary",), vmem_limit_bytes=VMEM_LIMIT),
    )(uf, *tables, bc, bs)


def _scan_lanes(x, reverse, op, fill):
    n = x.shape[-1]
    lane = lax.broadcasted_iota(jnp.int32, x.shape, x.ndim - 1)
    s = 1
    while s < n:
        if reverse:
            x = op(x, jnp.where(lane < n - s, pltpu.roll(x, n - s, x.ndim - 1), fill))
        else:
            x = op(x, jnp.where(lane >= s, pltpu.roll(x, s, x.ndim - 1), fill))
        s *= 2
    return x


SPLIT_ROWS = 16


def _mlstm_kernel(*refs, seq, chunk, hp, has_init, emit_state, unroll):
    it = iter(refs)
    q_ref, k_ref, v_ref, rows_ref = (next(it) for _ in range(4))
    c0_ref, n0_ref, m0_ref = (next(it) for _ in range(3)) if has_init else (None, None, None)
    hm_ref = next(it)
    cs_ref, ns_ref, ms_ref = (next(it) for _ in range(3)) if emit_state else (None, None, None)
    hf_ref, hb_ref = (next(it) for _ in range(2))

    t = chunk
    nc = seq // t
    rep = t // DH
    scale = DH ** -0.5

    row_i = lax.broadcasted_iota(jnp.int32, (t, t), 0)
    col_i = lax.broadcasted_iota(jnp.int32, (t, t), 1)
    eye_dh = jnp.where(lax.broadcasted_iota(jnp.int32, (DH, DH), 0) == lax.broadcasted_iota(jnp.int32, (DH, DH), 1),
                       1.0, 0.0).astype(BF16)
    sub = lax.broadcasted_iota(jnp.int32, (SPLIT_ROWS, t), 0)
    part = sub % 3
    p_sub = lax.broadcasted_iota(jnp.int32, (SPLIT_ROWS, 2 * DH), 0)
    p_lane = lax.broadcasted_iota(jnp.int32, (SPLIT_ROWS, 2 * DH), 1)
    gather_mat = jnp.where(((p_sub < 3) & (p_lane < DH)) | ((p_sub >= 3) & (p_sub < 6) & (p_lane >= DH)),
                           1.0, 0.0).astype(BF16)
    ones_cols = jnp.ones((t, DH), BF16)

    def step(c, hh, direction, Cn, m, h_ref):
        rows = pl.ds(pl.multiple_of(c * t, t), t)
        cols = slice(hh * DH, (hh + 1) * DH)
        r = pl.ds(c * N_GATE_COLS + direction * 2 * HEADS + pl.program_id(1) * hp + hh, 1)
        a_r = rows_ref[0, r, :]
        b_r = rows_ref[1, r, :]
        g_r = rows_ref[2, r, :]
        p_r = rows_ref[3, r, :]
        if direction == 0:
            end = t - 1
            mask = col_i <= row_i
        else:
            end = 0
            mask = col_i >= row_i
        b_end = b_r[:, end:end + 1]
        p_end = p_r[:, end:end + 1]

        x0 = jnp.where(sub < 3, b_r, jnp.where(sub < 6, g_r, 0.0))
        x1 = x0 - x0.astype(BF16).astype(F32)
        x2 = x1 - x1.astype(BF16).astype(F32)
        xs = jnp.where(part == 0, x0, jnp.where(part == 1, x1, x2)).astype(BF16)
        bg = lax.dot_general(xs, gather_mat, (((0,), (0,)), ((), ())), preferred_element_type=F32)
        b_c = bg[:, :DH]
        m_t = jnp.maximum(b_c + m, bg[:, DH:])
        mu = m_t - b_c
        sdec = jnp.exp(m - mu)

        qc = q_ref[rows, cols]
        kc = k_ref[rows, cols]
        v1 = jnp.concatenate([v_ref[rows, cols], ones_cols], axis=1)
        mu_t = jnp.concatenate([mu] * rep, axis=1)
        d = jnp.where(mask, jnp.exp((a_r + np.float32(np.log(scale))) - mu_t), 0.0)
        s = lax.dot_general(qc, kc, (((1,), (1,)), ((), ())), preferred_element_type=F32) * d
        qs = _dot(qc, Cn.astype(BF16))
        sv = _dot(s.astype(BF16), v1)
        num = sdec * qs[:, :DH] + sv[:, :DH]
        den = sdec * qs[:, DH:] + sv[:, DH:]
        h_ref[rows, cols] = num * (1.0 / jnp.maximum(jnp.abs(den), jnp.exp(-m_t)))

        mx = jnp.maximum(m, p_end)
        w_r = jnp.exp(a_r - mx) * scale
        kt = lax.dot_general(eye_dh, kc, (((1,), (1,)), ((), ())), preferred_element_type=F32)
        Cn_new = jnp.exp(m - mx) * Cn + _dot((kt * w_r).astype(BF16), v1)
        return Cn_new, b_end + mx

    def body(ci, carry):
        out = []
        for hh in range(hp):
            Cf, mf, Cb, mb = carry[4 * hh:4 * hh + 4]
            Cf, mf = step(ci, hh, 0, Cf, mf, hf_ref)
            Cb, mb = step(nc - 1 - ci, hh, 1, Cb, mb, hb_ref)
            out += [Cf, mf, Cb, mb]
        return tuple(out)

    init = []
    for hh in range(hp):
        for d in range(N_DIR):
            if has_init:
                n_col = jnp.sum(eye_dh.astype(F32) * n0_ref[d, hh], axis=1, keepdims=True)
                init += [jnp.concatenate([c0_ref[d, hh], jnp.broadcast_to(n_col, (DH, DH))], axis=1),
                         m0_ref[d, hh][:, 0:1]]
            else:
                init += [jnp.zeros((DH, 2 * DH), F32), jnp.zeros((1, 1), F32)]
    final = lax.fori_loop(0, nc, body, tuple(init), unroll=unroll)

    hm_ref[...] = (hf_ref[...] + hb_ref[...]).astype(BF16)
    if emit_state:
        eye_f = eye_dh.astype(F32)
        for hh in range(hp):
            for d in range(N_DIR):
                Cn, m = final[2 * (hh * N_DIR + d):2 * (hh * N_DIR + d) + 2]
                cs_ref[d, hh] = Cn[:, :DH]
                ns_ref[d, hh] = jnp.sum(eye_f * Cn[:, DH:], axis=0, keepdims=True)
                ms_ref[d, hh] = jnp.broadcast_to(m, (1, LANES))


def _mlstm(qkvo, grows, init, emit_state, hp, unroll):
    b, l, _ = qkvo.shape
    t = grows.shape[-1]
    nc = l // t
    assert l % t == 0 and t % DH == 0 and HEADS % hp == 0 and nc % unroll == 0
    has_init = init is not None
    groups = HEADS // hp
    head_cols = lambda k: pl.BlockSpec((None, l, hp * DH), lambda i, h: (i, 0, k * groups + h))
    state_c = pl.BlockSpec((None, N_DIR, hp, DH, DH), lambda i, h: (i, 0, h, 0, 0))
    state_v = pl.BlockSpec((None, N_DIR, hp, 1, LANES), lambda i, h: (i, 0, h, 0, 0))
    in_specs = [
        head_cols(0), head_cols(1), head_cols(2),
        pl.BlockSpec((None,) + grows.shape[1:], lambda i, h: (i, 0, 0, 0))]
    args = [qkvo, qkvo, qkvo, grows]
    if has_init:
        c0, n0, m0, layer = init
        in_specs += [
            pl.BlockSpec((None, None, N_DIR, hp, DH, DH), lambda i, h: (i, layer, 0, h, 0, 0)),
            pl.BlockSpec((None, None, N_DIR, hp, 1, DH), lambda i, h: (i, layer, 0, h, 0, 0)),
            pl.BlockSpec((None, None, N_DIR, hp, 1, LANES), lambda i, h: (i, layer, 0, h, 0, 0))]
        args += [c0, n0, m0]
    out_specs = [pl.BlockSpec((None, l, hp * DH), lambda i, h: (i, 0, h))]
    out_shape = [jax.ShapeDtypeStruct((b, l, MLSTM_WIDTH), BF16)]
    if emit_state:
        out_specs += [state_c, state_v, state_v]
        out_shape += [jax.ShapeDtypeStruct((b, N_DIR, HEADS, DH, DH), F32),
                      jax.ShapeDtypeStruct((b, N_DIR, HEADS, 1, DH), F32),
                      jax.ShapeDtypeStruct((b, N_DIR, HEADS, 1, LANES), F32)]
    return pl.pallas_call(
        functools.partial(_mlstm_kernel, seq=l, chunk=t, hp=hp, has_init=has_init, emit_state=emit_state,
                          unroll=unroll),
        name="mlstm",
        grid=(b, groups),
        in_specs=in_specs, out_specs=out_specs, out_shape=out_shape,
        scratch_shapes=[
            pltpu.VMEM((l, hp * DH), F32),
            pltpu.VMEM((l, hp * DH), F32)],
        compiler_params=pltpu.CompilerParams(
            dimension_semantics=("arbitrary", "arbitrary"), vmem_limit_bytes=VMEM_LIMIT),
    )(*args)


def _merge_ffn_kernel(x_ref, mod_ref, ng_ref, yf_ref, up_ref, hm_ref, og_ref, mg_ref, band_ref, icnt_ref,
                      wpool_ref, pscale_ref, wpf_ref, wpp_ref, wpm_ref, wout_ref, wfi_ref, wfo_ref,
                      o_ref, *, tm):
    x = x_ref[...]

    lane_group = lax.broadcasted_iota(jnp.int32, (POOL_BLOCK, POOL_WIDTH), 1) // POOL_GROUP_DIM
    pooled = []
    for r in range(tm // POOL_BLOCK):
        u = up_ref[r * POOL_BLOCK:(r + 1) * POOL_BLOCK, :]
        acc = jnp.zeros((POOL_BLOCK, POOL_WIDTH), F32)
        for g in range(len(POOL_WINDOWS)):
            acc = jnp.where(lane_group == g, _dot(band_ref[g], u), acc)
        pooled.append((acc * icnt_ref[...] - u.astype(F32)).astype(BF16))
    p = pooled[0] if len(pooled) == 1 else jnp.concatenate(pooled, axis=0)
    pp = (_dot(p, wpool_ref[...]) * pscale_ref[...]).astype(BF16)

    y_f = _dot(yf_ref[...], wpf_ref[...])
    y_p = _dot(pp, wpp_ref[...])
    hg = (og_ref[...].astype(F32) * hm_ref[...].astype(F32)).astype(BF16)
    y_m = _dot(hg, wpm_ref[...])
    g_f = mg_ref[:, 0:D_MODEL].astype(F32)
    g_p = mg_ref[:, D_MODEL:2 * D_MODEL].astype(F32)
    g_m = mg_ref[:, 2 * D_MODEL:3 * D_MODEL].astype(F32)
    y = (g_f * y_f + g_p * y_p + g_m * y_m).astype(BF16)
    x1 = x + _rms(_dot(y, wout_ref[...]), ng_ref[1:2, :] * mod_ref[2:3, :])

    h2 = (_rms(x1, ng_ref[2:3, :] * (1.0 + mod_ref[4:5, :])) + mod_ref[3:4, :]).astype(BF16)
    acc = jnp.zeros((tm, D_MODEL), F32)
    for c0, c1 in FFN_CHUNKS:
        a = _dot(h2, wfi_ref[:, c0:c1])
        bb = _dot(h2, wfi_ref[:, D_FF + c0:D_FF + c1])
        act = (a * _sigmoid(a) * bb).astype(BF16)
        acc = acc + _dot(act, wfo_ref[c0:c1, :])
    o_ref[...] = x1 + _rms(acc, ng_ref[3:4, :] * mod_ref[5:6, :])


def _merge_ffn(x, mod, norm_g, yf, up, hm, qkvo, mg, band, icnt, wpool, pscale, wpf, wpp, wpm, wout, wfi, wfo, tm):
    b, l, _ = x.shape
    ogate = pl.BlockSpec((None, tm, MLSTM_WIDTH), lambda i, j: (i, j, 3))
    per_batch_mod = mod.shape[0] > 1
    tok = lambda w: pl.BlockSpec((None, tm, w), lambda i, j: (i, j, 0))
    full = lambda a: pl.BlockSpec(a.shape, lambda i, j: (0,) * a.ndim)
    consts = [band, icnt, wpool, pscale, wpf, wpp, wpm, wout, wfi, wfo]
    in_specs = [
        tok(D_MODEL),
        pl.BlockSpec((None, 6, D_MODEL), (lambda i, j: (i, 0, 0)) if per_batch_mod else (lambda i, j: (0, 0, 0))),
        full(norm_g), tok(FOURIER_WIDTH), tok(POOL_WIDTH), tok(MLSTM_WIDTH), ogate, tok(3 * D_MODEL)]
    in_specs += [full(a) for a in consts]
    return pl.pallas_call(
        functools.partial(_merge_ffn_kernel, tm=tm),
        name="merge_ffn",
        grid=(b, l // tm),
        in_specs=in_specs,
        out_specs=tok(D_MODEL),
        out_shape=jax.ShapeDtypeStruct((b, l, D_MODEL), F32),
        compiler_params=pltpu.CompilerParams(
            dimension_semantics=("arbitrary", "arbitrary"), vmem_limit_bytes=VMEM_LIMIT),
    )(x, mod, norm_g, yf, up, hm, qkvo, mg, *consts)


def _layer_weights(i, norm_g, w_in, b_gate, w_proj_f, w_pool, pool_scale, w_proj_p, w_proj_m, w_out, w_ffn_in, w_ffn_out):
    w = w_in[i]
    wg = jnp.pad(w[:, OFF_G:OFF_M], ((0, 0), (0, LANES - N_GATE_COLS)))
    return dict(
        norm_g=norm_g[i],
        wa=w[:, :OFF_G].astype(BF16), wg=wg.astype(BF16), wm=w[:, OFF_M:].astype(BF16),
        b_gate=b_gate[i],
        wpool=jax.scipy.linalg.block_diag(*w_pool[i]).astype(BF16),
        pscale=pool_scale[i].reshape(1, POOL_WIDTH),
        wpf=w_proj_f[i].astype(BF16), wpp=w_proj_p[i].astype(BF16), wpm=w_proj_m[i].astype(BF16),
        wout=w_out[i].astype(BF16), wfi=w_ffn_in[i].astype(BF16), wfo=w_ffn_out[i].astype(BF16))


def _block(x, mod, lw, width, tm, init, emit_state, hp, unroll, pos=None):
    outs = _inproj(x, mod, lw["norm_g"], lw["wa"], lw["wg"], lw["b_gate"], lw["wm"], tm, pos)
    uf, up, qkvo, grows, mg = outs[:5]
    if pos is not None:
        x = outs[5]
    yf = _fourier(uf)
    mouts = _mlstm(qkvo, grows, init, emit_state, hp, unroll)
    band, icnt = _pool_tables(width)
    x = _merge_ffn(x, mod, lw["norm_g"], yf, up, mouts[0], qkvo, mg, jnp.asarray(band).astype(BF16), jnp.asarray(icnt),
                   lw["wpool"], lw["pscale"], lw["wpf"], lw["wpp"], lw["wpm"], lw["wout"], lw["wfi"], lw["wfo"], tm)
    return x, mouts[1:]


def kernel(x_prompt, x_sample, state_C, state_n, state_m, c, c_ctx, w_ada, b_ada, norm_g, w_in, b_gate, w_proj_f, w_pool, pool_scale, w_proj_p, w_proj_m, w_out, w_ffn_in, w_ffn_out):
    bp, lp, _ = x_prompt.shape
    bs, ls, _ = x_sample.shape
    cond = jnp.concatenate([c_ctx[None, :], c, jnp.zeros((COND_ROWS - 1 - bs, D_MODEL), F32)], axis=0)
    mod = _modulation(cond, w_ada, b_ada).reshape(DEPTH, COND_ROWS, 6, D_MODEL)
    weights = [_layer_weights(i, norm_g, w_in, b_gate, w_proj_f, w_pool, pool_scale, w_proj_p, w_proj_m,
                              w_out, w_ffn_in, w_ffn_out) for i in range(DEPTH)]

    xp = x_prompt
    cs, ns, ms = [], [], []
    for i in range(DEPTH):
        xp, (c_i, n_i, m_i) = _block(xp, mod[i, 0:1], weights[i], lp, min(lp, 512), None, True, HEADS, 1)
        cs.append(c_i)
        ns.append(n_i[:, :, :, 0, :])
        ms.append(m_i[:, :, :, 0, 0])

    xs = x_sample
    pos = jnp.asarray(_pos_table(ls // GRID_W))
    n0 = state_n.reshape(bs, DEPTH, N_DIR, HEADS, 1, DH)
    m0 = jnp.broadcast_to(state_m[..., None, None], (bs, DEPTH, N_DIR, HEADS, 1, LANES))
    for i in range(DEPTH):
        init = (state_C, n0, m0, i)
        xs, _ = _block(xs, mod[i, 1:1 + bs], weights[i], GRID_W, 512, init, False, 1, 2, pos if i == 0 else None)

    return (xp, xs, jnp.stack(cs, axis=1), jnp.stack(ns, axis=1), jnp.stack(ms, axis=1))
```

```python
import functools

import numpy as np
import jax
import jax.numpy as jnp
from jax import lax
from jax.experimental import pallas as pl
from jax.experimental.pallas import tpu as pltpu

F32 = jnp.float32
BF16 = jnp.bfloat16

D_MODEL = 1024
DEPTH = 2
GRID_W = 64
FOURIER_GROUPS = 4
FOURIER_WIDTH = D_MODEL // 4
FOURIER_GROUP_DIM = FOURIER_WIDTH // FOURIER_GROUPS
POOL_WINDOWS = (2, 4, 8, 16)
POOL_WIDTH = D_MODEL // 4
POOL_GROUP_DIM = POOL_WIDTH // len(POOL_WINDOWS)
HEADS = 4
MLSTM_WIDTH = D_MODEL // 2
DH = MLSTM_WIDTH // HEADS
N_DIR = 2
N_GATE_COLS = N_DIR * 2 * HEADS
D_FF = -(-8 * D_MODEL // (3 * 256)) * 256
RMS_EPS = 1e-6
POS_BASE = 10000.0

OFF_F = 0
OFF_P = OFF_F + FOURIER_WIDTH
OFF_QKVO = OFF_P + POOL_WIDTH
OFF_G = OFF_QKVO + 4 * MLSTM_WIDTH
OFF_M = OFF_G + N_GATE_COLS
D_IN = OFF_M + 3 * D_MODEL

LANES = 128
MLSTM_CHUNK = 256
POOL_BLOCK = 256
COND_ROWS = 16
VMEM_LIMIT = 60 * 1024 * 1024
MXU_COLS = 256
FFN_CHUNKS = ((0, 5 * MXU_COLS), (5 * MXU_COLS, D_FF))
CAT_M = OFF_G
CAT_G = CAT_M + 3 * D_MODEL
CAT_W = CAT_G + LANES


def _dot(a, b):
    return jnp.dot(a, b, preferred_element_type=F32)


def _rms(x, g):
    return x * lax.rsqrt(jnp.mean(x * x, axis=-1, keepdims=True) + RMS_EPS) * g


def _log_sigmoid(x):
    return jnp.minimum(x, 0.0) - jnp.log(1.0 + jnp.exp(-jnp.abs(x)))


def _sigmoid(x):
    return 0.5 * jnp.tanh(0.5 * x) + 0.5


@functools.lru_cache(maxsize=None)
def _dft_tables(n):
    idx = np.arange(n, dtype=np.int64)
    ang = 2.0 * np.pi * ((idx[:, None] * idx[None, :]) % n).astype(np.float64) / n
    return np.cos(ang).astype(np.float32), np.sin(ang).astype(np.float32)


@functools.lru_cache(maxsize=None)
def _dft_half_tables(n):
    f = np.arange(n // 2, dtype=np.int64)[:, None]
    j = np.arange(n // 2, dtype=np.int64)[None, :]
    out = []
    for l in (2 * j, 2 * j + 1):
        ang = 2.0 * np.pi * ((f * l) % n).astype(np.float64) / n
        out += [np.cos(ang).astype(np.float32), np.sin(ang).astype(np.float32)]
    return tuple(out)


@functools.lru_cache(maxsize=None)
def _group_dft_tables():
    c, s = _dft_tables(FOURIER_GROUP_DIM)
    bc = np.zeros((FOURIER_WIDTH, FOURIER_WIDTH), np.float32)
    bs = np.zeros((FOURIER_WIDTH, FOURIER_WIDTH), np.float32)
    for g in range(FOURIER_GROUPS):
        sl = slice(g * FOURIER_GROUP_DIM, (g + 1) * FOURIER_GROUP_DIM)
        bc[sl, sl] = c
        bs[sl, sl] = s
    return bc, bs


@functools.lru_cache(maxsize=None)
def _pool_tables(width):
    band = np.zeros((len(POOL_WINDOWS), POOL_BLOCK, POOL_BLOCK), np.float32)
    inv_cnt = np.zeros((POOL_BLOCK, POOL_WIDTH), np.float32)
    for g, w in enumerate(POOL_WINDOWS):
        left = w // 2
        right = w - 1 - left
        for t in range(POOL_BLOCK):
            row, pos = divmod(t, width)
            lo = min(max(pos - left, 0), width - 1)
            hi = min(max(pos + right, 0), width - 1)
            band[g, t, row * width + lo:row * width + hi + 1] = 1.0
            inv_cnt[t, g * POOL_GROUP_DIM:(g + 1) * POOL_GROUP_DIM] = 1.0 / (hi - lo + 1)
    return band, inv_cnt


@functools.lru_cache(maxsize=None)
def _pos_table(rows):
    quarter = D_MODEL // 4
    omega = 1.0 / (POS_BASE ** (np.arange(quarter, dtype=np.float64) / quarter))
    t = np.arange(rows * GRID_W)
    r = (t // GRID_W).astype(np.float64)
    col = (t % GRID_W).astype(np.float64)
    ar = r[:, None] * omega[None, :]
    ac = col[:, None] * omega[None, :]
    return np.concatenate([np.sin(ar), np.cos(ar), np.sin(ac), np.cos(ac)], axis=-1).astype(np.float32)


def _mod_kernel(c_ref, w_ref, b_ref, o_ref):
    @pl.when(pl.program_id(1) == 0)
    def _():
        o_ref[...] = jnp.broadcast_to(b_ref[...], o_ref.shape)

    c = c_ref[...]
    a = (c * jax.nn.sigmoid(c)).astype(BF16)
    o_ref[...] += _dot(a, w_ref[...].astype(BF16))


def _modulation(cond, w_ada, b_ada):
    tk = 256
    n = 6 * D_MODEL
    return pl.pallas_call(
        _mod_kernel,
        name="adaln_mod",
        grid=(DEPTH, D_MODEL // tk),
        in_specs=[
            pl.BlockSpec((COND_ROWS, tk), lambda l, k: (0, k)),
            pl.BlockSpec((None, tk, n), lambda l, k: (l, k, 0)),
            pl.BlockSpec((None, 1, n), lambda l, k: (l, 0, 0)),
        ],
        out_specs=pl.BlockSpec((None, COND_ROWS, n), lambda l, k: (l, 0, 0)),
        out_shape=jax.ShapeDtypeStruct((DEPTH, COND_ROWS, 6 * D_MODEL), F32),
        compiler_params=pltpu.CompilerParams(
            dimension_semantics=("arbitrary", "arbitrary"), vmem_limit_bytes=VMEM_LIMIT),
    )(cond, w_ada, b_ada.reshape(DEPTH, 1, 6 * D_MODEL))


def _gate_rows(g, bias_col, rows_ref, chunk):
    x = g.T[:N_GATE_COLS, :] + bias_col
    row = lax.broadcasted_iota(jnp.int32, x.shape, 0)
    fwd = row < 2 * HEADS
    lf = _log_sigmoid(x)
    b_f = jnp.where(fwd, _scan_lanes(lf, False, jnp.add, 0.0), _scan_lanes(lf, True, jnp.add, 0.0))
    b_i = pltpu.roll(b_f, N_GATE_COLS - HEADS, 0)
    a = x - b_i
    p = jnp.where(fwd, _scan_lanes(a, False, jnp.maximum, -jnp.inf), _scan_lanes(a, True, jnp.maximum, -jnp.inf))
    sl = slice(chunk * N_GATE_COLS, (chunk + 1) * N_GATE_COLS)
    rows_ref[0, sl, :] = a
    rows_ref[1, sl, :] = b_i
    rows_ref[2, sl, :] = b_i + p
    rows_ref[3, sl, :] = p


def _inproj_kernel(*refs, add_pos, chunk):
    it = iter(refs)
    x_ref = next(it)
    pos_ref = next(it) if add_pos else None
    mod_ref, ng_ref, w_ref, bg_ref = (next(it) for _ in range(4))
    uf_ref, up_ref, qkvo_ref, rows_ref, mg_ref = (next(it) for _ in range(5))

    x = x_ref[...]
    if add_pos:
        x = x + pos_ref[...]
    h = (_rms(x, ng_ref[0:1, :] * (1.0 + mod_ref[1:2, :])) + mod_ref[0:1, :]).astype(BF16)
    ck = 512
    za = _dot(h, w_ref[:, 0:ck])
    uf_ref[...] = za[:, :FOURIER_WIDTH].astype(BF16)
    up_ref[...] = za[:, FOURIER_WIDTH:].astype(BF16)
    for j in range(4 * MLSTM_WIDTH // ck):
        z = _dot(h, w_ref[:, (j + 1) * ck:(j + 2) * ck])
        if j * ck >= 3 * MLSTM_WIDTH:
            z = _sigmoid(z)
        qkvo_ref[:, j * ck:(j + 1) * ck] = z.astype(BF16)
    g = _dot(h, w_ref[:, CAT_G:CAT_W])
    for c in range(g.shape[0] // chunk):
        _gate_rows(g[c * chunk:(c + 1) * chunk, :], bg_ref[...], rows_ref, c)
    for j in range(3 * D_MODEL // ck):
        mg_ref[:, j * ck:(j + 1) * ck] = _sigmoid(_dot(h, w_ref[:, CAT_M + j * ck:CAT_M + (j + 1) * ck])).astype(BF16)


def _layer_spec(a, layer):
    return pl.BlockSpec((None,) + a.shape[1:], lambda *_: (layer,) + (0,) * (a.ndim - 1),
                        pipeline_mode=pl.Buffered(1))


def _inproj(x, mod, norm_g, w_cat, layer, b_gate, tm, pos=None):
    b, l, _ = x.shape
    add_pos = pos is not None
    t = min(MLSTM_CHUNK, l)
    assert tm % t == 0 and l % tm == 0
    bias_col = b_gate.reshape(N_GATE_COLS, 1)
    per_batch_mod = mod.shape[0] > 1
    tok = lambda w: pl.BlockSpec((None, tm, w), lambda i, j: (i, j, 0))
    full = lambda a: pl.BlockSpec(a.shape, lambda i, j: (0,) * a.ndim)
    in_specs = [tok(D_MODEL)]
    args = [x]
    if add_pos:
        in_specs.append(pl.BlockSpec((tm, D_MODEL), lambda i, j: (j, 0)))
        args.append(pos)
    in_specs += [
        pl.BlockSpec((None, 6, D_MODEL), (lambda i, j: (i, 0, 0)) if per_batch_mod else (lambda i, j: (0, 0, 0))),
        full(norm_g), _layer_spec(w_cat, layer), full(bias_col)]
    args += [mod, norm_g, w_cat, bias_col]
    rows_per_tile = tm // t * N_GATE_COLS
    out_specs = [tok(FOURIER_WIDTH), tok(POOL_WIDTH), tok(4 * MLSTM_WIDTH),
                 pl.BlockSpec((None, 4, rows_per_tile, t), lambda i, j: (i, 0, j, 0)), tok(3 * D_MODEL)]
    out_shape = [
        jax.ShapeDtypeStruct((b, l, FOURIER_WIDTH), BF16),
        jax.ShapeDtypeStruct((b, l, POOL_WIDTH), BF16),
        jax.ShapeDtypeStruct((b, l, 4 * MLSTM_WIDTH), BF16),
        jax.ShapeDtypeStruct((b, 4, l // t * N_GATE_COLS, t), F32),
        jax.ShapeDtypeStruct((b, l, 3 * D_MODEL), BF16)]
    return pl.pallas_call(
        functools.partial(_inproj_kernel, add_pos=add_pos, chunk=t),
        name="inproj",
        grid=(b, l // tm),
        in_specs=in_specs, out_specs=out_specs, out_shape=out_shape,
        compiler_params=pltpu.CompilerParams(
            dimension_semantics=("arbitrary", "arbitrary"), vmem_limit_bytes=VMEM_LIMIT),
    )(*args)


def _fourier_kernel(u_ref, ce_ref, se_ref, co_ref, so_ref, bc_ref, bs_ref, o_ref, vc_ref, vs_ref, *, scale, half):
    u = u_ref[...]
    nblk = u.shape[1] // LANES
    for ref, tab in ((vc_ref, bc_ref), (vs_ref, bs_ref)):
        v = _dot(u, tab[...])
        for k in range(nblk):
            ref[k] = v[:, k * LANES:(k + 1) * LANES]

    def rows(ref, start):
        return jnp.concatenate([ref[k, pl.ds(start, half, stride=2), :] for k in range(nblk)], axis=1).astype(BF16)

    e = _dot(ce_ref[...], rows(vc_ref, 0)) - _dot(se_ref[...], rows(vs_ref, 0))
    o = _dot(co_ref[...], rows(vc_ref, 1)) - _dot(so_ref[...], rows(vs_ref, 1))
    o_ref[0:half, :] = ((e + o) * scale).astype(BF16)
    o_ref[half:, :] = ((e - o) * scale).astype(BF16)


def _fourier(uf):
    b, l, w = uf.shape
    half = l // 2
    bc, bs = (jnp.asarray(t).astype(BF16) for t in _group_dft_tables())
    tables = [jnp.asarray(t).astype(BF16) for t in _dft_half_tables(l)]
    full = lambda a: pl.BlockSpec(a.shape, lambda i: (0,) * a.ndim)
    seq = pl.BlockSpec((None, l, w), lambda i: (i, 0, 0))
    return pl.pallas_call(
        functools.partial(_fourier_kernel, scale=float((l * FOURIER_GROUP_DIM) ** -0.5), half=half),
        name="fourier",
        grid=(b,),
        in_specs=[seq] + [full(t) for t in tables] + [full(bc), full(bs)],
        out_specs=seq,
        out_shape=jax.ShapeDtypeStruct((b, l, w), BF16),
        scratch_shapes=[pltpu.VMEM((w // LANES, l, LANES), F32), pltpu.VMEM((w // LANES, l, LANES), F32)],
        compiler_params=pltpu.CompilerParams(
            dimension_semantics=("arbitrary",), vmem_limit_bytes=VMEM_LIMIT),
    )(uf, *tables, bc, bs)


def _scan_lanes(x, reverse, op, fill):
    n = x.shape[-1]
    lane = lax.broadcasted_iota(jnp.int32, x.shape, x.ndim - 1)
    s = 1
    while s < n:
        if reverse:
            x = op(x, jnp.where(lane < n - s, pltpu.roll(x, n - s, x.ndim - 1), fill))
        else:
            x = op(x, jnp.where(lane >= s, pltpu.roll(x, s, x.ndim - 1), fill))
        s *= 2
    return x


SPLIT_ROWS = 16


def _mlstm_kernel(*refs, seq, chunk, hp, has_init, emit_state, unroll):
    it = iter(refs)
    q_ref, k_ref, v_ref, rows_ref = (next(it) for _ in range(4))
    c0_ref, n0_ref, m0_ref = (next(it) for _ in range(3)) if has_init else (None, None, None)
    hm_ref = next(it)
    cs_ref, ns_ref, ms_ref = (next(it) for _ in range(3)) if emit_state else (None, None, None)
    hf_ref, hb_ref = (next(it) for _ in range(2))

    t = chunk
    nc = seq // t
    rep = t // DH
    scale = DH ** -0.5

    row_i = lax.broadcasted_iota(jnp.int32, (t, t), 0)
    col_i = lax.broadcasted_iota(jnp.int32, (t, t), 1)
    eye_dh = jnp.where(lax.broadcasted_iota(jnp.int32, (DH, DH), 0) == lax.broadcasted_iota(jnp.int32, (DH, DH), 1),
                       1.0, 0.0).astype(BF16)
    sub = lax.broadcasted_iota(jnp.int32, (SPLIT_ROWS, t), 0)
    part = sub % 3
    p_sub = lax.broadcasted_iota(jnp.int32, (SPLIT_ROWS, 2 * DH), 0)
    p_lane = lax.broadcasted_iota(jnp.int32, (SPLIT_ROWS, 2 * DH), 1)
    gather_mat = jnp.where(((p_sub < 3) & (p_lane < DH)) | ((p_sub >= 3) & (p_sub < 6) & (p_lane >= DH)),
                           1.0, 0.0).astype(BF16)
    ones_cols = jnp.ones((t, DH), BF16)

    def step(c, hh, direction, Cn, m, h_ref):
        rows = pl.ds(pl.multiple_of(c * t, t), t)
        cols = slice(hh * DH, (hh + 1) * DH)
        r = pl.ds(c * N_GATE_COLS + direction * 2 * HEADS + pl.program_id(1) * hp + hh, 1)
        a_r = rows_ref[0, r, :]
        b_r = rows_ref[1, r, :]
        g_r = rows_ref[2, r, :]
        p_r = rows_ref[3, r, :]
        if direction == 0:
            end = t - 1
            mask = col_i <= row_i
        else:
            end = 0
            mask = col_i >= row_i
        b_end = b_r[:, end:end + 1]
        p_end = p_r[:, end:end + 1]

        x0 = jnp.where(sub < 3, b_r, jnp.where(sub < 6, g_r, 0.0))
        x1 = x0 - x0.astype(BF16).astype(F32)
        x2 = x1 - x1.astype(BF16).astype(F32)
        xs = jnp.where(part == 0, x0, jnp.where(part == 1, x1, x2)).astype(BF16)
        bg = lax.dot_general(xs, gather_mat, (((0,), (0,)), ((), ())), preferred_element_type=F32)
        b_c = bg[:, :DH]
        m_t = jnp.maximum(b_c + m, bg[:, DH:])
        mu = m_t - b_c
        sdec = jnp.exp(m - mu)

        qc = q_ref[rows, cols]
        kc = k_ref[rows, cols]
        v1 = jnp.concatenate([v_ref[rows, cols], ones_cols], axis=1)
        mu_t = jnp.concatenate([mu] * rep, axis=1)
        d = jnp.where(mask, jnp.exp((a_r + np.float32(np.log(scale))) - mu_t), 0.0)
        s = lax.dot_general(qc, kc, (((1,), (1,)), ((), ())), preferred_element_type=F32) * d
        qs = _dot(qc, Cn.astype(BF16))
        sv = _dot(s.astype(BF16), v1)
        num = sdec * qs[:, :DH] + sv[:, :DH]
        den = sdec * qs[:, DH:] + sv[:, DH:]
        h_ref[rows, cols] = num * (1.0 / jnp.maximum(jnp.abs(den), jnp.exp(-m_t)))

        mx = jnp.maximum(m, p_end)
        w_r = jnp.exp(a_r - mx) * scale
        kt = lax.dot_general(eye_dh, kc, (((1,), (1,)), ((), ())), preferred_element_type=F32)
        Cn_new = jnp.exp(m - mx) * Cn + _dot((kt * w_r).astype(BF16), v1)
        return Cn_new, b_end + mx

    def body(ci, carry):
        out = []
        for hh in range(hp):
            Cf, mf, Cb, mb = carry[4 * hh:4 * hh + 4]
            Cf, mf = step(ci, hh, 0, Cf, mf, hf_ref)
            Cb, mb = step(nc - 1 - ci, hh, 1, Cb, mb, hb_ref)
            out += [Cf, mf, Cb, mb]
        return tuple(out)

    init = []
    for hh in range(hp):
        for d in range(N_DIR):
            if has_init:
                n_col = jnp.sum(eye_dh.astype(F32) * n0_ref[d, hh], axis=1, keepdims=True)
                init += [jnp.concatenate([c0_ref[d, hh], jnp.broadcast_to(n_col, (DH, DH))], axis=1),
                         m0_ref[d, hh][:, 0:1]]
            else:
                init += [jnp.zeros((DH, 2 * DH), F32), jnp.zeros((1, 1), F32)]
    final = lax.fori_loop(0, nc, body, tuple(init), unroll=unroll)

    hm_ref[...] = (hf_ref[...] + hb_ref[...]).astype(BF16)
    if emit_state:
        eye_f = eye_dh.astype(F32)
        for hh in range(hp):
            for d in range(N_DIR):
                Cn, m = final[2 * (hh * N_DIR + d):2 * (hh * N_DIR + d) + 2]
                cs_ref[d, hh] = Cn[:, :DH]
                ns_ref[d, hh] = jnp.sum(eye_f * Cn[:, DH:], axis=0, keepdims=True)
                ms_ref[d, hh] = jnp.broadcast_to(m, (1, LANES))


def _mlstm(qkvo, grows, init, emit_state, hp, unroll):
    b, l, _ = qkvo.shape
    t = grows.shape[-1]
    nc = l // t
    assert l % t == 0 and t % DH == 0 and HEADS % hp == 0 and nc % unroll == 0
    has_init = init is not None
    groups = HEADS // hp
    head_cols = lambda k: pl.BlockSpec((None, l, hp * DH), lambda i, h: (i, 0, k * groups + h))
    state_c = pl.BlockSpec((None, N_DIR, hp, DH, DH), lambda i, h: (i, 0, h, 0, 0))
    state_v = pl.BlockSpec((None, N_DIR, hp, 1, LANES), lambda i, h: (i, 0, h, 0, 0))
    in_specs = [
        head_cols(0), head_cols(1), head_cols(2),
        pl.BlockSpec((None,) + grows.shape[1:], lambda i, h: (i, 0, 0, 0))]
    args = [qkvo, qkvo, qkvo, grows]
    if has_init:
        c0, n0, m0, layer = init
        in_specs += [
            pl.BlockSpec((None, None, N_DIR, hp, DH, DH), lambda i, h: (i, layer, 0, h, 0, 0)),
            pl.BlockSpec((None, None, N_DIR, hp, 1, DH), lambda i, h: (i, layer, 0, h, 0, 0)),
            pl.BlockSpec((None, None, N_DIR, hp, 1, LANES), lambda i, h: (i, layer, 0, h, 0, 0))]
        args += [c0, n0, m0]
    out_specs = [pl.BlockSpec((None, l, hp * DH), lambda i, h: (i, 0, h))]
    out_shape = [jax.ShapeDtypeStruct((b, l, MLSTM_WIDTH), BF16)]
    if emit_state:
        out_specs += [state_c, state_v, state_v]
        out_shape += [jax.ShapeDtypeStruct((b, N_DIR, HEADS, DH, DH), F32),
                      jax.ShapeDtypeStruct((b, N_DIR, HEADS, 1, DH), F32),
                      jax.ShapeDtypeStruct((b, N_DIR, HEADS, 1, LANES), F32)]
    return pl.pallas_call(
        functools.partial(_mlstm_kernel, seq=l, chunk=t, hp=hp, has_init=has_init, emit_state=emit_state,
                          unroll=unroll),
        name="mlstm",
        grid=(b, groups),
        in_specs=in_specs, out_specs=out_specs, out_shape=out_shape,
        scratch_shapes=[
            pltpu.VMEM((l, hp * DH), F32),
            pltpu.VMEM((l, hp * DH), F32)],
        compiler_params=pltpu.CompilerParams(
            dimension_semantics=("arbitrary", "arbitrary"), vmem_limit_bytes=VMEM_LIMIT),
    )(*args)


def _merge_ffn_kernel(*refs, tm, add_pos):
    it = iter(refs)
    x_ref = next(it)
    pos_ref = next(it) if add_pos else None
    (mod_ref, ng_ref, yf_ref, up_ref, hm_ref, og_ref, mg_ref, band_ref, icnt_ref, wpool_ref, pscale_ref,
     wpf_ref, wpp_ref, wpm_ref, wout_ref, wfi_ref, wfo_ref, o_ref) = (next(it) for _ in range(18))
    x = x_ref[...]
    if add_pos:
        x = x + pos_ref[...]

    lane_group = lax.broadcasted_iota(jnp.int32, (POOL_BLOCK, POOL_WIDTH), 1) // POOL_GROUP_DIM
    pooled = []
    for r in range(tm // POOL_BLOCK):
        u = up_ref[r * POOL_BLOCK:(r + 1) * POOL_BLOCK, :]
        acc = jnp.zeros((POOL_BLOCK, POOL_WIDTH), F32)
        for g in range(len(POOL_WINDOWS)):
            acc = jnp.where(lane_group == g, _dot(band_ref[g], u), acc)
        pooled.append((acc * icnt_ref[...] - u.astype(F32)).astype(BF16))
    p = pooled[0] if len(pooled) == 1 else jnp.concatenate(pooled, axis=0)
    pp = (_dot(p, wpool_ref[...]) * pscale_ref[...]).astype(BF16)

    y_f = _dot(yf_ref[...], wpf_ref[...])
    y_p = _dot(pp, wpp_ref[...])
    hg = (og_ref[...].astype(F32) * hm_ref[...].astype(F32)).astype(BF16)
    y_m = _dot(hg, wpm_ref[...])
    g_f = mg_ref[:, 0:D_MODEL].astype(F32)
    g_p = mg_ref[:, D_MODEL:2 * D_MODEL].astype(F32)
    g_m = mg_ref[:, 2 * D_MODEL:3 * D_MODEL].astype(F32)
    y = (g_f * y_f + g_p * y_p + g_m * y_m).astype(BF16)
    x1 = x + _rms(_dot(y, wout_ref[...]), ng_ref[1:2, :] * mod_ref[2:3, :])

    h2 = (_rms(x1, ng_ref[2:3, :] * (1.0 + mod_ref[4:5, :])) + mod_ref[3:4, :]).astype(BF16)
    acc = jnp.zeros((tm, D_MODEL), F32)
    for c0, c1 in FFN_CHUNKS:
        a = _dot(h2, wfi_ref[:, c0:c1])
        bb = _dot(h2, wfi_ref[:, D_FF + c0:D_FF + c1])
        act = (a * _sigmoid(a) * bb).astype(BF16)
        acc = acc + _dot(act, wfo_ref[c0:c1, :])
    o_ref[...] = x1 + _rms(acc, ng_ref[3:4, :] * mod_ref[5:6, :])


def _merge_ffn(x, mod, norm_g, yf, up, hm, qkvo, mg, band, icnt, wpool, pscale, lw, layer, tm, pos=None):
    b, l, _ = x.shape
    add_pos = pos is not None
    ogate = pl.BlockSpec((None, tm, MLSTM_WIDTH), lambda i, j: (i, j, 3))
    per_batch_mod = mod.shape[0] > 1
    tok = lambda w: pl.BlockSpec((None, tm, w), lambda i, j: (i, j, 0))
    full = lambda a: pl.BlockSpec(a.shape, lambda i, j: (0,) * a.ndim)
    stacked = [lw[k] for k in ("wpf", "wpp", "wpm", "wout", "wfi", "wfo")]
    in_specs = [tok(D_MODEL)]
    args = [x]
    if add_pos:
        in_specs.append(pl.BlockSpec((tm, D_MODEL), lambda i, j: (j, 0)))
        args.append(pos)
    in_specs += [
        pl.BlockSpec((None, 6, D_MODEL), (lambda i, j: (i, 0, 0)) if per_batch_mod else (lambda i, j: (0, 0, 0))),
        full(norm_g), tok(FOURIER_WIDTH), tok(POOL_WIDTH), tok(MLSTM_WIDTH), ogate, tok(3 * D_MODEL),
        full(band), full(icnt), full(wpool), full(pscale)]
    in_specs += [_layer_spec(a, layer) for a in stacked]
    args += [mod, norm_g, yf, up, hm, qkvo, mg, band, icnt, wpool, pscale] + stacked
    return pl.pallas_call(
        functools.partial(_merge_ffn_kernel, tm=tm, add_pos=add_pos),
        name="merge_ffn",
        grid=(b, l // tm),
        in_specs=in_specs,
        out_specs=tok(D_MODEL),
        out_shape=jax.ShapeDtypeStruct((b, l, D_MODEL), F32),
        compiler_params=pltpu.CompilerParams(
            dimension_semantics=("arbitrary", "arbitrary"), vmem_limit_bytes=VMEM_LIMIT),
    )(*args)


def _stacked_weights(w_in, w_proj_f, w_proj_p, w_proj_m, w_out, w_ffn_in, w_ffn_out):
    pad = jnp.zeros(w_in.shape[:2] + (LANES - N_GATE_COLS,), w_in.dtype)
    w_cat = jnp.concatenate([w_in[..., :OFF_G], w_in[..., OFF_M:], w_in[..., OFF_G:OFF_M], pad], axis=-1)
    return dict(
        w_cat=w_cat.astype(BF16),
        wpf=w_proj_f.astype(BF16), wpp=w_proj_p.astype(BF16), wpm=w_proj_m.astype(BF16),
        wout=w_out.astype(BF16), wfi=w_ffn_in.astype(BF16), wfo=w_ffn_out.astype(BF16))


def _block(x, mod, lw, layer, norm_g, b_gate, wpool, pscale, width, tm, init, emit_state, hp, unroll, pos=None):
    uf, up, qkvo, grows, mg = _inproj(x, mod, norm_g, lw["w_cat"], layer, b_gate, tm, pos)
    yf = _fourier(uf)
    mouts = _mlstm(qkvo, grows, init, emit_state, hp, unroll)
    band, icnt = _pool_tables(width)
    x = _merge_ffn(x, mod, norm_g, yf, up, mouts[0], qkvo, mg, jnp.asarray(band).astype(BF16), jnp.asarray(icnt),
                   wpool, pscale, lw, layer, tm, pos)
    return x, mouts[1:]


def kernel(x_prompt, x_sample, state_C, state_n, state_m, c, c_ctx, w_ada, b_ada, norm_g, w_in, b_gate, w_proj_f, w_pool, pool_scale, w_proj_p, w_proj_m, w_out, w_ffn_in, w_ffn_out):
    bp, lp, _ = x_prompt.shape
    bs, ls, _ = x_sample.shape
    cond = jnp.concatenate([c_ctx[None, :], c, jnp.zeros((COND_ROWS - 1 - bs, D_MODEL), F32)], axis=0)
    mod = _modulation(cond, w_ada, b_ada).reshape(DEPTH, COND_ROWS, 6, D_MODEL)
    lw = _stacked_weights(w_in, w_proj_f, w_proj_p, w_proj_m, w_out, w_ffn_in, w_ffn_out)
    layer_args = [(lw, i, norm_g[i], b_gate[i], jax.scipy.linalg.block_diag(*w_pool[i]).astype(BF16),
                   pool_scale[i].reshape(1, POOL_WIDTH)) for i in range(DEPTH)]

    xp = x_prompt
    cs, ns, ms = [], [], []
    for i in range(DEPTH):
        xp, (c_i, n_i, m_i) = _block(xp, mod[i, 0:1], *layer_args[i], lp, min(lp, 512), None, True, HEADS, 1)
        cs.append(c_i)
        ns.append(n_i[:, :, :, 0, :])
        ms.append(m_i[:, :, :, 0, 0])

    xs = x_sample
    pos = jnp.asarray(_pos_table(ls // GRID_W))
    n0 = state_n.reshape(bs, DEPTH, N_DIR, HEADS, 1, DH)
    m0 = jnp.broadcast_to(state_m[..., None, None], (bs, DEPTH, N_DIR, HEADS, 1, LANES))
    for i in range(DEPTH):
        init = (state_C, n0, m0, i)
        xs, _ = _block(xs, mod[i, 1:1 + bs], *layer_args[i], GRID_W, 512, init, False, 1, 2, pos if i == 0 else None)

    return (xp, xs, jnp.stack(cs, axis=1), jnp.stack(ns, axis=1), jnp.stack(ms, axis=1))
```

```python
import functools

import numpy as np
import jax
import jax.numpy as jnp
from jax import lax
from jax.experimental import pallas as pl
from jax.experimental.pallas import tpu as pltpu

F32 = jnp.float32
BF16 = jnp.bfloat16

D_MODEL = 1024
DEPTH = 2
GRID_W = 64
FOURIER_GROUPS = 4
FOURIER_WIDTH = D_MODEL // 4
FOURIER_GROUP_DIM = FOURIER_WIDTH // FOURIER_GROUPS
POOL_WINDOWS = (2, 4, 8, 16)
POOL_WIDTH = D_MODEL // 4
POOL_GROUP_DIM = POOL_WIDTH // len(POOL_WINDOWS)
HEADS = 4
MLSTM_WIDTH = D_MODEL // 2
DH = MLSTM_WIDTH // HEADS
N_DIR = 2
N_GATE_COLS = N_DIR * 2 * HEADS
D_FF = -(-8 * D_MODEL // (3 * 256)) * 256
RMS_EPS = 1e-6
POS_BASE = 10000.0

OFF_F = 0
OFF_P = OFF_F + FOURIER_WIDTH
OFF_QKVO = OFF_P + POOL_WIDTH
OFF_G = OFF_QKVO + 4 * MLSTM_WIDTH
OFF_M = OFF_G + N_GATE_COLS
D_IN = OFF_M + 3 * D_MODEL

LANES = 128
MLSTM_CHUNK = 256
POOL_BLOCK = 256
TOKEN_TILE = 512
COND_ROWS = 16
VMEM_LIMIT = 60 * 1024 * 1024
MXU_COLS = 256
FFN_CHUNKS = ((0, 5 * MXU_COLS), (5 * MXU_COLS, D_FF))
CAT_M = OFF_G
CAT_G = CAT_M + 3 * D_MODEL
CAT_W = CAT_G + LANES


def _dot(a, b):
    return jnp.dot(a, b, preferred_element_type=F32)


def _rms(x, g):
    return x * lax.rsqrt(jnp.mean(x * x, axis=-1, keepdims=True) + RMS_EPS) * g


def _log_sigmoid(x):
    return jnp.minimum(x, 0.0) - jnp.log(1.0 + jnp.exp(-jnp.abs(x)))


def _sigmoid(x):
    return 0.5 * jnp.tanh(0.5 * x) + 0.5


@functools.lru_cache(maxsize=None)
def _dft_tables(n):
    idx = np.arange(n, dtype=np.int64)
    ang = 2.0 * np.pi * ((idx[:, None] * idx[None, :]) % n).astype(np.float64) / n
    return np.cos(ang).astype(np.float32), np.sin(ang).astype(np.float32)


@functools.lru_cache(maxsize=None)
def _dft_half_tables(n):
    f = np.arange(n // 2, dtype=np.int64)[:, None]
    j = np.arange(n // 2, dtype=np.int64)[None, :]
    out = []
    for l in (2 * j, 2 * j + 1):
        ang = 2.0 * np.pi * ((f * l) % n).astype(np.float64) / n
        out += [np.cos(ang).astype(np.float32), np.sin(ang).astype(np.float32)]
    return tuple(out)


@functools.lru_cache(maxsize=None)
def _group_dft_tables():
    c, s = _dft_tables(FOURIER_GROUP_DIM)
    bc = np.zeros((FOURIER_WIDTH, FOURIER_WIDTH), np.float32)
    bs = np.zeros((FOURIER_WIDTH, FOURIER_WIDTH), np.float32)
    for g in range(FOURIER_GROUPS):
        sl = slice(g * FOURIER_GROUP_DIM, (g + 1) * FOURIER_GROUP_DIM)
        bc[sl, sl] = c
        bs[sl, sl] = s
    return bc, bs


@functools.lru_cache(maxsize=None)
def _pool_tables(width):
    band = np.zeros((len(POOL_WINDOWS), POOL_BLOCK, POOL_BLOCK), np.float32)
    inv_cnt = np.zeros((POOL_BLOCK, POOL_WIDTH), np.float32)
    for g, w in enumerate(POOL_WINDOWS):
        left = w // 2
        right = w - 1 - left
        for t in range(POOL_BLOCK):
            row, pos = divmod(t, width)
            lo = min(max(pos - left, 0), width - 1)
            hi = min(max(pos + right, 0), width - 1)
            band[g, t, row * width + lo:row * width + hi + 1] = 1.0
            inv_cnt[t, g * POOL_GROUP_DIM:(g + 1) * POOL_GROUP_DIM] = 1.0 / (hi - lo + 1)
    return band, inv_cnt


@functools.lru_cache(maxsize=None)
def _pos_table(rows):
    quarter = D_MODEL // 4
    omega = 1.0 / (POS_BASE ** (np.arange(quarter, dtype=np.float64) / quarter))
    t = np.arange(rows * GRID_W)
    r = (t // GRID_W).astype(np.float64)
    col = (t % GRID_W).astype(np.float64)
    ar = r[:, None] * omega[None, :]
    ac = col[:, None] * omega[None, :]
    return np.concatenate([np.sin(ar), np.cos(ar), np.sin(ac), np.cos(ac)], axis=-1).astype(np.float32)


def _mod_kernel(c_ref, w_ref, b_ref, o_ref):
    @pl.when(pl.program_id(1) == 0)
    def _():
        o_ref[...] = jnp.broadcast_to(b_ref[...], o_ref.shape)

    c = c_ref[...]
    a = (c * jax.nn.sigmoid(c)).astype(BF16)
    o_ref[...] += _dot(a, w_ref[...].astype(BF16))


def _modulation(cond, w_ada, b_ada):
    tk = 256
    n = 6 * D_MODEL
    return pl.pallas_call(
        _mod_kernel,
        name="adaln_mod",
        grid=(DEPTH, D_MODEL // tk),
        in_specs=[
            pl.BlockSpec((COND_ROWS, tk), lambda l, k: (0, k)),
            pl.BlockSpec((None, tk, n), lambda l, k: (l, k, 0)),
            pl.BlockSpec((None, 1, n), lambda l, k: (l, 0, 0)),
        ],
        out_specs=pl.BlockSpec((None, COND_ROWS, n), lambda l, k: (l, 0, 0)),
        out_shape=jax.ShapeDtypeStruct((DEPTH, COND_ROWS, 6 * D_MODEL), F32),
        compiler_params=pltpu.CompilerParams(
            dimension_semantics=("arbitrary", "arbitrary"), vmem_limit_bytes=VMEM_LIMIT),
    )(cond, w_ada, b_ada.reshape(DEPTH, 1, 6 * D_MODEL))


def _gate_rows(g, bias_col, rows_ref, chunk):
    x = g.T[:N_GATE_COLS, :] + bias_col
    row = lax.broadcasted_iota(jnp.int32, x.shape, 0)
    fwd = row < 2 * HEADS
    lf = _log_sigmoid(x)
    b_f = jnp.where(fwd, _scan_lanes(lf, False, jnp.add, 0.0), _scan_lanes(lf, True, jnp.add, 0.0))
    b_i = pltpu.roll(b_f, N_GATE_COLS - HEADS, 0)
    a = x - b_i
    p = jnp.where(fwd, _scan_lanes(a, False, jnp.maximum, -jnp.inf), _scan_lanes(a, True, jnp.maximum, -jnp.inf))
    sl = slice(chunk * N_GATE_COLS, (chunk + 1) * N_GATE_COLS)
    rows_ref[0, sl, :] = a
    rows_ref[1, sl, :] = b_i
    rows_ref[2, sl, :] = b_i + p
    rows_ref[3, sl, :] = p


def _inproj_kernel(*refs, add_pos, chunk):
    it = iter(refs)
    x_ref = next(it)
    pos_ref = next(it) if add_pos else None
    mod_ref, ng_ref, w_ref, bg_ref = (next(it) for _ in range(4))
    uf_ref, up_ref, qkvo_ref, rows_ref, mg_ref = (next(it) for _ in range(5))

    x = x_ref[...]
    if add_pos:
        x = x + pos_ref[...]
    h = (_rms(x, ng_ref[0:1, :] * (1.0 + mod_ref[1:2, :])) + mod_ref[0:1, :]).astype(BF16)
    ck = 512
    za = _dot(h, w_ref[:, 0:ck])
    uf_ref[...] = za[:, :FOURIER_WIDTH].astype(BF16)
    up_ref[...] = za[:, FOURIER_WIDTH:].astype(BF16)
    for j in range(4 * MLSTM_WIDTH // ck):
        z = _dot(h, w_ref[:, (j + 1) * ck:(j + 2) * ck])
        if j * ck >= 3 * MLSTM_WIDTH:
            z = _sigmoid(z)
        qkvo_ref[:, j * ck:(j + 1) * ck] = z.astype(BF16)
    g = _dot(h, w_ref[:, CAT_G:CAT_W])
    for c in range(g.shape[0] // chunk):
        _gate_rows(g[c * chunk:(c + 1) * chunk, :], bg_ref[...], rows_ref, c)
    for j in range(3 * D_MODEL // ck):
        mg_ref[:, j * ck:(j + 1) * ck] = _sigmoid(_dot(h, w_ref[:, CAT_M + j * ck:CAT_M + (j + 1) * ck])).astype(BF16)


def _layer_spec(a, layer):
    return pl.BlockSpec((None,) + a.shape[1:], lambda *_: (layer,) + (0,) * (a.ndim - 1),
                        pipeline_mode=pl.Buffered(1))


def _inproj(x, mod, norm_g, w_cat, layer, b_gate, tm, t, pos=None):
    b, l, _ = x.shape
    add_pos = pos is not None
    assert tm % t == 0 and l % tm == 0
    bias_col = b_gate.reshape(N_GATE_COLS, 1)
    per_batch_mod = mod.shape[0] > 1
    tok = lambda w: pl.BlockSpec((None, tm, w), lambda i, j: (i, j, 0))
    full = lambda a: pl.BlockSpec(a.shape, lambda i, j: (0,) * a.ndim)
    in_specs = [tok(D_MODEL)]
    args = [x]
    if add_pos:
        in_specs.append(pl.BlockSpec((tm, D_MODEL), lambda i, j: (j, 0)))
        args.append(pos)
    in_specs += [
        pl.BlockSpec((None, 6, D_MODEL), (lambda i, j: (i, 0, 0)) if per_batch_mod else (lambda i, j: (0, 0, 0))),
        full(norm_g), _layer_spec(w_cat, layer), full(bias_col)]
    args += [mod, norm_g, w_cat, bias_col]
    rows_per_tile = tm // t * N_GATE_COLS
    out_specs = [tok(FOURIER_WIDTH), tok(POOL_WIDTH), tok(4 * MLSTM_WIDTH),
                 pl.BlockSpec((None, 4, rows_per_tile, t), lambda i, j: (i, 0, j, 0)), tok(3 * D_MODEL)]
    out_shape = [
        jax.ShapeDtypeStruct((b, l, FOURIER_WIDTH), BF16),
        jax.ShapeDtypeStruct((b, l, POOL_WIDTH), BF16),
        jax.ShapeDtypeStruct((b, l, 4 * MLSTM_WIDTH), BF16),
        jax.ShapeDtypeStruct((b, 4, l // t * N_GATE_COLS, t), F32),
        jax.ShapeDtypeStruct((b, l, 3 * D_MODEL), BF16)]
    return pl.pallas_call(
        functools.partial(_inproj_kernel, add_pos=add_pos, chunk=t),
        name="inproj",
        grid=(b, l // tm),
        in_specs=in_specs, out_specs=out_specs, out_shape=out_shape,
        compiler_params=pltpu.CompilerParams(
            dimension_semantics=("arbitrary", "arbitrary"), vmem_limit_bytes=VMEM_LIMIT),
    )(*args)


def _fourier_kernel(u_ref, ce_ref, se_ref, co_ref, so_ref, bc_ref, bs_ref, o_ref, vc_ref, vs_ref, *, scale, half):
    u = u_ref[...]
    nblk = u.shape[1] // LANES
    for ref, tab in ((vc_ref, bc_ref), (vs_ref, bs_ref)):
        v = _dot(u, tab[...])
        for k in range(nblk):
            ref[k] = v[:, k * LANES:(k + 1) * LANES]

    def rows(ref, start):
        return jnp.concatenate([ref[k, pl.ds(start, half, stride=2), :] for k in range(nblk)], axis=1).astype(BF16)

    e = _dot(ce_ref[...], rows(vc_ref, 0)) - _dot(se_ref[...], rows(vs_ref, 0))
    o = _dot(co_ref[...], rows(vc_ref, 1)) - _dot(so_ref[...], rows(vs_ref, 1))
    o_ref[0:half, :] = ((e + o) * scale).astype(BF16)
    o_ref[half:, :] = ((e - o) * scale).astype(BF16)


def _fourier(uf):
    b, l, w = uf.shape
    half = l // 2
    bc, bs = (jnp.asarray(t).astype(BF16) for t in _group_dft_tables())
    tables = [jnp.asarray(t).astype(BF16) for t in _dft_half_tables(l)]
    full = lambda a: pl.BlockSpec(a.shape, lambda i: (0,) * a.ndim)
    seq = pl.BlockSpec((None, l, w), lambda i: (i, 0, 0))
    return pl.pallas_call(
        functools.partial(_fourier_kernel, scale=float((l * FOURIER_GROUP_DIM) ** -0.5), half=half),
        name="fourier",
        grid=(b,),
        in_specs=[seq] + [full(t) for t in tables] + [full(bc), full(bs)],
        out_specs=seq,
        out_shape=jax.ShapeDtypeStruct((b, l, w), BF16),
        scratch_shapes=[pltpu.VMEM((w // LANES, l, LANES), F32), pltpu.VMEM((w // LANES, l, LANES), F32)],
        compiler_params=pltpu.CompilerParams(
            dimension_semantics=("arbitrary",), vmem_limit_bytes=VMEM_LIMIT),
    )(uf, *tables, bc, bs)


def _scan_lanes(x, reverse, op, fill):
    n = x.shape[-1]
    lane = lax.broadcasted_iota(jnp.int32, x.shape, x.ndim - 1)
    s = 1
    while s < n:
        if reverse:
            x = op(x, jnp.where(lane < n - s, pltpu.roll(x, n - s, x.ndim - 1), fill))
        else:
            x = op(x, jnp.where(lane >= s, pltpu.roll(x, s, x.ndim - 1), fill))
        s *= 2
    return x


SPLIT_ROWS = 16


def _mlstm_kernel(*refs, seq, chunk, hp, seq_group, has_init, emit_state, n_alias, unroll):
    it = iter(refs)
    q_ref, k_ref, v_ref, rows_ref = (next(it) for _ in range(4))
    c0_ref, n0_ref, m0_ref = (next(it) for _ in range(3)) if has_init else (None, None, None)
    if n_alias:
        [next(it) for _ in range(n_alias)]
    hm_ref = next(it)
    cs_ref, ns_ref, ms_ref = (next(it) for _ in range(3)) if emit_state else (None, None, None)
    hf_ref, hb_ref = (next(it) for _ in range(2))

    t = chunk
    nc = seq // t
    rep = t // DH
    scale = DH ** -0.5

    row_i = lax.broadcasted_iota(jnp.int32, (t, t), 0)
    col_i = lax.broadcasted_iota(jnp.int32, (t, t), 1)
    eye_dh = jnp.where(lax.broadcasted_iota(jnp.int32, (DH, DH), 0) == lax.broadcasted_iota(jnp.int32, (DH, DH), 1),
                       1.0, 0.0).astype(BF16)
    sub = lax.broadcasted_iota(jnp.int32, (SPLIT_ROWS, t), 0)
    part = sub % 3
    p_sub = lax.broadcasted_iota(jnp.int32, (SPLIT_ROWS, 2 * DH), 0)
    p_lane = lax.broadcasted_iota(jnp.int32, (SPLIT_ROWS, 2 * DH), 1)
    gather_mat = jnp.where(((p_sub < 3) & (p_lane < DH)) | ((p_sub >= 3) & (p_sub < 6) & (p_lane >= DH)),
                           1.0, 0.0).astype(BF16)
    ones_cols = jnp.ones((t, DH), BF16)

    def step(c, hh, direction, Cn, m, h_ref):
        rows = pl.ds(pl.multiple_of(c * t, t), t)
        cols = slice(hh * DH, (hh + 1) * DH)
        chunk_row = (pl.program_id(0) % seq_group) * nc + c
        r = pl.ds(chunk_row * N_GATE_COLS + direction * 2 * HEADS + pl.program_id(1) * hp + hh, 1)
        a_r = rows_ref[0, r, :]
        b_r = rows_ref[1, r, :]
        g_r = rows_ref[2, r, :]
        p_r = rows_ref[3, r, :]
        if direction == 0:
            end = t - 1
            mask = col_i <= row_i
        else:
            end = 0
            mask = col_i >= row_i
        b_end = b_r[:, end:end + 1]
        p_end = p_r[:, end:end + 1]

        x0 = jnp.where(sub < 3, b_r, jnp.where(sub < 6, g_r, 0.0))
        x1 = x0 - x0.astype(BF16).astype(F32)
        x2 = x1 - x1.astype(BF16).astype(F32)
        xs = jnp.where(part == 0, x0, jnp.where(part == 1, x1, x2)).astype(BF16)
        bg = lax.dot_general(xs, gather_mat, (((0,), (0,)), ((), ())), preferred_element_type=F32)
        b_c = bg[:, :DH]
        m_t = jnp.maximum(b_c + m, bg[:, DH:])
        mu = m_t - b_c
        sdec = jnp.exp(m - mu)

        qc = q_ref[rows, cols]
        kc = k_ref[rows, cols]
        v1 = jnp.concatenate([v_ref[rows, cols], ones_cols], axis=1)
        mu_t = jnp.concatenate([mu] * rep, axis=1)
        d = jnp.where(mask, jnp.exp((a_r + np.float32(np.log(scale))) - mu_t), 0.0)
        s = lax.dot_general(qc, kc, (((1,), (1,)), ((), ())), preferred_element_type=F32) * d
        qs = _dot(qc, Cn.astype(BF16))
        sv = _dot(s.astype(BF16), v1)
        num = sdec * qs[:, :DH] + sv[:, :DH]
        den = sdec * qs[:, DH:] + sv[:, DH:]
        h_ref[rows, cols] = num * (1.0 / jnp.maximum(jnp.abs(den), jnp.exp(-m_t)))

        mx = jnp.maximum(m, p_end)
        w_r = jnp.exp(a_r - mx) * scale
        kt = lax.dot_general(eye_dh, kc, (((1,), (1,)), ((), ())), preferred_element_type=F32)
        Cn_new = jnp.exp(m - mx) * Cn + _dot((kt * w_r).astype(BF16), v1)
        return Cn_new, b_end + mx

    def body(ci, carry):
        out = []
        for hh in range(hp):
            Cf, mf, Cb, mb = carry[4 * hh:4 * hh + 4]
            Cf, mf = step(ci, hh, 0, Cf, mf, hf_ref)
            Cb, mb = step(nc - 1 - ci, hh, 1, Cb, mb, hb_ref)
            out += [Cf, mf, Cb, mb]
        return tuple(out)

    init = []
    for hh in range(hp):
        for d in range(N_DIR):
            if has_init:
                n_col = jnp.sum(eye_dh.astype(F32) * n0_ref[d, hh], axis=1, keepdims=True)
                init += [jnp.concatenate([c0_ref[d, hh], jnp.broadcast_to(n_col, (DH, DH))], axis=1),
                         m0_ref[d, hh][:, 0:1]]
            else:
                init += [jnp.zeros((DH, 2 * DH), F32), jnp.zeros((1, 1), F32)]
    final = lax.fori_loop(0, nc, body, tuple(init), unroll=unroll)

    hm_ref[...] = (hf_ref[...] + hb_ref[...]).astype(BF16)
    if emit_state:
        eye_f = eye_dh.astype(F32)
        for hh in range(hp):
            for d in range(N_DIR):
                Cn, m = final[2 * (hh * N_DIR + d):2 * (hh * N_DIR + d) + 2]
                cs_ref[d, hh] = Cn[:, :DH]
                ns_ref[d, hh] = jnp.sum(eye_f * Cn[:, DH:], axis=0, keepdims=True)
                ms_ref[d, hh] = jnp.broadcast_to(m, (1, LANES))


def _mlstm(qkvo, grows, init, state_out, hp, unroll):
    b, l, _ = qkvo.shape
    emit_state = state_out is not None
    t = grows.shape[-1]
    nc = l // t
    seq_group = b // grows.shape[0]
    assert l % t == 0 and t % DH == 0 and HEADS % hp == 0 and nc % unroll == 0
    assert grows.shape[2] == seq_group * nc * N_GATE_COLS
    has_init = init is not None
    groups = HEADS // hp
    head_cols = lambda k: pl.BlockSpec((None, l, hp * DH), lambda i, h: (i, 0, k * groups + h))
    in_specs = [
        head_cols(0), head_cols(1), head_cols(2),
        pl.BlockSpec((None,) + grows.shape[1:], lambda i, h: (i // seq_group, 0, 0, 0))]
    args = [qkvo, qkvo, qkvo, grows]
    if has_init:
        c0, n0, m0, layer = init
        in_specs += [
            pl.BlockSpec((None, None, N_DIR, hp, DH, DH), lambda i, h: (i, layer, 0, h, 0, 0)),
            pl.BlockSpec((None, None, N_DIR, hp, 1, DH), lambda i, h: (i, layer, 0, h, 0, 0)),
            pl.BlockSpec((None, None, N_DIR, hp, 1, LANES), lambda i, h: (i, layer, 0, h, 0, 0))]
        args += [c0, n0, m0]
    out_specs = [pl.BlockSpec((None, l, hp * DH), lambda i, h: (i, 0, h))]
    out_shape = [jax.ShapeDtypeStruct((b, l, MLSTM_WIDTH), BF16)]
    aliases = {}
    if emit_state:
        out_layer, prev = state_out
        slab = lambda w: pl.BlockSpec((None, None, N_DIR, hp, w, DH), lambda i, h: (i, out_layer, 0, h, 0, 0))
        out_specs += [slab(DH), slab(1), slab(1)]
        out_shape += [jax.ShapeDtypeStruct((b, DEPTH, N_DIR, HEADS, DH, DH), F32),
                      jax.ShapeDtypeStruct((b, DEPTH, N_DIR, HEADS, 1, DH), F32),
                      jax.ShapeDtypeStruct((b, DEPTH, N_DIR, HEADS, 1, DH), F32)]
        if prev is not None:
            aliases = {len(args) + k: 1 + k for k in range(len(prev))}
            in_specs += [pl.BlockSpec(memory_space=pl.ANY)] * len(prev)
            args += list(prev)
    return pl.pallas_call(
        functools.partial(_mlstm_kernel, seq=l, chunk=t, hp=hp, seq_group=seq_group, has_init=has_init,
                          emit_state=emit_state, n_alias=len(aliases), unroll=unroll),
        name="mlstm",
        grid=(b, groups),
        in_specs=in_specs, out_specs=out_specs, out_shape=out_shape,
        input_output_aliases=aliases,
        scratch_shapes=[
            pltpu.VMEM((l, hp * DH), F32),
            pltpu.VMEM((l, hp * DH), F32)],
        compiler_params=pltpu.CompilerParams(
            dimension_semantics=("arbitrary", "arbitrary"), vmem_limit_bytes=VMEM_LIMIT),
    )(*args)


def _merge_ffn_kernel(*refs, tm, add_pos):
    it = iter(refs)
    x_ref = next(it)
    pos_ref = next(it) if add_pos else None
    (mod_ref, ng_ref, yf_ref, up_ref, hm_ref, og_ref, mg_ref, band_ref, icnt_ref, wpool_ref, pscale_ref,
     wpf_ref, wpp_ref, wpm_ref, wout_ref, wfi_ref, wfo_ref, o_ref) = (next(it) for _ in range(18))
    x = x_ref[...]
    if add_pos:
        x = x + pos_ref[...]

    lane_group = lax.broadcasted_iota(jnp.int32, (POOL_BLOCK, POOL_WIDTH), 1) // POOL_GROUP_DIM
    pooled = []
    for r in range(tm // POOL_BLOCK):
        u = up_ref[r * POOL_BLOCK:(r + 1) * POOL_BLOCK, :]
        acc = jnp.zeros((POOL_BLOCK, POOL_WIDTH), F32)
        for g in range(len(POOL_WINDOWS)):
            acc = jnp.where(lane_group == g, _dot(band_ref[g], u), acc)
        pooled.append((acc * icnt_ref[...] - u.astype(F32)).astype(BF16))
    p = pooled[0] if len(pooled) == 1 else jnp.concatenate(pooled, axis=0)
    pp = (_dot(p, wpool_ref[...]) * pscale_ref[...]).astype(BF16)

    y_f = _dot(yf_ref[...], wpf_ref[...])
    y_p = _dot(pp, wpp_ref[...])
    hg = (og_ref[...].astype(F32) * hm_ref[...].astype(F32)).astype(BF16)
    y_m = _dot(hg, wpm_ref[...])
    g_f = mg_ref[:, 0:D_MODEL].astype(F32)
    g_p = mg_ref[:, D_MODEL:2 * D_MODEL].astype(F32)
    g_m = mg_ref[:, 2 * D_MODEL:3 * D_MODEL].astype(F32)
    y = (g_f * y_f + g_p * y_p + g_m * y_m).astype(BF16)
    x1 = x + _rms(_dot(y, wout_ref[...]), ng_ref[1:2, :] * mod_ref[2:3, :])

    h2 = (_rms(x1, ng_ref[2:3, :] * (1.0 + mod_ref[4:5, :])) + mod_ref[3:4, :]).astype(BF16)
    acc = jnp.zeros((tm, D_MODEL), F32)
    for c0, c1 in FFN_CHUNKS:
        a = _dot(h2, wfi_ref[:, c0:c1])
        bb = _dot(h2, wfi_ref[:, D_FF + c0:D_FF + c1])
        act = (a * _sigmoid(a) * bb).astype(BF16)
        acc = acc + _dot(act, wfo_ref[c0:c1, :])
    o_ref[...] = x1 + _rms(acc, ng_ref[3:4, :] * mod_ref[5:6, :])


def _merge_ffn(x, mod, norm_g, yf, up, hm, qkvo, mg, band, icnt, wpool, pscale, lw, layer, tm, pos=None):
    b, l, _ = x.shape
    add_pos = pos is not None
    ogate = pl.BlockSpec((None, tm, MLSTM_WIDTH), lambda i, j: (i, j, 3))
    per_batch_mod = mod.shape[0] > 1
    tok = lambda w: pl.BlockSpec((None, tm, w), lambda i, j: (i, j, 0))
    full = lambda a: pl.BlockSpec(a.shape, lambda i, j: (0,) * a.ndim)
    stacked = [lw[k] for k in ("wpf", "wpp", "wpm", "wout", "wfi", "wfo")]
    in_specs = [tok(D_MODEL)]
    args = [x]
    if add_pos:
        in_specs.append(pl.BlockSpec((tm, D_MODEL), lambda i, j: (j, 0)))
        args.append(pos)
    in_specs += [
        pl.BlockSpec((None, 6, D_MODEL), (lambda i, j: (i, 0, 0)) if per_batch_mod else (lambda i, j: (0, 0, 0))),
        full(norm_g), tok(FOURIER_WIDTH), tok(POOL_WIDTH), tok(MLSTM_WIDTH), ogate, tok(3 * D_MODEL),
        full(band), full(icnt), full(wpool), full(pscale)]
    in_specs += [_layer_spec(a, layer) for a in stacked]
    args += [mod, norm_g, yf, up, hm, qkvo, mg, band, icnt, wpool, pscale] + stacked
    return pl.pallas_call(
        functools.partial(_merge_ffn_kernel, tm=tm, add_pos=add_pos),
        name="merge_ffn",
        grid=(b, l // tm),
        in_specs=in_specs,
        out_specs=tok(D_MODEL),
        out_shape=jax.ShapeDtypeStruct((b, l, D_MODEL), F32),
        compiler_params=pltpu.CompilerParams(
            dimension_semantics=("arbitrary", "arbitrary"), vmem_limit_bytes=VMEM_LIMIT),
    )(*args)


def _stacked_weights(w_in, w_proj_f, w_proj_p, w_proj_m, w_out, w_ffn_in, w_ffn_out):
    wb = w_in.astype(BF16)
    pad = jnp.zeros(wb.shape[:2] + (LANES - N_GATE_COLS,), BF16)
    w_cat = jnp.concatenate([wb[..., :OFF_G], wb[..., OFF_M:], wb[..., OFF_G:OFF_M], pad], axis=-1)
    return dict(
        w_cat=w_cat,
        wpf=w_proj_f.astype(BF16), wpp=w_proj_p.astype(BF16), wpm=w_proj_m.astype(BF16),
        wout=w_out.astype(BF16), wfi=w_ffn_in.astype(BF16), wfo=w_ffn_out.astype(BF16))


def _block(x, mod, lw, layer, norm_g, b_gate, wpool, pscale, width, tm, fold, init, state_out, hp, unroll, pos=None):
    b, l, _ = x.shape
    folded = lambda a: a.reshape(b // fold, fold * l, a.shape[-1])
    unfolded = lambda a: a.reshape(b, l, a.shape[-1])
    uf, up, qkvo, grows, mg = _inproj(folded(x), mod, norm_g, lw["w_cat"], layer, b_gate, tm, min(MLSTM_CHUNK, l), pos)
    yf = _fourier(unfolded(uf))
    mouts = _mlstm(unfolded(qkvo), grows, init, state_out, hp, unroll)
    band, icnt = _pool_tables(width)
    x = _merge_ffn(folded(x), mod, norm_g, folded(yf), up, folded(mouts[0]), qkvo, mg,
                   jnp.asarray(band).astype(BF16), jnp.asarray(icnt), wpool, pscale, lw, layer, tm, pos)
    return unfolded(x), mouts[1:]


def kernel(x_prompt, x_sample, state_C, state_n, state_m, c, c_ctx, w_ada, b_ada, norm_g, w_in, b_gate, w_proj_f, w_pool, pool_scale, w_proj_p, w_proj_m, w_out, w_ffn_in, w_ffn_out):
    bp, lp, _ = x_prompt.shape
    bs, ls, _ = x_sample.shape
    cond = jnp.concatenate([c_ctx[None, :], c, jnp.zeros((COND_ROWS - 1 - bs, D_MODEL), F32)], axis=0)
    mod = _modulation(cond, w_ada, b_ada).reshape(DEPTH, COND_ROWS, 6, D_MODEL)
    lw = _stacked_weights(w_in, w_proj_f, w_proj_p, w_proj_m, w_out, w_ffn_in, w_ffn_out)
    group_eye = jnp.eye(len(POOL_WINDOWS), dtype=w_pool.dtype)[None, :, None, :, None]
    wpool = (w_pool[:, :, :, None, :] * group_eye).reshape(DEPTH, POOL_WIDTH, POOL_WIDTH).astype(BF16)
    layer_args = [(lw, i, norm_g[i], b_gate[i], wpool[i], pool_scale[i].reshape(1, POOL_WIDTH))
                  for i in range(DEPTH)]

    xp = x_prompt
    states = None
    for i in range(DEPTH):
        xp, states = _block(xp, mod[i, 0:1], *layer_args[i], lp, TOKEN_TILE, TOKEN_TILE // lp, None, (i, states),
                            HEADS, 1)
    new_c, new_n, new_m = states

    xs = x_sample
    pos = jnp.asarray(_pos_table(ls // GRID_W))
    n0 = state_n.reshape(bs, DEPTH, N_DIR, HEADS, 1, DH)
    m0 = jnp.broadcast_to(state_m[..., None, None], (bs, DEPTH, N_DIR, HEADS, 1, LANES))
    for i in range(DEPTH):
        init = (state_C, n0, m0, i)
        xs, _ = _block(xs, mod[i, 1:1 + bs], *layer_args[i], GRID_W, TOKEN_TILE, 1, init, None, HEADS, 1,
                       pos if i == 0 else None)

    return (xp, xs, new_c, new_n[:, :, :, :, 0, :], new_m[:, :, :, :, 0, 0])
```

```python
import functools

import numpy as np
import jax
import jax.numpy as jnp
from jax import lax
from jax.experimental import pallas as pl
from jax.experimental.pallas import tpu as pltpu

F32 = jnp.float32
BF16 = jnp.bfloat16

D_MODEL = 1024
DEPTH = 2
GRID_W = 64
FOURIER_GROUPS = 4
FOURIER_WIDTH = D_MODEL // 4
FOURIER_GROUP_DIM = FOURIER_WIDTH // FOURIER_GROUPS
POOL_WINDOWS = (2, 4, 8, 16)
POOL_WIDTH = D_MODEL // 4
POOL_GROUP_DIM = POOL_WIDTH // len(POOL_WINDOWS)
HEADS = 4
MLSTM_WIDTH = D_MODEL // 2
DH = MLSTM_WIDTH // HEADS
N_DIR = 2
N_GATE_COLS = N_DIR * 2 * HEADS
D_FF = -(-8 * D_MODEL // (3 * 256)) * 256
RMS_EPS = 1e-6
POS_BASE = 10000.0

OFF_F = 0
OFF_P = OFF_F + FOURIER_WIDTH
OFF_QKVO = OFF_P + POOL_WIDTH
OFF_G = OFF_QKVO + 4 * MLSTM_WIDTH
OFF_M = OFF_G + N_GATE_COLS
D_IN = OFF_M + 3 * D_MODEL

LANES = 128
MLSTM_CHUNK = 256
POOL_BLOCK = 256
TOKEN_TILE = 512
COND_ROWS = 16
VMEM_LIMIT = 60 * 1024 * 1024
MXU_COLS = 256
FFN_CHUNKS = ((0, 5 * MXU_COLS), (5 * MXU_COLS, D_FF))
CAT_M = OFF_G
CAT_G = CAT_M + 3 * D_MODEL
CAT_W = CAT_G + LANES


def _dot(a, b):
    return jnp.dot(a, b, preferred_element_type=F32)


def _rms(x, g):
    return x * lax.rsqrt(jnp.mean(x * x, axis=-1, keepdims=True) + RMS_EPS) * g


def _log_sigmoid(x):
    return jnp.minimum(x, 0.0) - jnp.log(1.0 + jnp.exp(-jnp.abs(x)))


def _sigmoid(x):
    return 0.5 * jnp.tanh(0.5 * x) + 0.5


@functools.lru_cache(maxsize=None)
def _dft_tables(n):
    idx = np.arange(n, dtype=np.int64)
    ang = 2.0 * np.pi * ((idx[:, None] * idx[None, :]) % n).astype(np.float64) / n
    return np.cos(ang).astype(np.float32), np.sin(ang).astype(np.float32)


@functools.lru_cache(maxsize=None)
def _dft_half_tables(n):
    f = np.arange(n // 2, dtype=np.int64)[:, None]
    j = np.arange(n // 2, dtype=np.int64)[None, :]
    out = []
    for l in (2 * j, 2 * j + 1):
        ang = 2.0 * np.pi * ((f * l) % n).astype(np.float64) / n
        out += [np.cos(ang).astype(np.float32), np.sin(ang).astype(np.float32)]
    return tuple(out)


@functools.lru_cache(maxsize=None)
def _group_dft_tables():
    c, s = _dft_tables(FOURIER_GROUP_DIM)
    bc = np.zeros((FOURIER_WIDTH, FOURIER_WIDTH), np.float32)
    bs = np.zeros((FOURIER_WIDTH, FOURIER_WIDTH), np.float32)
    for g in range(FOURIER_GROUPS):
        sl = slice(g * FOURIER_GROUP_DIM, (g + 1) * FOURIER_GROUP_DIM)
        bc[sl, sl] = c
        bs[sl, sl] = s
    return bc, bs


@functools.lru_cache(maxsize=None)
def _pool_tables(width):
    band = np.zeros((len(POOL_WINDOWS), POOL_BLOCK, POOL_BLOCK), np.float32)
    inv_cnt = np.zeros((POOL_BLOCK, POOL_WIDTH), np.float32)
    for g, w in enumerate(POOL_WINDOWS):
        left = w // 2
        right = w - 1 - left
        for t in range(POOL_BLOCK):
            row, pos = divmod(t, width)
            lo = min(max(pos - left, 0), width - 1)
            hi = min(max(pos + right, 0), width - 1)
            band[g, t, row * width + lo:row * width + hi + 1] = 1.0
            inv_cnt[t, g * POOL_GROUP_DIM:(g + 1) * POOL_GROUP_DIM] = 1.0 / (hi - lo + 1)
    return band, inv_cnt


@functools.lru_cache(maxsize=None)
def _pos_table(rows):
    quarter = D_MODEL // 4
    omega = 1.0 / (POS_BASE ** (np.arange(quarter, dtype=np.float64) / quarter))
    t = np.arange(rows * GRID_W)
    r = (t // GRID_W).astype(np.float64)
    col = (t % GRID_W).astype(np.float64)
    ar = r[:, None] * omega[None, :]
    ac = col[:, None] * omega[None, :]
    return np.concatenate([np.sin(ar), np.cos(ar), np.sin(ac), np.cos(ac)], axis=-1).astype(np.float32)


def _mod_kernel(c_ref, w_ref, b_ref, o_ref):
    @pl.when(pl.program_id(1) == 0)
    def _():
        o_ref[...] = jnp.broadcast_to(b_ref[...], o_ref.shape)

    c = c_ref[...]
    a = (c * jax.nn.sigmoid(c)).astype(BF16)
    o_ref[...] += _dot(a, w_ref[...].astype(BF16))


def _modulation(cond, w_ada, b_ada):
    tk = 256
    n = 6 * D_MODEL
    return pl.pallas_call(
        _mod_kernel,
        name="adaln_mod",
        grid=(DEPTH, D_MODEL // tk),
        in_specs=[
            pl.BlockSpec((COND_ROWS, tk), lambda l, k: (0, k)),
            pl.BlockSpec((None, tk, n), lambda l, k: (l, k, 0)),
            pl.BlockSpec((None, 1, n), lambda l, k: (l, 0, 0)),
        ],
        out_specs=pl.BlockSpec((None, COND_ROWS, n), lambda l, k: (l, 0, 0)),
        out_shape=jax.ShapeDtypeStruct((DEPTH, COND_ROWS, 6 * D_MODEL), F32),
        compiler_params=pltpu.CompilerParams(
            dimension_semantics=("arbitrary", "arbitrary"), vmem_limit_bytes=VMEM_LIMIT),
    )(cond, w_ada, b_ada.reshape(DEPTH, 1, 6 * D_MODEL))


def _gate_rows(g, bias_col, rows_ref, chunk):
    x = g.T[:N_GATE_COLS, :] + bias_col
    row = lax.broadcasted_iota(jnp.int32, x.shape, 0)
    fwd = row < 2 * HEADS
    lf = _log_sigmoid(x)
    b_f = jnp.where(fwd, _scan_lanes(lf, False, jnp.add, 0.0), _scan_lanes(lf, True, jnp.add, 0.0))
    b_i = pltpu.roll(b_f, N_GATE_COLS - HEADS, 0)
    a = x - b_i
    p = jnp.where(fwd, _scan_lanes(a, False, jnp.maximum, -jnp.inf), _scan_lanes(a, True, jnp.maximum, -jnp.inf))
    sl = slice(chunk * N_GATE_COLS, (chunk + 1) * N_GATE_COLS)
    rows_ref[0, sl, :] = a
    rows_ref[1, sl, :] = b_i
    rows_ref[2, sl, :] = b_i + p
    rows_ref[3, sl, :] = p


def _inproj_kernel(*refs, add_pos, chunk):
    it = iter(refs)
    x_ref = next(it)
    pos_ref = next(it) if add_pos else None
    mod_ref, ng_ref, w_ref, bg_ref = (next(it) for _ in range(4))
    uf_ref, up_ref, qkvo_ref, rows_ref, mg_ref = (next(it) for _ in range(5))

    x = x_ref[...]
    if add_pos:
        x = x + pos_ref[...]
    h = (_rms(x, ng_ref[0:1, :] * (1.0 + mod_ref[1:2, :])) + mod_ref[0:1, :]).astype(BF16)
    ck = 512
    za = _dot(h, w_ref[:, 0:ck])
    uf_ref[...] = za[:, :FOURIER_WIDTH].astype(BF16)
    up_ref[...] = za[:, FOURIER_WIDTH:].astype(BF16)
    for j in range(4 * MLSTM_WIDTH // ck):
        z = _dot(h, w_ref[:, (j + 1) * ck:(j + 2) * ck])
        if j * ck >= 3 * MLSTM_WIDTH:
            z = _sigmoid(z)
        qkvo_ref[:, j * ck:(j + 1) * ck] = z.astype(BF16)
    g = _dot(h, w_ref[:, CAT_G:CAT_W])
    for c in range(g.shape[0] // chunk):
        _gate_rows(g[c * chunk:(c + 1) * chunk, :], bg_ref[...], rows_ref, c)
    for j in range(3 * D_MODEL // ck):
        mg_ref[:, j * ck:(j + 1) * ck] = _sigmoid(_dot(h, w_ref[:, CAT_M + j * ck:CAT_M + (j + 1) * ck])).astype(BF16)


def _layer_spec(a, layer):
    return pl.BlockSpec((None,) + a.shape[1:], lambda *_: (layer,) + (0,) * (a.ndim - 1),
                        pipeline_mode=pl.Buffered(1))


def _inproj(x, mod, norm_g, w_cat, layer, b_gate, tm, t, pos=None):
    b, l, _ = x.shape
    add_pos = pos is not None
    assert tm % t == 0 and l % tm == 0
    bias_col = b_gate.reshape(N_GATE_COLS, 1)
    per_batch_mod = mod.shape[0] > 1
    tok = lambda w: pl.BlockSpec((None, tm, w), lambda i, j: (i, j, 0))
    full = lambda a: pl.BlockSpec(a.shape, lambda i, j: (0,) * a.ndim)
    in_specs = [tok(D_MODEL)]
    args = [x]
    if add_pos:
        in_specs.append(pl.BlockSpec((tm, D_MODEL), lambda i, j: (j, 0)))
        args.append(pos)
    in_specs += [
        pl.BlockSpec((None, 6, D_MODEL), (lambda i, j: (i, 0, 0)) if per_batch_mod else (lambda i, j: (0, 0, 0))),
        full(norm_g), _layer_spec(w_cat, layer), full(bias_col)]
    args += [mod, norm_g, w_cat, bias_col]
    rows_per_tile = tm // t * N_GATE_COLS
    out_specs = [tok(FOURIER_WIDTH), tok(POOL_WIDTH), tok(4 * MLSTM_WIDTH),
                 pl.BlockSpec((None, 4, rows_per_tile, t), lambda i, j: (i, 0, j, 0)), tok(3 * D_MODEL)]
    out_shape = [
        jax.ShapeDtypeStruct((b, l, FOURIER_WIDTH), BF16),
        jax.ShapeDtypeStruct((b, l, POOL_WIDTH), BF16),
        jax.ShapeDtypeStruct((b, l, 4 * MLSTM_WIDTH), BF16),
        jax.ShapeDtypeStruct((b, 4, l // t * N_GATE_COLS, t), F32),
        jax.ShapeDtypeStruct((b, l, 3 * D_MODEL), BF16)]
    return pl.pallas_call(
        functools.partial(_inproj_kernel, add_pos=add_pos, chunk=t),
        name="inproj",
        grid=(b, l // tm),
        in_specs=in_specs, out_specs=out_specs, out_shape=out_shape,
        compiler_params=pltpu.CompilerParams(
            dimension_semantics=("arbitrary", "arbitrary"), vmem_limit_bytes=VMEM_LIMIT),
    )(*args)


def _fourier_kernel(u_ref, ce_ref, se_ref, co_ref, so_ref, bc_ref, bs_ref, o_ref, vc_ref, vs_ref, *, scale, half):
    u = u_ref[...]
    nblk = u.shape[1] // LANES
    for ref, tab in ((vc_ref, bc_ref), (vs_ref, bs_ref)):
        v = _dot(u, tab[...])
        for k in range(nblk):
            ref[k] = v[:, k * LANES:(k + 1) * LANES]

    def rows(ref, start):
        return jnp.concatenate([ref[k, pl.ds(start, half, stride=2), :] for k in range(nblk)], axis=1).astype(BF16)

    e = _dot(ce_ref[...], rows(vc_ref, 0)) - _dot(se_ref[...], rows(vs_ref, 0))
    o = _dot(co_ref[...], rows(vc_ref, 1)) - _dot(so_ref[...], rows(vs_ref, 1))
    o_ref[0:half, :] = ((e + o) * scale).astype(BF16)
    o_ref[half:, :] = ((e - o) * scale).astype(BF16)


def _fourier(uf):
    b, l, w = uf.shape
    half = l // 2
    bc, bs = (jnp.asarray(t).astype(BF16) for t in _group_dft_tables())
    tables = [jnp.asarray(t).astype(BF16) for t in _dft_half_tables(l)]
    full = lambda a: pl.BlockSpec(a.shape, lambda i: (0,) * a.ndim)
    seq = pl.BlockSpec((None, l, w), lambda i: (i, 0, 0))
    return pl.pallas_call(
        functools.partial(_fourier_kernel, scale=float((l * FOURIER_GROUP_DIM) ** -0.5), half=half),
        name="fourier",
        grid=(b,),
        in_specs=[seq] + [full(t) for t in tables] + [full(bc), full(bs)],
        out_specs=seq,
        out_shape=jax.ShapeDtypeStruct((b, l, w), BF16),
        scratch_shapes=[pltpu.VMEM((w // LANES, l, LANES), F32), pltpu.VMEM((w // LANES, l, LANES), F32)],
        compiler_params=pltpu.CompilerParams(
            dimension_semantics=("arbitrary",), vmem_limit_bytes=VMEM_LIMIT),
    )(uf, *tables, bc, bs)


def _scan_lanes(x, reverse, op, fill):
    n = x.shape[-1]
    lane = lax.broadcasted_iota(jnp.int32, x.shape, x.ndim - 1)
    s = 1
    while s < n:
        if reverse:
            x = op(x, jnp.where(lane < n - s, pltpu.roll(x, n - s, x.ndim - 1), fill))
        else:
            x = op(x, jnp.where(lane >= s, pltpu.roll(x, s, x.ndim - 1), fill))
        s *= 2
    return x


SPLIT_ROWS = 16


def _mlstm_kernel(*refs, seq, chunk, hp, seq_group, has_init, emit_state, n_alias, unroll):
    it = iter(refs)
    q_ref, k_ref, v_ref, rows_ref = (next(it) for _ in range(4))
    c0_ref, n0_ref, m0_ref = (next(it) for _ in range(3)) if has_init else (None, None, None)
    if n_alias:
        [next(it) for _ in range(n_alias)]
    hm_ref = next(it)
    cs_ref, ns_ref, ms_ref = (next(it) for _ in range(3)) if emit_state else (None, None, None)
    hf_ref, hb_ref = (next(it) for _ in range(2))

    t = chunk
    nc = seq // t
    rep = t // DH
    scale = DH ** -0.5

    row_i = lax.broadcasted_iota(jnp.int32, (t, t), 0)
    col_i = lax.broadcasted_iota(jnp.int32, (t, t), 1)
    eye_dh = jnp.where(lax.broadcasted_iota(jnp.int32, (DH, DH), 0) == lax.broadcasted_iota(jnp.int32, (DH, DH), 1),
                       1.0, 0.0).astype(BF16)
    sub = lax.broadcasted_iota(jnp.int32, (SPLIT_ROWS, t), 0)
    part = sub % 3
    p_sub = lax.broadcasted_iota(jnp.int32, (SPLIT_ROWS, 2 * DH), 0)
    p_lane = lax.broadcasted_iota(jnp.int32, (SPLIT_ROWS, 2 * DH), 1)
    gather_mat = jnp.where(((p_sub < 3) & (p_lane < DH)) | ((p_sub >= 3) & (p_sub < 6) & (p_lane >= DH)),
                           1.0, 0.0).astype(BF16)
    ones_cols = jnp.ones((t, DH), BF16)

    def step(c, hh, direction, Cn, m, h_ref):
        rows = pl.ds(pl.multiple_of(c * t, t), t)
        cols = slice(hh * DH, (hh + 1) * DH)
        chunk_row = (pl.program_id(0) % seq_group) * nc + c
        r = pl.ds(chunk_row * N_GATE_COLS + direction * 2 * HEADS + pl.program_id(1) * hp + hh, 1)
        a_r = rows_ref[0, r, :]
        b_r = rows_ref[1, r, :]
        g_r = rows_ref[2, r, :]
        p_r = rows_ref[3, r, :]
        if direction == 0:
            end = t - 1
            mask = col_i <= row_i
        else:
            end = 0
            mask = col_i >= row_i
        b_end = b_r[:, end:end + 1]
        p_end = p_r[:, end:end + 1]

        x0 = jnp.where(sub < 3, b_r, jnp.where(sub < 6, g_r, 0.0))
        x1 = x0 - x0.astype(BF16).astype(F32)
        x2 = x1 - x1.astype(BF16).astype(F32)
        xs = jnp.where(part == 0, x0, jnp.where(part == 1, x1, x2)).astype(BF16)
        bg = lax.dot_general(xs, gather_mat, (((0,), (0,)), ((), ())), preferred_element_type=F32)
        b_c = bg[:, :DH]
        m_t = jnp.maximum(b_c + m, bg[:, DH:])
        mu = m_t - b_c
        sdec = jnp.exp(m - mu)

        qc = q_ref[rows, cols]
        kc = k_ref[rows, cols]
        v1 = jnp.concatenate([v_ref[rows, cols], ones_cols], axis=1)
        mu_t = jnp.concatenate([mu] * rep, axis=1)
        d = jnp.where(mask, jnp.exp((a_r + np.float32(np.log(scale))) - mu_t), 0.0)
        s = lax.dot_general(qc, kc, (((1,), (1,)), ((), ())), preferred_element_type=F32) * d
        qs = _dot(qc, Cn.astype(BF16))
        sv = _dot(s.astype(BF16), v1)
        num = sdec * qs[:, :DH] + sv[:, :DH]
        den = sdec * qs[:, DH:] + sv[:, DH:]
        h_ref[rows, cols] = num * (1.0 / jnp.maximum(jnp.abs(den), jnp.exp(-m_t)))

        mx = jnp.maximum(m, p_end)
        w_r = jnp.exp(a_r - mx) * scale
        kt = lax.dot_general(eye_dh, kc, (((1,), (1,)), ((), ())), preferred_element_type=F32)
        Cn_new = jnp.exp(m - mx) * Cn + _dot((kt * w_r).astype(BF16), v1)
        return Cn_new, b_end + mx

    def body(ci, carry):
        out = []
        for hh in range(hp):
            Cf, mf, Cb, mb = carry[4 * hh:4 * hh + 4]
            Cf, mf = step(ci, hh, 0, Cf, mf, hf_ref)
            Cb, mb = step(nc - 1 - ci, hh, 1, Cb, mb, hb_ref)
            out += [Cf, mf, Cb, mb]
        return tuple(out)

    init = []
    for hh in range(hp):
        for d in range(N_DIR):
            if has_init:
                n_col = jnp.sum(eye_dh.astype(F32) * n0_ref[d, hh], axis=1, keepdims=True)
                init += [jnp.concatenate([c0_ref[d, hh], jnp.broadcast_to(n_col, (DH, DH))], axis=1),
                         m0_ref[d, hh][:, 0:1]]
            else:
                init += [jnp.zeros((DH, 2 * DH), F32), jnp.zeros((1, 1), F32)]
    final = lax.fori_loop(0, nc, body, tuple(init), unroll=unroll)

    hm_ref[...] = (hf_ref[...] + hb_ref[...]).astype(BF16)
    if emit_state:
        eye_f = eye_dh.astype(F32)
        for hh in range(hp):
            for d in range(N_DIR):
                Cn, m = final[2 * (hh * N_DIR + d):2 * (hh * N_DIR + d) + 2]
                cs_ref[d, hh] = Cn[:, :DH]
                ns_ref[d, hh] = jnp.sum(eye_f * Cn[:, DH:], axis=0, keepdims=True)
                ms_ref[d, hh] = jnp.broadcast_to(m, (1, LANES))


def _mlstm(qkvo, grows, init, state_out, hp, unroll):
    b, l, _ = qkvo.shape
    emit_state = state_out is not None
    t = grows.shape[-1]
    nc = l // t
    seq_group = b // grows.shape[0]
    assert l % t == 0 and t % DH == 0 and HEADS % hp == 0 and nc % unroll == 0
    assert grows.shape[2] == seq_group * nc * N_GATE_COLS
    has_init = init is not None
    groups = HEADS // hp
    head_cols = lambda k: pl.BlockSpec((None, l, hp * DH), lambda i, h: (i, 0, k * groups + h))
    in_specs = [
        head_cols(0), head_cols(1), head_cols(2),
        pl.BlockSpec((None,) + grows.shape[1:], lambda i, h: (i // seq_group, 0, 0, 0))]
    args = [qkvo, qkvo, qkvo, grows]
    if has_init:
        c0, n0, m0, layer = init
        in_specs += [
            pl.BlockSpec((None, None, N_DIR, hp, DH, DH), lambda i, h: (i, layer, 0, h, 0, 0)),
            pl.BlockSpec((None, None, N_DIR, hp, 1, DH), lambda i, h: (i, layer, 0, h, 0, 0)),
            pl.BlockSpec((None, None, N_DIR, hp, 1, LANES), lambda i, h: (i, layer, 0, h, 0, 0))]
        args += [c0, n0, m0]
    out_specs = [pl.BlockSpec((None, l, hp * DH), lambda i, h: (i, 0, h))]
    out_shape = [jax.ShapeDtypeStruct((b, l, MLSTM_WIDTH), BF16)]
    aliases = {}
    if emit_state:
        out_layer, prev = state_out
        slab = lambda w: pl.BlockSpec((None, None, N_DIR, hp, w, DH), lambda i, h: (i, out_layer, 0, h, 0, 0))
        out_specs += [slab(DH), slab(1), slab(1)]
        out_shape += [jax.ShapeDtypeStruct((b, DEPTH, N_DIR, HEADS, DH, DH), F32),
                      jax.ShapeDtypeStruct((b, DEPTH, N_DIR, HEADS, 1, DH), F32),
                      jax.ShapeDtypeStruct((b, DEPTH, N_DIR, HEADS, 1, DH), F32)]
        if prev is not None:
            aliases = {len(args) + k: 1 + k for k in range(len(prev))}
            in_specs += [pl.BlockSpec(memory_space=pl.ANY)] * len(prev)
            args += list(prev)
    return pl.pallas_call(
        functools.partial(_mlstm_kernel, seq=l, chunk=t, hp=hp, seq_group=seq_group, has_init=has_init,
                          emit_state=emit_state, n_alias=len(aliases), unroll=unroll),
        name="mlstm",
        grid=(b, groups),
        in_specs=in_specs, out_specs=out_specs, out_shape=out_shape,
        input_output_aliases=aliases,
        scratch_shapes=[
            pltpu.VMEM((l, hp * DH), F32),
            pltpu.VMEM((l, hp * DH), F32)],
        compiler_params=pltpu.CompilerParams(
            dimension_semantics=("arbitrary", "arbitrary"), vmem_limit_bytes=VMEM_LIMIT),
    )(*args)


def _merge_ffn_kernel(*refs, tm, add_pos):
    it = iter(refs)
    x_ref = next(it)
    pos_ref = next(it) if add_pos else None
    (mod_ref, ng_ref, yf_ref, up_ref, hm_ref, og_ref, mg_ref, band_ref, icnt_ref, wpool_ref, pscale_ref,
     wpf_ref, wpp_ref, wpm_ref, wout_ref, wfi_ref, wfo_ref, o_ref) = (next(it) for _ in range(18))
    x = x_ref[...]
    if add_pos:
        x = x + pos_ref[...]

    blocks = [slice(r * POOL_BLOCK, (r + 1) * POOL_BLOCK) for r in range(tm // POOL_BLOCK)]

    lane_group = lax.broadcasted_iota(jnp.int32, (POOL_BLOCK, POOL_WIDTH), 1) // POOL_GROUP_DIM
    pooled = []
    for rows in blocks:
        u = up_ref[rows, :]
        acc = jnp.zeros((POOL_BLOCK, POOL_WIDTH), F32)
        for g in range(len(POOL_WINDOWS)):
            acc = jnp.where(lane_group == g, _dot(band_ref[g], u), acc)
        pooled.append((acc * icnt_ref[...] - u.astype(F32)).astype(BF16))

    merged = []
    for rows, p in zip(blocks, pooled):
        pp_pre = _dot(p, wpool_ref[...])
        y_f = _dot(yf_ref[rows, :], wpf_ref[...])
        hg = (og_ref[rows, :].astype(F32) * hm_ref[rows, :].astype(F32)).astype(BF16)
        y_m = _dot(hg, wpm_ref[...])
        y_p = _dot((pp_pre * pscale_ref[...]).astype(BF16), wpp_ref[...])
        g_f = mg_ref[rows, 0:D_MODEL].astype(F32)
        g_p = mg_ref[rows, D_MODEL:2 * D_MODEL].astype(F32)
        g_m = mg_ref[rows, 2 * D_MODEL:3 * D_MODEL].astype(F32)
        merged.append((g_f * y_f + g_p * y_p + g_m * y_m).astype(BF16))

    x1_blocks, h2_blocks = [], []
    for rows, y in zip(blocks, merged):
        x1b = x[rows, :] + _rms(_dot(y, wout_ref[...]), ng_ref[1:2, :] * mod_ref[2:3, :])
        x1_blocks.append(x1b)
        h2_blocks.append((_rms(x1b, ng_ref[2:3, :] * (1.0 + mod_ref[4:5, :])) + mod_ref[3:4, :]).astype(BF16))
    h2 = h2_blocks[0] if len(blocks) == 1 else jnp.concatenate(h2_blocks, axis=0)
    acc = jnp.zeros((tm, D_MODEL), F32)
    for k, (c0, c1) in enumerate(FFN_CHUNKS):
        if k == 0:
            ab = [(_dot(hb, wfi_ref[:, c0:c1]), _dot(hb, wfi_ref[:, D_FF + c0:D_FF + c1])) for hb in h2_blocks]
            a = jnp.concatenate([t[0] for t in ab], axis=0)
            bb = jnp.concatenate([t[1] for t in ab], axis=0)
        else:
            a = _dot(h2, wfi_ref[:, c0:c1])
            bb = _dot(h2, wfi_ref[:, D_FF + c0:D_FF + c1])
        act = (a * _sigmoid(a) * bb).astype(BF16)
        if k < len(FFN_CHUNKS) - 1:
            acc = acc + _dot(act, wfo_ref[c0:c1, :])
        else:
            for rows, x1b in zip(blocks, x1_blocks):
                yb = acc[rows, :] + _dot(act[rows, :], wfo_ref[c0:c1, :])
                o_ref[rows, :] = x1b + _rms(yb, ng_ref[3:4, :] * mod_ref[5:6, :])


def _merge_ffn(x, mod, norm_g, yf, up, hm, qkvo, mg, band, icnt, wpool, pscale, lw, layer, tm, pos=None):
    b, l, _ = x.shape
    add_pos = pos is not None
    ogate = pl.BlockSpec((None, tm, MLSTM_WIDTH), lambda i, j: (i, j, 3))
    per_batch_mod = mod.shape[0] > 1
    tok = lambda w: pl.BlockSpec((None, tm, w), lambda i, j: (i, j, 0))
    full = lambda a: pl.BlockSpec(a.shape, lambda i, j: (0,) * a.ndim)
    stacked = [lw[k] for k in ("wpf", "wpp", "wpm", "wout", "wfi", "wfo")]
    in_specs = [tok(D_MODEL)]
    args = [x]
    if add_pos:
        in_specs.append(pl.BlockSpec((tm, D_MODEL), lambda i, j: (j, 0)))
        args.append(pos)
    in_specs += [
        pl.BlockSpec((None, 6, D_MODEL), (lambda i, j: (i, 0, 0)) if per_batch_mod else (lambda i, j: (0, 0, 0))),
        full(norm_g), tok(FOURIER_WIDTH), tok(POOL_WIDTH), tok(MLSTM_WIDTH), ogate, tok(3 * D_MODEL),
        full(band), full(icnt), full(wpool), full(pscale)]
    in_specs += [_layer_spec(a, layer) for a in stacked]
    args += [mod, norm_g, yf, up, hm, qkvo, mg, band, icnt, wpool, pscale] + stacked
    return pl.pallas_call(
        functools.partial(_merge_ffn_kernel, tm=tm, add_pos=add_pos),
        name="merge_ffn",
        grid=(b, l // tm),
        in_specs=in_specs,
        out_specs=tok(D_MODEL),
        out_shape=jax.ShapeDtypeStruct((b, l, D_MODEL), F32),
        compiler_params=pltpu.CompilerParams(
            dimension_semantics=("arbitrary", "arbitrary"), vmem_limit_bytes=VMEM_LIMIT),
    )(*args)


def _repack_kernel(w_ref, o_ref):
    o_ref[:, 0:CAT_M] = w_ref[:, 0:OFF_G].astype(BF16)
    o_ref[:, CAT_M:CAT_G] = w_ref[:, OFF_M:D_IN].astype(BF16)
    tail = w_ref[:, OFF_G:OFF_G + LANES]
    lane = lax.broadcasted_iota(jnp.int32, tail.shape, 1)
    o_ref[:, CAT_G:CAT_W] = jnp.where(lane < N_GATE_COLS, tail, 0.0).astype(BF16)


def _repack_w_in(w_in):
    tk = 128
    return pl.pallas_call(
        _repack_kernel,
        name="repack_w_in",
        grid=(DEPTH, D_MODEL // tk),
        in_specs=[pl.BlockSpec((None, tk, D_IN), lambda l, k: (l, k, 0))],
        out_specs=pl.BlockSpec((None, tk, CAT_W), lambda l, k: (l, k, 0)),
        out_shape=jax.ShapeDtypeStruct((DEPTH, D_MODEL, CAT_W), BF16),
        compiler_params=pltpu.CompilerParams(
            dimension_semantics=("arbitrary", "arbitrary"), vmem_limit_bytes=VMEM_LIMIT),
    )(w_in)


def _stacked_weights(w_in, w_proj_f, w_proj_p, w_proj_m, w_out, w_ffn_in, w_ffn_out):
    return dict(
        w_cat=_repack_w_in(w_in),
        wpf=w_proj_f.astype(BF16), wpp=w_proj_p.astype(BF16), wpm=w_proj_m.astype(BF16),
        wout=w_out.astype(BF16), wfi=w_ffn_in.astype(BF16), wfo=w_ffn_out.astype(BF16))


def _block(x, mod, lw, layer, norm_g, b_gate, wpool, pscale, width, tm, fold, init, state_out, hp, unroll, pos=None):
    b, l, _ = x.shape
    folded = lambda a: a.reshape(b // fold, fold * l, a.shape[-1])
    unfolded = lambda a: a.reshape(b, l, a.shape[-1])
    uf, up, qkvo, grows, mg = _inproj(folded(x), mod, norm_g, lw["w_cat"], layer, b_gate, tm, min(MLSTM_CHUNK, l), pos)
    yf = _fourier(unfolded(uf))
    mouts = _mlstm(unfolded(qkvo), grows, init, state_out, hp, unroll)
    band, icnt = _pool_tables(width)
    x = _merge_ffn(folded(x), mod, norm_g, folded(yf), up, folded(mouts[0]), qkvo, mg,
                   jnp.asarray(band).astype(BF16), jnp.asarray(icnt), wpool, pscale, lw, layer, tm, pos)
    return unfolded(x), mouts[1:]


def kernel(x_prompt, x_sample, state_C, state_n, state_m, c, c_ctx, w_ada, b_ada, norm_g, w_in, b_gate, w_proj_f, w_pool, pool_scale, w_proj_p, w_proj_m, w_out, w_ffn_in, w_ffn_out):
    bp, lp, _ = x_prompt.shape
    bs, ls, _ = x_sample.shape
    cond = jnp.concatenate([c_ctx[None, :], c, jnp.zeros((COND_ROWS - 1 - bs, D_MODEL), F32)], axis=0)
    mod = _modulation(cond, w_ada, b_ada).reshape(DEPTH, COND_ROWS, 6, D_MODEL)
    lw = _stacked_weights(w_in, w_proj_f, w_proj_p, w_proj_m, w_out, w_ffn_in, w_ffn_out)
    group_eye = jnp.eye(len(POOL_WINDOWS), dtype=w_pool.dtype)[None, :, None, :, None]
    wpool = (w_pool[:, :, :, None, :] * group_eye).reshape(DEPTH, POOL_WIDTH, POOL_WIDTH).astype(BF16)
    layer_args = [(lw, i, norm_g[i], b_gate[i], wpool[i], pool_scale[i].reshape(1, POOL_WIDTH))
                  for i in range(DEPTH)]

    xp = x_prompt
    states = None
    for i in range(DEPTH):
        xp, states = _block(xp, mod[i, 0:1], *layer_args[i], lp, TOKEN_TILE, TOKEN_TILE // lp, None, (i, states),
                            HEADS, 1)
    new_c, new_n, new_m = states

    xs = x_sample
    pos = jnp.asarray(_pos_table(ls // GRID_W))
    n0 = state_n.reshape(bs, DEPTH, N_DIR, HEADS, 1, DH)
    m0 = jnp.broadcast_to(state_m[..., None, None], (bs, DEPTH, N_DIR, HEADS, 1, LANES))
    for i in range(DEPTH):
        init = (state_C, n0, m0, i)
        xs, _ = _block(xs, mod[i, 1:1 + bs], *layer_args[i], GRID_W, TOKEN_TILE, 1, init, None, HEADS, 1,
                       pos if i == 0 else None)

    return (xp, xs, new_c, new_n[:, :, :, :, 0, :], new_m[:, :, :, :, 0, 0])
```

```python
import functools

import numpy as np
import jax
import jax.numpy as jnp
from jax import lax
from jax.experimental import pallas as pl
from jax.experimental.pallas import tpu as pltpu

F32 = jnp.float32
BF16 = jnp.bfloat16

D_MODEL = 1024
DEPTH = 2
GRID_W = 64
FOURIER_GROUPS = 4
FOURIER_WIDTH = D_MODEL // 4
FOURIER_GROUP_DIM = FOURIER_WIDTH // FOURIER_GROUPS
POOL_WINDOWS = (2, 4, 8, 16)
POOL_WIDTH = D_MODEL // 4
POOL_GROUP_DIM = POOL_WIDTH // len(POOL_WINDOWS)
HEADS = 4
MLSTM_WIDTH = D_MODEL // 2
DH = MLSTM_WIDTH // HEADS
N_DIR = 2
N_GATE_COLS = N_DIR * 2 * HEADS
D_FF = -(-8 * D_MODEL // (3 * 256)) * 256
RMS_EPS = 1e-6
POS_BASE = 10000.0

OFF_F = 0
OFF_P = OFF_F + FOURIER_WIDTH
OFF_QKVO = OFF_P + POOL_WIDTH
OFF_G = OFF_QKVO + 4 * MLSTM_WIDTH
OFF_M = OFF_G + N_GATE_COLS
D_IN = OFF_M + 3 * D_MODEL

LANES = 128
MLSTM_CHUNK = 256
POOL_BLOCK = 256
TOKEN_TILE = 512
COND_ROWS = 16
VMEM_LIMIT = 60 * 1024 * 1024
MXU_COLS = 256
FFN_CHUNKS = ((0, 5 * MXU_COLS), (5 * MXU_COLS, D_FF))
CAT_M = OFF_G
CAT_G = CAT_M + 3 * D_MODEL
CAT_W = CAT_G + LANES


def _dot(a, b):
    return jnp.dot(a, b, preferred_element_type=F32)


def _rms(x, g):
    return x * lax.rsqrt(jnp.mean(x * x, axis=-1, keepdims=True) + RMS_EPS) * g


def _log_sigmoid(x):
    return jnp.minimum(x, 0.0) - jnp.log(1.0 + jnp.exp(-jnp.abs(x)))


def _sigmoid(x):
    return 0.5 * jnp.tanh(0.5 * x) + 0.5


@functools.lru_cache(maxsize=None)
def _dft_tables(n):
    idx = np.arange(n, dtype=np.int64)
    ang = 2.0 * np.pi * ((idx[:, None] * idx[None, :]) % n).astype(np.float64) / n
    return np.cos(ang).astype(np.float32), np.sin(ang).astype(np.float32)


@functools.lru_cache(maxsize=None)
def _dft_half_tables(n):
    f = np.arange(n // 2, dtype=np.int64)[:, None]
    j = np.arange(n // 2, dtype=np.int64)[None, :]
    out = []
    for l in (2 * j, 2 * j + 1):
        ang = 2.0 * np.pi * ((f * l) % n).astype(np.float64) / n
        out += [np.cos(ang).astype(np.float32), np.sin(ang).astype(np.float32)]
    return tuple(out)


@functools.lru_cache(maxsize=None)
def _group_dft_tables():
    c, s = _dft_tables(FOURIER_GROUP_DIM)
    bc = np.zeros((FOURIER_WIDTH, FOURIER_WIDTH), np.float32)
    bs = np.zeros((FOURIER_WIDTH, FOURIER_WIDTH), np.float32)
    for g in range(FOURIER_GROUPS):
        sl = slice(g * FOURIER_GROUP_DIM, (g + 1) * FOURIER_GROUP_DIM)
        bc[sl, sl] = c
        bs[sl, sl] = s
    return bc, bs


@functools.lru_cache(maxsize=None)
def _pool_tables(width):
    band = np.zeros((len(POOL_WINDOWS), POOL_BLOCK, POOL_BLOCK), np.float32)
    inv_cnt = np.zeros((POOL_BLOCK, POOL_WIDTH), np.float32)
    for g, w in enumerate(POOL_WINDOWS):
        left = w // 2
        right = w - 1 - left
        for t in range(POOL_BLOCK):
            row, pos = divmod(t, width)
            lo = min(max(pos - left, 0), width - 1)
            hi = min(max(pos + right, 0), width - 1)
            band[g, t, row * width + lo:row * width + hi + 1] = 1.0
            inv_cnt[t, g * POOL_GROUP_DIM:(g + 1) * POOL_GROUP_DIM] = 1.0 / (hi - lo + 1)
    return band, inv_cnt


@functools.lru_cache(maxsize=None)
def _pos_table(rows):
    quarter = D_MODEL // 4
    omega = 1.0 / (POS_BASE ** (np.arange(quarter, dtype=np.float64) / quarter))
    t = np.arange(rows * GRID_W)
    r = (t // GRID_W).astype(np.float64)
    col = (t % GRID_W).astype(np.float64)
    ar = r[:, None] * omega[None, :]
    ac = col[:, None] * omega[None, :]
    return np.concatenate([np.sin(ar), np.cos(ar), np.sin(ac), np.cos(ac)], axis=-1).astype(np.float32)


def _mod_kernel(c_ref, w_ref, b_ref, o_ref):
    @pl.when(pl.program_id(1) == 0)
    def _():
        o_ref[...] = jnp.broadcast_to(b_ref[...], o_ref.shape)

    c = c_ref[...]
    a = (c * jax.nn.sigmoid(c)).astype(BF16)
    o_ref[...] += _dot(a, w_ref[...].astype(BF16))


def _modulation(cond, w_ada, b_ada):
    tk = 256
    n = 6 * D_MODEL
    return pl.pallas_call(
        _mod_kernel,
        name="adaln_mod",
        grid=(DEPTH, D_MODEL // tk),
        in_specs=[
            pl.BlockSpec((COND_ROWS, tk), lambda l, k: (0, k)),
            pl.BlockSpec((None, tk, n), lambda l, k: (l, k, 0)),
            pl.BlockSpec((None, 1, n), lambda l, k: (l, 0, 0)),
        ],
        out_specs=pl.BlockSpec((None, COND_ROWS, n), lambda l, k: (l, 0, 0)),
        out_shape=jax.ShapeDtypeStruct((DEPTH, COND_ROWS, 6 * D_MODEL), F32),
        compiler_params=pltpu.CompilerParams(
            dimension_semantics=("arbitrary", "arbitrary"), vmem_limit_bytes=VMEM_LIMIT),
    )(cond, w_ada, b_ada.reshape(DEPTH, 1, 6 * D_MODEL))


def _gate_rows(g, bias_col, rows_ref, chunk):
    x = g.T[:N_GATE_COLS, :] + bias_col
    row = lax.broadcasted_iota(jnp.int32, x.shape, 0)
    fwd = row < 2 * HEADS
    lf = _log_sigmoid(x)
    b_f = jnp.where(fwd, _scan_lanes(lf, False, jnp.add, 0.0), _scan_lanes(lf, True, jnp.add, 0.0))
    b_i = pltpu.roll(b_f, N_GATE_COLS - HEADS, 0)
    a = x - b_i
    p = jnp.where(fwd, _scan_lanes(a, False, jnp.maximum, -jnp.inf), _scan_lanes(a, True, jnp.maximum, -jnp.inf))
    sl = slice(chunk * N_GATE_COLS, (chunk + 1) * N_GATE_COLS)
    rows_ref[0, sl, :] = a
    rows_ref[1, sl, :] = b_i
    rows_ref[2, sl, :] = b_i + p
    rows_ref[3, sl, :] = p


def _inproj_kernel(*refs, add_pos, chunk):
    it = iter(refs)
    x_ref = next(it)
    pos_ref = next(it) if add_pos else None
    mod_ref, ng_ref, w_ref, bg_ref = (next(it) for _ in range(4))
    uf_ref, up_ref, qkvo_ref, rows_ref, mg_ref = (next(it) for _ in range(5))

    x = x_ref[...]
    if add_pos:
        x = x + pos_ref[...]
    h = (_rms(x, ng_ref[0:1, :] * (1.0 + mod_ref[1:2, :])) + mod_ref[0:1, :]).astype(BF16)
    ck = 512
    za = _dot(h, w_ref[:, 0:ck])
    uf_ref[...] = za[:, :FOURIER_WIDTH].astype(BF16)
    up_ref[...] = za[:, FOURIER_WIDTH:].astype(BF16)
    for j in range(4 * MLSTM_WIDTH // ck):
        z = _dot(h, w_ref[:, (j + 1) * ck:(j + 2) * ck])
        if j * ck >= 3 * MLSTM_WIDTH:
            z = _sigmoid(z)
        qkvo_ref[:, j * ck:(j + 1) * ck] = z.astype(BF16)
    g = _dot(h, w_ref[:, CAT_G:CAT_W])
    for c in range(g.shape[0] // chunk):
        _gate_rows(g[c * chunk:(c + 1) * chunk, :], bg_ref[...], rows_ref, c)
    for j in range(3 * D_MODEL // ck):
        mg_ref[:, j * ck:(j + 1) * ck] = _sigmoid(_dot(h, w_ref[:, CAT_M + j * ck:CAT_M + (j + 1) * ck])).astype(BF16)


def _layer_spec(a, layer):
    return pl.BlockSpec((None,) + a.shape[1:], lambda *_: (layer,) + (0,) * (a.ndim - 1),
                        pipeline_mode=pl.Buffered(1))


def _inproj(x, mod, norm_g, w_cat, layer, b_gate, tm, t, pos=None):
    b, l, _ = x.shape
    add_pos = pos is not None
    assert tm % t == 0 and l % tm == 0
    bias_col = b_gate.reshape(N_GATE_COLS, 1)
    per_batch_mod = mod.shape[0] > 1
    tok = lambda w: pl.BlockSpec((None, tm, w), lambda i, j: (i, j, 0))
    full = lambda a: pl.BlockSpec(a.shape, lambda i, j: (0,) * a.ndim)
    in_specs = [tok(D_MODEL)]
    args = [x]
    if add_pos:
        in_specs.append(pl.BlockSpec((tm, D_MODEL), lambda i, j: (j, 0)))
        args.append(pos)
    in_specs += [
        pl.BlockSpec((None, 6, D_MODEL), (lambda i, j: (i, 0, 0)) if per_batch_mod else (lambda i, j: (0, 0, 0))),
        full(norm_g), _layer_spec(w_cat, layer), full(bias_col)]
    args += [mod, norm_g, w_cat, bias_col]
    rows_per_tile = tm // t * N_GATE_COLS
    out_specs = [tok(FOURIER_WIDTH), tok(POOL_WIDTH), tok(4 * MLSTM_WIDTH),
                 pl.BlockSpec((None, 4, rows_per_tile, t), lambda i, j: (i, 0, j, 0)), tok(3 * D_MODEL)]
    out_shape = [
        jax.ShapeDtypeStruct((b, l, FOURIER_WIDTH), BF16),
        jax.ShapeDtypeStruct((b, l, POOL_WIDTH), BF16),
        jax.ShapeDtypeStruct((b, l, 4 * MLSTM_WIDTH), BF16),
        jax.ShapeDtypeStruct((b, 4, l // t * N_GATE_COLS, t), F32),
        jax.ShapeDtypeStruct((b, l, 3 * D_MODEL), BF16)]
    return pl.pallas_call(
        functools.partial(_inproj_kernel, add_pos=add_pos, chunk=t),
        name="inproj",
        grid=(b, l // tm),
        in_specs=in_specs, out_specs=out_specs, out_shape=out_shape,
        compiler_params=pltpu.CompilerParams(
            dimension_semantics=("arbitrary", "arbitrary"), vmem_limit_bytes=VMEM_LIMIT),
    )(*args)


def _fourier_kernel(u_ref, ce_ref, se_ref, co_ref, so_ref, bc_ref, bs_ref, o_ref, vc_ref, vs_ref, *, scale, half):
    u = u_ref[...]
    nblk = u.shape[1] // LANES
    for ref, tab in ((vc_ref, bc_ref), (vs_ref, bs_ref)):
        v = _dot(u, tab[...])
        for k in range(nblk):
            ref[k] = v[:, k * LANES:(k + 1) * LANES]

    def rows(ref, start):
        return jnp.concatenate([ref[k, pl.ds(start, half, stride=2), :] for k in range(nblk)], axis=1).astype(BF16)

    e = _dot(ce_ref[...], rows(vc_ref, 0)) - _dot(se_ref[...], rows(vs_ref, 0))
    o = _dot(co_ref[...], rows(vc_ref, 1)) - _dot(so_ref[...], rows(vs_ref, 1))
    o_ref[0:half, :] = ((e + o) * scale).astype(BF16)
    o_ref[half:, :] = ((e - o) * scale).astype(BF16)


def _fourier(uf):
    b, l, w = uf.shape
    half = l // 2
    bc, bs = (jnp.asarray(t).astype(BF16) for t in _group_dft_tables())
    tables = [jnp.asarray(t).astype(BF16) for t in _dft_half_tables(l)]
    full = lambda a: pl.BlockSpec(a.shape, lambda i: (0,) * a.ndim)
    seq = pl.BlockSpec((None, l, w), lambda i: (i, 0, 0))
    return pl.pallas_call(
        functools.partial(_fourier_kernel, scale=float((l * FOURIER_GROUP_DIM) ** -0.5), half=half),
        name="fourier",
        grid=(b,),
        in_specs=[seq] + [full(t) for t in tables] + [full(bc), full(bs)],
        out_specs=seq,
        out_shape=jax.ShapeDtypeStruct((b, l, w), BF16),
        scratch_shapes=[pltpu.VMEM((w // LANES, l, LANES), F32), pltpu.VMEM((w // LANES, l, LANES), F32)],
        compiler_params=pltpu.CompilerParams(
            dimension_semantics=("arbitrary",), vmem_limit_bytes=VMEM_LIMIT),
    )(uf, *tables, bc, bs)


def _scan_lanes(x, reverse, op, fill):
    n = x.shape[-1]
    lane = lax.broadcasted_iota(jnp.int32, x.shape, x.ndim - 1)
    s = 1
    while s < n:
        if reverse:
            x = op(x, jnp.where(lane < n - s, pltpu.roll(x, n - s, x.ndim - 1), fill))
        else:
            x = op(x, jnp.where(lane >= s, pltpu.roll(x, s, x.ndim - 1), fill))
        s *= 2
    return x


SPLIT_ROWS = 16


def _mlstm_kernel(*refs, seq, chunk, hp, seq_group, has_init, emit_state, n_alias, unroll):
    it = iter(refs)
    q_ref, k_ref, v_ref, rows_ref = (next(it) for _ in range(4))
    c0_ref, n0_ref, m0_ref = (next(it) for _ in range(3)) if has_init else (None, None, None)
    if n_alias:
        [next(it) for _ in range(n_alias)]
    hm_ref = next(it)
    cs_ref, ns_ref, ms_ref = (next(it) for _ in range(3)) if emit_state else (None, None, None)
    hf_ref, hb_ref = (next(it) for _ in range(2))

    t = chunk
    nc = seq // t
    rep = t // DH
    scale = DH ** -0.5

    row_i = lax.broadcasted_iota(jnp.int32, (t, t), 0)
    col_i = lax.broadcasted_iota(jnp.int32, (t, t), 1)
    eye_dh = jnp.where(lax.broadcasted_iota(jnp.int32, (DH, DH), 0) == lax.broadcasted_iota(jnp.int32, (DH, DH), 1),
                       1.0, 0.0).astype(BF16)
    sub = lax.broadcasted_iota(jnp.int32, (SPLIT_ROWS, t), 0)
    part = sub % 3
    p_sub = lax.broadcasted_iota(jnp.int32, (SPLIT_ROWS, 2 * DH), 0)
    p_lane = lax.broadcasted_iota(jnp.int32, (SPLIT_ROWS, 2 * DH), 1)
    gather_mat = jnp.where(((p_sub < 3) & (p_lane < DH)) | ((p_sub >= 3) & (p_sub < 6) & (p_lane >= DH)),
                           1.0, 0.0).astype(BF16)
    ones_cols = jnp.ones((t, DH), BF16)

    def stage1(c, hh, direction, Cn):
        rows = pl.ds(pl.multiple_of(c * t, t), t)
        cols = slice(hh * DH, (hh + 1) * DH)
        chunk_row = (pl.program_id(0) % seq_group) * nc + c
        r = pl.ds(chunk_row * N_GATE_COLS + direction * 2 * HEADS + pl.program_id(1) * hp + hh, 1)
        a_r = rows_ref[0, r, :]
        b_r = rows_ref[1, r, :]
        g_r = rows_ref[2, r, :]
        p_r = rows_ref[3, r, :]
        if direction == 0:
            end = t - 1
            mask = col_i <= row_i
        else:
            end = 0
            mask = col_i >= row_i
        b_end = b_r[:, end:end + 1]
        p_end = p_r[:, end:end + 1]

        x0 = jnp.where(sub < 3, b_r, jnp.where(sub < 6, g_r, 0.0))
        x1 = x0 - x0.astype(BF16).astype(F32)
        x2 = x1 - x1.astype(BF16).astype(F32)
        xs = jnp.where(part == 0, x0, jnp.where(part == 1, x1, x2)).astype(BF16)
        bg = lax.dot_general(xs, gather_mat, (((0,), (0,)), ((), ())), preferred_element_type=F32)
        qc = q_ref[rows, cols]
        kc = k_ref[rows, cols]
        v1 = jnp.concatenate([v_ref[rows, cols], ones_cols], axis=1)
        qk = lax.dot_general(qc, kc, (((1,), (1,)), ((), ())), preferred_element_type=F32)
        qs = _dot(qc, Cn.astype(BF16))
        kt = lax.dot_general(eye_dh, kc, (((1,), (1,)), ((), ())), preferred_element_type=F32)
        return rows, cols, mask, a_r, b_end, p_end, bg, v1, qk, qs, kt

    def stage2(st, Cn, m, h_ref):
        rows, cols, mask, a_r, b_end, p_end, bg, v1, qk, qs, kt = st
        b_c = bg[:, :DH]
        m_t = jnp.maximum(b_c + m, bg[:, DH:])
        mu = m_t - b_c
        sdec = jnp.exp(m - mu)
        mu_t = jnp.concatenate([mu] * rep, axis=1)
        d = jnp.where(mask, jnp.exp((a_r + np.float32(np.log(scale))) - mu_t), 0.0)
        sv = _dot((qk * d).astype(BF16), v1)
        num = sdec * qs[:, :DH] + sv[:, :DH]
        den = sdec * qs[:, DH:] + sv[:, DH:]
        h_ref[rows, cols] = num * (1.0 / jnp.maximum(jnp.abs(den), jnp.exp(-m_t)))

        mx = jnp.maximum(m, p_end)
        w_r = jnp.exp(a_r - mx) * scale
        Cn_new = jnp.exp(m - mx) * Cn + _dot((kt * w_r).astype(BF16), v1)
        return Cn_new, b_end + mx

    def body(ci, carry):
        chains = [(hh, d) for hh in range(hp) for d in range(N_DIR)]
        chunk_of = lambda d: ci if d == 0 else nc - 1 - ci
        firsts = [stage1(chunk_of(d), hh, d, carry[2 * k]) for k, (hh, d) in enumerate(chains)]
        out = []
        for k, (hh, d) in enumerate(chains):
            out += stage2(firsts[k], carry[2 * k], carry[2 * k + 1], hf_ref if d == 0 else hb_ref)
        return tuple(out)

    init = []
    for hh in range(hp):
        for d in range(N_DIR):
            if has_init:
                n_col = jnp.sum(eye_dh.astype(F32) * n0_ref[d, hh], axis=1, keepdims=True)
                init += [jnp.concatenate([c0_ref[d, hh], jnp.broadcast_to(n_col, (DH, DH))], axis=1),
                         m0_ref[d, hh][:, 0:1]]
            else:
                init += [jnp.zeros((DH, 2 * DH), F32), jnp.zeros((1, 1), F32)]
    final = lax.fori_loop(0, nc, body, tuple(init), unroll=unroll)

    hm_ref[...] = (hf_ref[...] + hb_ref[...]).astype(BF16)
    if emit_state:
        eye_f = eye_dh.astype(F32)
        for hh in range(hp):
            for d in range(N_DIR):
                Cn, m = final[2 * (hh * N_DIR + d):2 * (hh * N_DIR + d) + 2]
                cs_ref[d, hh] = Cn[:, :DH]
                ns_ref[d, hh] = jnp.sum(eye_f * Cn[:, DH:], axis=0, keepdims=True)
                ms_ref[d, hh] = jnp.broadcast_to(m, (1, LANES))


def _mlstm(qkvo, grows, init, state_out, hp, unroll):
    b, l, _ = qkvo.shape
    emit_state = state_out is not None
    t = grows.shape[-1]
    nc = l // t
    seq_group = b // grows.shape[0]
    assert l % t == 0 and t % DH == 0 and HEADS % hp == 0 and nc % unroll == 0
    assert grows.shape[2] == seq_group * nc * N_GATE_COLS
    has_init = init is not None
    groups = HEADS // hp
    head_cols = lambda k: pl.BlockSpec((None, l, hp * DH), lambda i, h: (i, 0, k * groups + h))
    in_specs = [
        head_cols(0), head_cols(1), head_cols(2),
        pl.BlockSpec((None,) + grows.shape[1:], lambda i, h: (i // seq_group, 0, 0, 0))]
    args = [qkvo, qkvo, qkvo, grows]
    if has_init:
        c0, n0, m0, layer = init
        in_specs += [
            pl.BlockSpec((None, None, N_DIR, hp, DH, DH), lambda i, h: (i, layer, 0, h, 0, 0)),
            pl.BlockSpec((None, None, N_DIR, hp, 1, DH), lambda i, h: (i, layer, 0, h, 0, 0)),
            pl.BlockSpec((None, None, N_DIR, hp, 1, LANES), lambda i, h: (i, layer, 0, h, 0, 0))]
        args += [c0, n0, m0]
    out_specs = [pl.BlockSpec((None, l, hp * DH), lambda i, h: (i, 0, h))]
    out_shape = [jax.ShapeDtypeStruct((b, l, MLSTM_WIDTH), BF16)]
    aliases = {}
    if emit_state:
        out_layer, prev = state_out
        slab = lambda w: pl.BlockSpec((None, None, N_DIR, hp, w, DH), lambda i, h: (i, out_layer, 0, h, 0, 0))
        out_specs += [slab(DH), slab(1), slab(1)]
        out_shape += [jax.ShapeDtypeStruct((b, DEPTH, N_DIR, HEADS, DH, DH), F32),
                      jax.ShapeDtypeStruct((b, DEPTH, N_DIR, HEADS, 1, DH), F32),
                      jax.ShapeDtypeStruct((b, DEPTH, N_DIR, HEADS, 1, DH), F32)]
        if prev is not None:
            aliases = {len(args) + k: 1 + k for k in range(len(prev))}
            in_specs += [pl.BlockSpec(memory_space=pl.ANY)] * len(prev)
            args += list(prev)
    return pl.pallas_call(
        functools.partial(_mlstm_kernel, seq=l, chunk=t, hp=hp, seq_group=seq_group, has_init=has_init,
                          emit_state=emit_state, n_alias=len(aliases), unroll=unroll),
        name="mlstm",
        grid=(b, groups),
        in_specs=in_specs, out_specs=out_specs, out_shape=out_shape,
        input_output_aliases=aliases,
        scratch_shapes=[
            pltpu.VMEM((l, hp * DH), F32),
            pltpu.VMEM((l, hp * DH), F32)],
        compiler_params=pltpu.CompilerParams(
            dimension_semantics=("arbitrary", "arbitrary"), vmem_limit_bytes=VMEM_LIMIT),
    )(*args)


def _merge_ffn_kernel(*refs, tm, add_pos):
    it = iter(refs)
    x_ref = next(it)
    pos_ref = next(it) if add_pos else None
    (mod_ref, ng_ref, yf_ref, up_ref, hm_ref, og_ref, mg_ref, band_ref, icnt_ref, wpool_ref, pscale_ref,
     wpf_ref, wpp_ref, wpm_ref, wout_ref, wfi_ref, wfo_ref, o_ref) = (next(it) for _ in range(18))
    x = x_ref[...]
    if add_pos:
        x = x + pos_ref[...]

    blocks = [slice(r * POOL_BLOCK, (r + 1) * POOL_BLOCK) for r in range(tm // POOL_BLOCK)]

    lane_group = lax.broadcasted_iota(jnp.int32, (POOL_BLOCK, POOL_WIDTH), 1) // POOL_GROUP_DIM
    pooled = []
    for rows in blocks:
        u = up_ref[rows, :]
        acc = jnp.zeros((POOL_BLOCK, POOL_WIDTH), F32)
        for g in range(len(POOL_WINDOWS)):
            acc = jnp.where(lane_group == g, _dot(band_ref[g], u), acc)
        pooled.append((acc * icnt_ref[...] - u.astype(F32)).astype(BF16))

    merged = []
    for rows, p in zip(blocks, pooled):
        pp_pre = _dot(p, wpool_ref[...])
        y_f = _dot(yf_ref[rows, :], wpf_ref[...])
        hg = (og_ref[rows, :].astype(F32) * hm_ref[rows, :].astype(F32)).astype(BF16)
        y_m = _dot(hg, wpm_ref[...])
        y_p = _dot((pp_pre * pscale_ref[...]).astype(BF16), wpp_ref[...])
        g_f = mg_ref[rows, 0:D_MODEL].astype(F32)
        g_p = mg_ref[rows, D_MODEL:2 * D_MODEL].astype(F32)
        g_m = mg_ref[rows, 2 * D_MODEL:3 * D_MODEL].astype(F32)
        merged.append((g_f * y_f + g_p * y_p + g_m * y_m).astype(BF16))

    x1_blocks, h2_blocks = [], []
    for rows, y in zip(blocks, merged):
        x1b = x[rows, :] + _rms(_dot(y, wout_ref[...]), ng_ref[1:2, :] * mod_ref[2:3, :])
        x1_blocks.append(x1b)
        h2_blocks.append((_rms(x1b, ng_ref[2:3, :] * (1.0 + mod_ref[4:5, :])) + mod_ref[3:4, :]).astype(BF16))
    h2 = h2_blocks[0] if len(blocks) == 1 else jnp.concatenate(h2_blocks, axis=0)
    acc = jnp.zeros((tm, D_MODEL), F32)
    for k, (c0, c1) in enumerate(FFN_CHUNKS):
        if k == 0:
            ab = [(_dot(hb, wfi_ref[:, c0:c1]), _dot(hb, wfi_ref[:, D_FF + c0:D_FF + c1])) for hb in h2_blocks]
            a = jnp.concatenate([t[0] for t in ab], axis=0)
            bb = jnp.concatenate([t[1] for t in ab], axis=0)
        else:
            a = _dot(h2, wfi_ref[:, c0:c1])
            bb = _dot(h2, wfi_ref[:, D_FF + c0:D_FF + c1])
        act = (a * _sigmoid(a) * bb).astype(BF16)
        if k < len(FFN_CHUNKS) - 1:
            acc = acc + _dot(act, wfo_ref[c0:c1, :])
        else:
            for rows, x1b in zip(blocks, x1_blocks):
                yb = acc[rows, :] + _dot(act[rows, :], wfo_ref[c0:c1, :])
                o_ref[rows, :] = x1b + _rms(yb, ng_ref[3:4, :] * mod_ref[5:6, :])


def _merge_ffn(x, mod, norm_g, yf, up, hm, qkvo, mg, band, icnt, wpool, pscale, lw, layer, tm, pos=None):
    b, l, _ = x.shape
    add_pos = pos is not None
    ogate = pl.BlockSpec((None, tm, MLSTM_WIDTH), lambda i, j: (i, j, 3))
    per_batch_mod = mod.shape[0] > 1
    tok = lambda w: pl.BlockSpec((None, tm, w), lambda i, j: (i, j, 0))
    full = lambda a: pl.BlockSpec(a.shape, lambda i, j: (0,) * a.ndim)
    stacked = [lw[k] for k in ("wpf", "wpp", "wpm", "wout", "wfi", "wfo")]
    in_specs = [tok(D_MODEL)]
    args = [x]
    if add_pos:
        in_specs.append(pl.BlockSpec((tm, D_MODEL), lambda i, j: (j, 0)))
        args.append(pos)
    in_specs += [
        pl.BlockSpec((None, 6, D_MODEL), (lambda i, j: (i, 0, 0)) if per_batch_mod else (lambda i, j: (0, 0, 0))),
        full(norm_g), tok(FOURIER_WIDTH), tok(POOL_WIDTH), tok(MLSTM_WIDTH), ogate, tok(3 * D_MODEL),
        full(band), full(icnt), full(wpool), full(pscale)]
    in_specs += [_layer_spec(a, layer) for a in stacked]
    args += [mod, norm_g, yf, up, hm, qkvo, mg, band, icnt, wpool, pscale] + stacked
    return pl.pallas_call(
        functools.partial(_merge_ffn_kernel, tm=tm, add_pos=add_pos),
        name="merge_ffn",
        grid=(b, l // tm),
        in_specs=in_specs,
        out_specs=tok(D_MODEL),
        out_shape=jax.ShapeDtypeStruct((b, l, D_MODEL), F32),
        compiler_params=pltpu.CompilerParams(
            dimension_semantics=("arbitrary", "arbitrary"), vmem_limit_bytes=VMEM_LIMIT),
    )(*args)


def _repack_kernel(w_ref, o_ref):
    o_ref[:, 0:CAT_M] = w_ref[:, 0:OFF_G].astype(BF16)
    o_ref[:, CAT_M:CAT_G] = w_ref[:, OFF_M:D_IN].astype(BF16)
    tail = w_ref[:, OFF_G:OFF_G + LANES]
    lane = lax.broadcasted_iota(jnp.int32, tail.shape, 1)
    o_ref[:, CAT_G:CAT_W] = jnp.where(lane < N_GATE_COLS, tail, 0.0).astype(BF16)


def _repack_w_in(w_in):
    tk = 128
    return pl.pallas_call(
        _repack_kernel,
        name="repack_w_in",
        grid=(DEPTH, D_MODEL // tk),
        in_specs=[pl.BlockSpec((None, tk, D_IN), lambda l, k: (l, k, 0))],
        out_specs=pl.BlockSpec((None, tk, CAT_W), lambda l, k: (l, k, 0)),
        out_shape=jax.ShapeDtypeStruct((DEPTH, D_MODEL, CAT_W), BF16),
        compiler_params=pltpu.CompilerParams(
            dimension_semantics=("arbitrary", "arbitrary"), vmem_limit_bytes=VMEM_LIMIT),
    )(w_in)


def _stacked_weights(w_in, w_proj_f, w_proj_p, w_proj_m, w_out, w_ffn_in, w_ffn_out):
    return dict(
        w_cat=_repack_w_in(w_in),
        wpf=w_proj_f.astype(BF16), wpp=w_proj_p.astype(BF16), wpm=w_proj_m.astype(BF16),
        wout=w_out.astype(BF16), wfi=w_ffn_in.astype(BF16), wfo=w_ffn_out.astype(BF16))


def _block(x, mod, lw, layer, norm_g, b_gate, wpool, pscale, width, tm, fold, init, state_out, hp, unroll, pos=None):
    b, l, _ = x.shape
    folded = lambda a: a.reshape(b // fold, fold * l, a.shape[-1])
    unfolded = lambda a: a.reshape(b, l, a.shape[-1])
    uf, up, qkvo, grows, mg = _inproj(folded(x), mod, norm_g, lw["w_cat"], layer, b_gate, tm, min(MLSTM_CHUNK, l), pos)
    yf = _fourier(unfolded(uf))
    mouts = _mlstm(unfolded(qkvo), grows, init, state_out, hp, unroll)
    band, icnt = _pool_tables(width)
    x = _merge_ffn(folded(x), mod, norm_g, folded(yf), up, folded(mouts[0]), qkvo, mg,
                   jnp.asarray(band).astype(BF16), jnp.asarray(icnt), wpool, pscale, lw, layer, tm, pos)
    return unfolded(x), mouts[1:]


def kernel(x_prompt, x_sample, state_C, state_n, state_m, c, c_ctx, w_ada, b_ada, norm_g, w_in, b_gate, w_proj_f, w_pool, pool_scale, w_proj_p, w_proj_m, w_out, w_ffn_in, w_ffn_out):
    bp, lp, _ = x_prompt.shape
    bs, ls, _ = x_sample.shape
    cond = jnp.concatenate([c_ctx[None, :], c, jnp.zeros((COND_ROWS - 1 - bs, D_MODEL), F32)], axis=0)
    mod = _modulation(cond, w_ada, b_ada).reshape(DEPTH, COND_ROWS, 6, D_MODEL)
    lw = _stacked_weights(w_in, w_proj_f, w_proj_p, w_proj_m, w_out, w_ffn_in, w_ffn_out)
    group_eye = jnp.eye(len(POOL_WINDOWS), dtype=w_pool.dtype)[None, :, None, :, None]
    wpool = (w_pool[:, :, :, None, :] * group_eye).reshape(DEPTH, POOL_WIDTH, POOL_WIDTH).astype(BF16)
    layer_args = [(lw, i, norm_g[i], b_gate[i], wpool[i], pool_scale[i].reshape(1, POOL_WIDTH))
                  for i in range(DEPTH)]

    xp = x_prompt
    states = None
    for i in range(DEPTH):
        xp, states = _block(xp, mod[i, 0:1], *layer_args[i], lp, TOKEN_TILE, TOKEN_TILE // lp, None, (i, states),
                            HEADS, 1)
    new_c, new_n, new_m = states

    xs = x_sample
    pos = jnp.asarray(_pos_table(ls // GRID_W))
    n0 = state_n.reshape(bs, DEPTH, N_DIR, HEADS, 1, DH)
    m0 = jnp.broadcast_to(state_m[..., None, None], (bs, DEPTH, N_DIR, HEADS, 1, LANES))
    for i in range(DEPTH):
        init = (state_C, n0, m0, i)
        xs, _ = _block(xs, mod[i, 1:1 + bs], *layer_args[i], GRID_W, TOKEN_TILE, 1, init, None, HEADS, 1,
                       pos if i == 0 else None)

    return (xp, xs, new_c, new_n[:, :, :, :, 0, :], new_m[:, :, :, :, 0, 0])
```

```python
import functools

import numpy as np
import jax
import jax.numpy as jnp
from jax import lax
from jax.experimental import pallas as pl
from jax.experimental.pallas import tpu as pltpu

F32 = jnp.float32
BF16 = jnp.bfloat16

D_MODEL = 1024
DEPTH = 2
GRID_W = 64
FOURIER_GROUPS = 4
FOURIER_WIDTH = D_MODEL // 4
FOURIER_GROUP_DIM = FOURIER_WIDTH // FOURIER_GROUPS
POOL_WINDOWS = (2, 4, 8, 16)
POOL_WIDTH = D_MODEL // 4
POOL_GROUP_DIM = POOL_WIDTH // len(POOL_WINDOWS)
HEADS = 4
MLSTM_WIDTH = D_MODEL // 2
DH = MLSTM_WIDTH // HEADS
N_DIR = 2
N_GATE_COLS = N_DIR * 2 * HEADS
D_FF = -(-8 * D_MODEL // (3 * 256)) * 256
RMS_EPS = 1e-6
POS_BASE = 10000.0

OFF_F = 0
OFF_P = OFF_F + FOURIER_WIDTH
OFF_QKVO = OFF_P + POOL_WIDTH
OFF_G = OFF_QKVO + 4 * MLSTM_WIDTH
OFF_M = OFF_G + N_GATE_COLS
D_IN = OFF_M + 3 * D_MODEL

LANES = 128
MLSTM_CHUNK = 256
POOL_BLOCK = 256
TOKEN_TILE = 512
COND_ROWS = 16
VMEM_LIMIT = 60 * 1024 * 1024
MXU_COLS = 256
FFN_CHUNKS = ((0, 5 * MXU_COLS), (5 * MXU_COLS, D_FF))
CAT_M = OFF_G
CAT_G = CAT_M + 3 * D_MODEL
CAT_W = CAT_G + LANES


def _dot(a, b):
    return jnp.dot(a, b, preferred_element_type=F32)


def _rms(x, g):
    return x * lax.rsqrt(jnp.mean(x * x, axis=-1, keepdims=True) + RMS_EPS) * g


def _log_sigmoid(x):
    return jnp.minimum(x, 0.0) - jnp.log(1.0 + jnp.exp(-jnp.abs(x)))


def _sigmoid(x):
    return 0.5 * jnp.tanh(0.5 * x) + 0.5


@functools.lru_cache(maxsize=None)
def _dft_tables(n):
    idx = np.arange(n, dtype=np.int64)
    ang = 2.0 * np.pi * ((idx[:, None] * idx[None, :]) % n).astype(np.float64) / n
    return np.cos(ang).astype(np.float32), np.sin(ang).astype(np.float32)


@functools.lru_cache(maxsize=None)
def _dft_half_tables(n):
    f = np.arange(n // 2, dtype=np.int64)[:, None]
    j = np.arange(n // 2, dtype=np.int64)[None, :]
    out = []
    for l in (2 * j, 2 * j + 1):
        ang = 2.0 * np.pi * ((f * l) % n).astype(np.float64) / n
        out += [np.cos(ang).astype(np.float32), np.sin(ang).astype(np.float32)]
    return tuple(out)


@functools.lru_cache(maxsize=None)
def _group_dft_tables():
    c, s = _dft_tables(FOURIER_GROUP_DIM)
    bc = np.zeros((FOURIER_WIDTH, FOURIER_WIDTH), np.float32)
    bs = np.zeros((FOURIER_WIDTH, FOURIER_WIDTH), np.float32)
    for g in range(FOURIER_GROUPS):
        sl = slice(g * FOURIER_GROUP_DIM, (g + 1) * FOURIER_GROUP_DIM)
        bc[sl, sl] = c
        bs[sl, sl] = s
    return bc, bs


@functools.lru_cache(maxsize=None)
def _pool_tables(width):
    band = np.zeros((len(POOL_WINDOWS), POOL_BLOCK, POOL_BLOCK), np.float32)
    inv_cnt = np.zeros((POOL_BLOCK, POOL_WIDTH), np.float32)
    for g, w in enumerate(POOL_WINDOWS):
        left = w // 2
        right = w - 1 - left
        for t in range(POOL_BLOCK):
            row, pos = divmod(t, width)
            lo = min(max(pos - left, 0), width - 1)
            hi = min(max(pos + right, 0), width - 1)
            band[g, t, row * width + lo:row * width + hi + 1] = 1.0
            inv_cnt[t, g * POOL_GROUP_DIM:(g + 1) * POOL_GROUP_DIM] = 1.0 / (hi - lo + 1)
    return band, inv_cnt


@functools.lru_cache(maxsize=None)
def _pos_table(rows):
    quarter = D_MODEL // 4
    omega = 1.0 / (POS_BASE ** (np.arange(quarter, dtype=np.float64) / quarter))
    t = np.arange(rows * GRID_W)
    r = (t // GRID_W).astype(np.float64)
    col = (t % GRID_W).astype(np.float64)
    ar = r[:, None] * omega[None, :]
    ac = col[:, None] * omega[None, :]
    return np.concatenate([np.sin(ar), np.cos(ar), np.sin(ac), np.cos(ac)], axis=-1).astype(np.float32)


def _mod_kernel(c_ref, w_ref, b_ref, o_ref):
    @pl.when(pl.program_id(1) == 0)
    def _():
        o_ref[...] = jnp.broadcast_to(b_ref[...], o_ref.shape)

    c = c_ref[...]
    a = (c * jax.nn.sigmoid(c)).astype(BF16)
    o_ref[...] += _dot(a, w_ref[...].astype(BF16))


def _modulation(cond, w_ada, b_ada):
    tk = 256
    n = 6 * D_MODEL
    return pl.pallas_call(
        _mod_kernel,
        name="adaln_mod",
        grid=(DEPTH, D_MODEL // tk),
        in_specs=[
            pl.BlockSpec((COND_ROWS, tk), lambda l, k: (0, k)),
            pl.BlockSpec((None, tk, n), lambda l, k: (l, k, 0)),
            pl.BlockSpec((None, 1, n), lambda l, k: (l, 0, 0)),
        ],
        out_specs=pl.BlockSpec((None, COND_ROWS, n), lambda l, k: (l, 0, 0)),
        out_shape=jax.ShapeDtypeStruct((DEPTH, COND_ROWS, 6 * D_MODEL), F32),
        compiler_params=pltpu.CompilerParams(
            dimension_semantics=("arbitrary", "arbitrary"), vmem_limit_bytes=VMEM_LIMIT),
    )(cond, w_ada, b_ada.reshape(DEPTH, 1, 6 * D_MODEL))


def _gate_rows(g, bias_col, rows_ref, chunk):
    x = g.T[:N_GATE_COLS, :] + bias_col
    row = lax.broadcasted_iota(jnp.int32, x.shape, 0)
    fwd = row < 2 * HEADS
    lf = _log_sigmoid(x)
    b_f = jnp.where(fwd, _scan_lanes(lf, False, jnp.add, 0.0), _scan_lanes(lf, True, jnp.add, 0.0))
    b_i = pltpu.roll(b_f, N_GATE_COLS - HEADS, 0)
    a = x - b_i
    p = jnp.where(fwd, _scan_lanes(a, False, jnp.maximum, -jnp.inf), _scan_lanes(a, True, jnp.maximum, -jnp.inf))
    sl = slice(chunk * N_GATE_COLS, (chunk + 1) * N_GATE_COLS)
    rows_ref[0, sl, :] = a
    rows_ref[1, sl, :] = b_i
    rows_ref[2, sl, :] = b_i + p
    rows_ref[3, sl, :] = p


def _inproj_kernel(*refs, add_pos, chunk):
    it = iter(refs)
    x_ref = next(it)
    pos_ref = next(it) if add_pos else None
    mod_ref, ng_ref, w_ref, bg_ref = (next(it) for _ in range(4))
    uf_ref, up_ref, qkvo_ref, rows_ref, mg_ref = (next(it) for _ in range(5))

    x = x_ref[...]
    if add_pos:
        x = x + pos_ref[...]
    h = (_rms(x, ng_ref[0:1, :] * (1.0 + mod_ref[1:2, :])) + mod_ref[0:1, :]).astype(BF16)
    ck = 512
    za = _dot(h, w_ref[:, 0:ck])
    uf_ref[...] = za[:, :FOURIER_WIDTH].astype(BF16)
    up_ref[...] = za[:, FOURIER_WIDTH:].astype(BF16)
    for j in range(4 * MLSTM_WIDTH // ck):
        z = _dot(h, w_ref[:, (j + 1) * ck:(j + 2) * ck])
        if j * ck >= 3 * MLSTM_WIDTH:
            z = _sigmoid(z)
        qkvo_ref[:, j * ck:(j + 1) * ck] = z.astype(BF16)
    g = _dot(h, w_ref[:, CAT_G:CAT_W])
    for c in range(g.shape[0] // chunk):
        _gate_rows(g[c * chunk:(c + 1) * chunk, :], bg_ref[...], rows_ref, c)
    for j in range(3 * D_MODEL // ck):
        mg_ref[:, j * ck:(j + 1) * ck] = _sigmoid(_dot(h, w_ref[:, CAT_M + j * ck:CAT_M + (j + 1) * ck])).astype(BF16)


def _layer_spec(a, layer):
    return pl.BlockSpec((None,) + a.shape[1:], lambda *_: (layer,) + (0,) * (a.ndim - 1),
                        pipeline_mode=pl.Buffered(1))


def _inproj(x, mod, norm_g, w_cat, layer, b_gate, tm, t, pos=None):
    b, l, _ = x.shape
    add_pos = pos is not None
    assert tm % t == 0 and l % tm == 0
    bias_col = b_gate.reshape(N_GATE_COLS, 1)
    per_batch_mod = mod.shape[0] > 1
    tok = lambda w: pl.BlockSpec((None, tm, w), lambda i, j: (i, j, 0))
    full = lambda a: pl.BlockSpec(a.shape, lambda i, j: (0,) * a.ndim)
    in_specs = [tok(D_MODEL)]
    args = [x]
    if add_pos:
        in_specs.append(pl.BlockSpec((tm, D_MODEL), lambda i, j: (j, 0)))
        args.append(pos)
    in_specs += [
        pl.BlockSpec((None, 6, D_MODEL), (lambda i, j: (i, 0, 0)) if per_batch_mod else (lambda i, j: (0, 0, 0))),
        full(norm_g), _layer_spec(w_cat, layer), full(bias_col)]
    args += [mod, norm_g, w_cat, bias_col]
    rows_per_tile = tm // t * N_GATE_COLS
    out_specs = [tok(FOURIER_WIDTH), tok(POOL_WIDTH), tok(4 * MLSTM_WIDTH),
                 pl.BlockSpec((None, 4, rows_per_tile, t), lambda i, j: (i, 0, j, 0)), tok(3 * D_MODEL)]
    out_shape = [
        jax.ShapeDtypeStruct((b, l, FOURIER_WIDTH), BF16),
        jax.ShapeDtypeStruct((b, l, POOL_WIDTH), BF16),
        jax.ShapeDtypeStruct((b, l, 4 * MLSTM_WIDTH), BF16),
        jax.ShapeDtypeStruct((b, 4, l // t * N_GATE_COLS, t), F32),
        jax.ShapeDtypeStruct((b, l, 3 * D_MODEL), BF16)]
    return pl.pallas_call(
        functools.partial(_inproj_kernel, add_pos=add_pos, chunk=t),
        name="inproj",
        grid=(b, l // tm),
        in_specs=in_specs, out_specs=out_specs, out_shape=out_shape,
        compiler_params=pltpu.CompilerParams(
            dimension_semantics=("arbitrary", "arbitrary"), vmem_limit_bytes=VMEM_LIMIT),
    )(*args)


def _fourier_kernel(u_ref, ce_ref, se_ref, co_ref, so_ref, bc_ref, bs_ref, o_ref, vc_ref, vs_ref, *, scale, half):
    u = u_ref[...]
    nblk = u.shape[1] // LANES
    for ref, tab in ((vc_ref, bc_ref), (vs_ref, bs_ref)):
        v = _dot(u, tab[...])
        for k in range(nblk):
            ref[k] = v[:, k * LANES:(k + 1) * LANES]

    def rows(ref, start):
        return jnp.concatenate([ref[k, pl.ds(start, half, stride=2), :] for k in range(nblk)], axis=1).astype(BF16)

    e = _dot(ce_ref[...], rows(vc_ref, 0)) - _dot(se_ref[...], rows(vs_ref, 0))
    o = _dot(co_ref[...], rows(vc_ref, 1)) - _dot(so_ref[...], rows(vs_ref, 1))
    o_ref[0:half, :] = ((e + o) * scale).astype(BF16)
    o_ref[half:, :] = ((e - o) * scale).astype(BF16)


def _fourier(uf):
    b, l, w = uf.shape
    half = l // 2
    bc, bs = (jnp.asarray(t).astype(BF16) for t in _group_dft_tables())
    tables = [jnp.asarray(t).astype(BF16) for t in _dft_half_tables(l)]
    full = lambda a: pl.BlockSpec(a.shape, lambda i: (0,) * a.ndim)
    seq = pl.BlockSpec((None, l, w), lambda i: (i, 0, 0))
    return pl.pallas_call(
        functools.partial(_fourier_kernel, scale=float((l * FOURIER_GROUP_DIM) ** -0.5), half=half),
        name="fourier",
        grid=(b,),
        in_specs=[seq] + [full(t) for t in tables] + [full(bc), full(bs)],
        out_specs=seq,
        out_shape=jax.ShapeDtypeStruct((b, l, w), BF16),
        scratch_shapes=[pltpu.VMEM((w // LANES, l, LANES), F32), pltpu.VMEM((w // LANES, l, LANES), F32)],
        compiler_params=pltpu.CompilerParams(
            dimension_semantics=("arbitrary",), vmem_limit_bytes=VMEM_LIMIT),
    )(uf, *tables, bc, bs)


def _scan_lanes(x, reverse, op, fill):
    n = x.shape[-1]
    lane = lax.broadcasted_iota(jnp.int32, x.shape, x.ndim - 1)
    s = 1
    while s < n:
        if reverse:
            x = op(x, jnp.where(lane < n - s, pltpu.roll(x, n - s, x.ndim - 1), fill))
        else:
            x = op(x, jnp.where(lane >= s, pltpu.roll(x, s, x.ndim - 1), fill))
        s *= 2
    return x


SPLIT_ROWS = 16


def _mlstm_kernel(*refs, seq, chunk, hp, seq_group, has_init, emit_state, n_prev, unroll):
    it = iter(refs)
    q_ref, k_ref, v_ref, rows_ref = (next(it) for _ in range(4))
    c0_ref, n0_ref, m0_ref = (next(it) for _ in range(3)) if has_init else (None, None, None)
    pc_ref, pn_ref, pm_ref = (next(it) for _ in range(3)) if n_prev else (None, None, None)
    hm_ref = next(it)
    cs_ref, ns_ref, ms_ref = (next(it) for _ in range(3)) if emit_state else (None, None, None)
    hf_ref, hb_ref = (next(it) for _ in range(2))

    t = chunk
    nc = seq // t
    rep = t // DH
    scale = DH ** -0.5

    row_i = lax.broadcasted_iota(jnp.int32, (t, t), 0)
    col_i = lax.broadcasted_iota(jnp.int32, (t, t), 1)
    eye_dh = jnp.where(lax.broadcasted_iota(jnp.int32, (DH, DH), 0) == lax.broadcasted_iota(jnp.int32, (DH, DH), 1),
                       1.0, 0.0).astype(BF16)
    sub = lax.broadcasted_iota(jnp.int32, (SPLIT_ROWS, t), 0)
    part = sub % 3
    p_sub = lax.broadcasted_iota(jnp.int32, (SPLIT_ROWS, 2 * DH), 0)
    p_lane = lax.broadcasted_iota(jnp.int32, (SPLIT_ROWS, 2 * DH), 1)
    gather_mat = jnp.where(((p_sub < 3) & (p_lane < DH)) | ((p_sub >= 3) & (p_sub < 6) & (p_lane >= DH)),
                           1.0, 0.0).astype(BF16)
    ones_cols = jnp.ones((t, DH), BF16)

    def stage1(c, hh, direction, Cn):
        rows = pl.ds(pl.multiple_of(c * t, t), t)
        cols = slice(hh * DH, (hh + 1) * DH)
        chunk_row = (pl.program_id(0) % seq_group) * nc + c
        r = pl.ds(chunk_row * N_GATE_COLS + direction * 2 * HEADS + pl.program_id(1) * hp + hh, 1)
        a_r = rows_ref[0, r, :]
        b_r = rows_ref[1, r, :]
        g_r = rows_ref[2, r, :]
        p_r = rows_ref[3, r, :]
        if direction == 0:
            end = t - 1
            mask = col_i <= row_i
        else:
            end = 0
            mask = col_i >= row_i
        b_end = b_r[:, end:end + 1]
        p_end = p_r[:, end:end + 1]

        x0 = jnp.where(sub < 3, b_r, jnp.where(sub < 6, g_r, 0.0))
        x1 = x0 - x0.astype(BF16).astype(F32)
        x2 = x1 - x1.astype(BF16).astype(F32)
        xs = jnp.where(part == 0, x0, jnp.where(part == 1, x1, x2)).astype(BF16)
        bg = lax.dot_general(xs, gather_mat, (((0,), (0,)), ((), ())), preferred_element_type=F32)
        qc = q_ref[rows, cols]
        kc = k_ref[rows, cols]
        v1 = jnp.concatenate([v_ref[rows, cols], ones_cols], axis=1)
        qk = lax.dot_general(qc, kc, (((1,), (1,)), ((), ())), preferred_element_type=F32)
        qs = _dot(qc, Cn.astype(BF16))
        kt = lax.dot_general(eye_dh, kc, (((1,), (1,)), ((), ())), preferred_element_type=F32)
        return rows, cols, mask, a_r, b_end, p_end, bg, v1, qk, qs, kt

    def stage2(st, Cn, m, h_ref):
        rows, cols, mask, a_r, b_end, p_end, bg, v1, qk, qs, kt = st
        b_c = bg[:, :DH]
        m_t = jnp.maximum(b_c + m, bg[:, DH:])
        mu = m_t - b_c
        sdec = jnp.exp(m - mu)
        mu_t = jnp.concatenate([mu] * rep, axis=1)
        d = jnp.where(mask, jnp.exp((a_r + np.float32(np.log(scale))) - mu_t), 0.0)
        sv = _dot((qk * d).astype(BF16), v1)
        num = sdec * qs[:, :DH] + sv[:, :DH]
        den = sdec * qs[:, DH:] + sv[:, DH:]
        h_ref[rows, cols] = num * (1.0 / jnp.maximum(jnp.abs(den), jnp.exp(-m_t)))

        mx = jnp.maximum(m, p_end)
        w_r = jnp.exp(a_r - mx) * scale
        Cn_new = jnp.exp(m - mx) * Cn + _dot((kt * w_r).astype(BF16), v1)
        return Cn_new, b_end + mx

    def body(ci, carry):
        chains = [(hh, d) for hh in range(hp) for d in range(N_DIR)]
        chunk_of = lambda d: ci if d == 0 else nc - 1 - ci
        firsts = [stage1(chunk_of(d), hh, d, carry[2 * k]) for k, (hh, d) in enumerate(chains)]
        out = []
        for k, (hh, d) in enumerate(chains):
            out += stage2(firsts[k], carry[2 * k], carry[2 * k + 1], hf_ref if d == 0 else hb_ref)
        return tuple(out)

    init = []
    for hh in range(hp):
        for d in range(N_DIR):
            if has_init:
                n_col = jnp.sum(eye_dh.astype(F32) * n0_ref[d, hh], axis=1, keepdims=True)
                init += [jnp.concatenate([c0_ref[d, hh], jnp.broadcast_to(n_col, (DH, DH))], axis=1),
                         m0_ref[d, hh][:, 0:1]]
            else:
                init += [jnp.zeros((DH, 2 * DH), F32), jnp.zeros((1, 1), F32)]
    final = lax.fori_loop(0, nc, body, tuple(init), unroll=unroll)

    hm_ref[...] = (hf_ref[...] + hb_ref[...]).astype(BF16)
    if emit_state:
        for layer in range(n_prev):
            cs_ref[layer] = pc_ref[layer]
            ns_ref[layer] = pn_ref[layer]
            ms_ref[layer] = pm_ref[layer]
        eye_f = eye_dh.astype(F32)
        for hh in range(hp):
            for d in range(N_DIR):
                Cn, m = final[2 * (hh * N_DIR + d):2 * (hh * N_DIR + d) + 2]
                cs_ref[n_prev, d, hh] = Cn[:, :DH]
                ns_ref[n_prev, d, hh] = jnp.sum(eye_f * Cn[:, DH:], axis=0, keepdims=True)
                ms_ref[n_prev, d, hh] = jnp.broadcast_to(m, (1, LANES))


def _mlstm(qkvo, grows, init, state_out, hp, unroll):
    b, l, _ = qkvo.shape
    emit_state = state_out is not None
    t = grows.shape[-1]
    nc = l // t
    seq_group = b // grows.shape[0]
    assert l % t == 0 and t % DH == 0 and HEADS % hp == 0 and nc % unroll == 0
    assert grows.shape[2] == seq_group * nc * N_GATE_COLS
    has_init = init is not None
    groups = HEADS // hp
    head_cols = lambda k: pl.BlockSpec((None, l, hp * DH), lambda i, h: (i, 0, k * groups + h))
    in_specs = [
        head_cols(0), head_cols(1), head_cols(2),
        pl.BlockSpec((None,) + grows.shape[1:], lambda i, h: (i // seq_group, 0, 0, 0))]
    args = [qkvo, qkvo, qkvo, grows]
    if has_init:
        c0, n0, m0, layer = init
        in_specs += [
            pl.BlockSpec((None, None, N_DIR, hp, DH, DH), lambda i, h: (i, layer, 0, h, 0, 0)),
            pl.BlockSpec((None, None, N_DIR, hp, 1, DH), lambda i, h: (i, layer, 0, h, 0, 0)),
            pl.BlockSpec((None, None, N_DIR, hp, 1, LANES), lambda i, h: (i, layer, 0, h, 0, 0))]
        args += [c0, n0, m0]
    out_specs = [pl.BlockSpec((None, l, hp * DH), lambda i, h: (i, 0, h))]
    out_shape = [jax.ShapeDtypeStruct((b, l, MLSTM_WIDTH), BF16)]
    n_prev = 0
    if emit_state:
        n_prev, prev = state_out
        slabs = lambda n, w: pl.BlockSpec((None, n, N_DIR, hp, w, DH), lambda i, h: (i, 0, 0, h, 0, 0))
        out_specs += [slabs(n_prev + 1, DH), slabs(n_prev + 1, 1), slabs(n_prev + 1, 1)]
        out_shape += [jax.ShapeDtypeStruct((b, n_prev + 1, N_DIR, HEADS, DH, DH), F32),
                      jax.ShapeDtypeStruct((b, n_prev + 1, N_DIR, HEADS, 1, DH), F32),
                      jax.ShapeDtypeStruct((b, n_prev + 1, N_DIR, HEADS, 1, DH), F32)]
        if n_prev:
            in_specs += [slabs(n_prev, DH), slabs(n_prev, 1), slabs(n_prev, 1)]
            args += list(prev)
    return pl.pallas_call(
        functools.partial(_mlstm_kernel, seq=l, chunk=t, hp=hp, seq_group=seq_group, has_init=has_init,
                          emit_state=emit_state, n_prev=n_prev, unroll=unroll),
        name="mlstm",
        grid=(b, groups),
        in_specs=in_specs, out_specs=out_specs, out_shape=out_shape,
        scratch_shapes=[
            pltpu.VMEM((l, hp * DH), F32),
            pltpu.VMEM((l, hp * DH), F32)],
        compiler_params=pltpu.CompilerParams(
            dimension_semantics=("arbitrary", "arbitrary"), vmem_limit_bytes=VMEM_LIMIT),
    )(*args)


def _merge_ffn_kernel(*refs, tm, add_pos):
    it = iter(refs)
    x_ref = next(it)
    pos_ref = next(it) if add_pos else None
    (mod_ref, ng_ref, yf_ref, up_ref, hm_ref, og_ref, mg_ref, band_ref, icnt_ref, wpool_ref, pscale_ref,
     wpf_ref, wpp_ref, wpm_ref, wout_ref, wfi_ref, wfo_ref, o_ref) = (next(it) for _ in range(18))
    x = x_ref[...]
    if add_pos:
        x = x + pos_ref[...]

    blocks = [slice(r * POOL_BLOCK, (r + 1) * POOL_BLOCK) for r in range(tm // POOL_BLOCK)]

    lane_group = lax.broadcasted_iota(jnp.int32, (POOL_BLOCK, POOL_WIDTH), 1) // POOL_GROUP_DIM
    pooled = []
    for rows in blocks:
        u = up_ref[rows, :]
        acc = jnp.zeros((POOL_BLOCK, POOL_WIDTH), F32)
        for g in range(len(POOL_WINDOWS)):
            acc = jnp.where(lane_group == g, _dot(band_ref[g], u), acc)
        pooled.append((acc * icnt_ref[...] - u.astype(F32)).astype(BF16))

    merged = []
    for rows, p in zip(blocks, pooled):
        pp_pre = _dot(p, wpool_ref[...])
        y_f = _dot(yf_ref[rows, :], wpf_ref[...])
        hg = (og_ref[rows, :].astype(F32) * hm_ref[rows, :].astype(F32)).astype(BF16)
        y_m = _dot(hg, wpm_ref[...])
        y_p = _dot((pp_pre * pscale_ref[...]).astype(BF16), wpp_ref[...])
        g_f = mg_ref[rows, 0:D_MODEL].astype(F32)
        g_p = mg_ref[rows, D_MODEL:2 * D_MODEL].astype(F32)
        g_m = mg_ref[rows, 2 * D_MODEL:3 * D_MODEL].astype(F32)
        merged.append((g_f * y_f + g_p * y_p + g_m * y_m).astype(BF16))

    x1_blocks, h2_blocks = [], []
    for rows, y in zip(blocks, merged):
        x1b = x[rows, :] + _rms(_dot(y, wout_ref[...]), ng_ref[1:2, :] * mod_ref[2:3, :])
        x1_blocks.append(x1b)
        h2_blocks.append((_rms(x1b, ng_ref[2:3, :] * (1.0 + mod_ref[4:5, :])) + mod_ref[3:4, :]).astype(BF16))
    h2 = h2_blocks[0] if len(blocks) == 1 else jnp.concatenate(h2_blocks, axis=0)
    acc = jnp.zeros((tm, D_MODEL), F32)
    for k, (c0, c1) in enumerate(FFN_CHUNKS):
        if k == 0:
            ab = [(_dot(hb, wfi_ref[:, c0:c1]), _dot(hb, wfi_ref[:, D_FF + c0:D_FF + c1])) for hb in h2_blocks]
            a = jnp.concatenate([t[0] for t in ab], axis=0)
            bb = jnp.concatenate([t[1] for t in ab], axis=0)
        else:
            a = _dot(h2, wfi_ref[:, c0:c1])
            bb = _dot(h2, wfi_ref[:, D_FF + c0:D_FF + c1])
        act = (a * _sigmoid(a) * bb).astype(BF16)
        if k < len(FFN_CHUNKS) - 1:
            acc = acc + _dot(act, wfo_ref[c0:c1, :])
        else:
            for rows, x1b in zip(blocks, x1_blocks):
                yb = acc[rows, :] + _dot(act[rows, :], wfo_ref[c0:c1, :])
                o_ref[rows, :] = x1b + _rms(yb, ng_ref[3:4, :] * mod_ref[5:6, :])


def _merge_ffn(x, mod, norm_g, yf, up, hm, qkvo, mg, band, icnt, wpool, pscale, lw, layer, tm, pos=None):
    b, l, _ = x.shape
    add_pos = pos is not None
    ogate = pl.BlockSpec((None, tm, MLSTM_WIDTH), lambda i, j: (i, j, 3))
    per_batch_mod = mod.shape[0] > 1
    tok = lambda w: pl.BlockSpec((None, tm, w), lambda i, j: (i, j, 0))
    full = lambda a: pl.BlockSpec(a.shape, lambda i, j: (0,) * a.ndim)
    stacked = [lw[k] for k in ("wpf", "wpp", "wpm", "wout", "wfi", "wfo")]
    in_specs = [tok(D_MODEL)]
    args = [x]
    if add_pos:
        in_specs.append(pl.BlockSpec((tm, D_MODEL), lambda i, j: (j, 0)))
        args.append(pos)
    in_specs += [
        pl.BlockSpec((None, 6, D_MODEL), (lambda i, j: (i, 0, 0)) if per_batch_mod else (lambda i, j: (0, 0, 0))),
        full(norm_g), tok(FOURIER_WIDTH), tok(POOL_WIDTH), tok(MLSTM_WIDTH), ogate, tok(3 * D_MODEL),
        full(band), full(icnt), full(wpool), full(pscale)]
    in_specs += [_layer_spec(a, layer) for a in stacked]
    args += [mod, norm_g, yf, up, hm, qkvo, mg, band, icnt, wpool, pscale] + stacked
    return pl.pallas_call(
        functools.partial(_merge_ffn_kernel, tm=tm, add_pos=add_pos),
        name="merge_ffn",
        grid=(b, l // tm),
        in_specs=in_specs,
        out_specs=tok(D_MODEL),
        out_shape=jax.ShapeDtypeStruct((b, l, D_MODEL), F32),
        compiler_params=pltpu.CompilerParams(
            dimension_semantics=("arbitrary", "arbitrary"), vmem_limit_bytes=VMEM_LIMIT),
    )(*args)


def _repack_kernel(w_ref, o_ref):
    o_ref[:, 0:CAT_M] = w_ref[:, 0:OFF_G]
    o_ref[:, CAT_M:CAT_G] = w_ref[:, OFF_M:D_IN]
    tail = w_ref[:, OFF_G:OFF_G + LANES].astype(F32)
    lane = lax.broadcasted_iota(jnp.int32, tail.shape, 1)
    o_ref[:, CAT_G:CAT_W] = jnp.where(lane < N_GATE_COLS, tail, 0.0).astype(BF16)


def _repack_w_in(w_in):
    wp = jnp.pad(w_in.astype(BF16), ((0, 0), (0, 0), (0, CAT_W - D_IN)))
    tk = 256
    blk = pl.BlockSpec((None, tk, CAT_W), lambda l, k: (l, k, 0))
    return pl.pallas_call(
        _repack_kernel,
        name="repack_w_in",
        grid=(DEPTH, D_MODEL // tk),
        in_specs=[blk],
        out_specs=blk,
        out_shape=jax.ShapeDtypeStruct((DEPTH, D_MODEL, CAT_W), BF16),
        compiler_params=pltpu.CompilerParams(
            dimension_semantics=("arbitrary", "arbitrary"), vmem_limit_bytes=VMEM_LIMIT),
    )(wp)


def _stacked_weights(w_in, w_proj_f, w_proj_p, w_proj_m, w_out, w_ffn_in, w_ffn_out):
    return dict(
        w_cat=_repack_w_in(w_in),
        wpf=w_proj_f.astype(BF16), wpp=w_proj_p.astype(BF16), wpm=w_proj_m.astype(BF16),
        wout=w_out.astype(BF16), wfi=w_ffn_in.astype(BF16), wfo=w_ffn_out.astype(BF16))


def _block(x, mod, lw, layer, norm_g, b_gate, wpool, pscale, width, tm, fold, init, state_out, hp, unroll, pos=None):
    b, l, _ = x.shape
    folded = lambda a: a.reshape(b // fold, fold * l, a.shape[-1])
    unfolded = lambda a: a.reshape(b, l, a.shape[-1])
    uf, up, qkvo, grows, mg = _inproj(folded(x), mod, norm_g, lw["w_cat"], layer, b_gate, tm, min(MLSTM_CHUNK, l), pos)
    yf = _fourier(unfolded(uf))
    mouts = _mlstm(unfolded(qkvo), grows, init, state_out, hp, unroll)
    band, icnt = _pool_tables(width)
    x = _merge_ffn(folded(x), mod, norm_g, folded(yf), up, folded(mouts[0]), qkvo, mg,
                   jnp.asarray(band).astype(BF16), jnp.asarray(icnt), wpool, pscale, lw, layer, tm, pos)
    return unfolded(x), mouts[1:]


def kernel(x_prompt, x_sample, state_C, state_n, state_m, c, c_ctx, w_ada, b_ada, norm_g, w_in, b_gate, w_proj_f, w_pool, pool_scale, w_proj_p, w_proj_m, w_out, w_ffn_in, w_ffn_out):
    bp, lp, _ = x_prompt.shape
    bs, ls, _ = x_sample.shape
    cond = jnp.concatenate([c_ctx[None, :], c, jnp.zeros((COND_ROWS - 1 - bs, D_MODEL), F32)], axis=0)
    mod = _modulation(cond, w_ada, b_ada).reshape(DEPTH, COND_ROWS, 6, D_MODEL)
    lw = _stacked_weights(w_in, w_proj_f, w_proj_p, w_proj_m, w_out, w_ffn_in, w_ffn_out)
    group_eye = jnp.eye(len(POOL_WINDOWS), dtype=w_pool.dtype)[None, :, None, :, None]
    wpool = (w_pool[:, :, :, None, :] * group_eye).reshape(DEPTH, POOL_WIDTH, POOL_WIDTH).astype(BF16)
    layer_args = [(lw, i, norm_g[i], b_gate[i], wpool[i], pool_scale[i].reshape(1, POOL_WIDTH))
                  for i in range(DEPTH)]

    xp = x_prompt
    states = None
    for i in range(DEPTH):
        xp, states = _block(xp, mod[i, 0:1], *layer_args[i], lp, TOKEN_TILE, TOKEN_TILE // lp, None, (i, states),
                            HEADS, 1)
    new_c, new_n, new_m = states

    xs = x_sample
    pos = jnp.asarray(_pos_table(ls // GRID_W))
    n0 = state_n.reshape(bs, DEPTH, N_DIR, HEADS, 1, DH)
    m0 = jnp.broadcast_to(state_m[..., None, None], (bs, DEPTH, N_DIR, HEADS, 1, LANES))
    for i in range(DEPTH):
        init = (state_C, n0, m0, i)
        xs, _ = _block(xs, mod[i, 1:1 + bs], *layer_args[i], GRID_W, TOKEN_TILE, 1, init, None, HEADS, 1,
                       pos if i == 0 else None)

    return (xp, xs, new_c, new_n[:, :, :, :, 0, :], new_m[:, :, :, :, 0, 0])
```

```python
import functools

import numpy as np
import jax
import jax.numpy as jnp
from jax import lax
from jax.experimental import pallas as pl
from jax.experimental.pallas import tpu as pltpu

F32 = jnp.float32
BF16 = jnp.bfloat16

D_MODEL = 1024
DEPTH = 2
GRID_W = 64
FOURIER_GROUPS = 4
FOURIER_WIDTH = D_MODEL // 4
FOURIER_GROUP_DIM = FOURIER_WIDTH // FOURIER_GROUPS
POOL_WINDOWS = (2, 4, 8, 16)
POOL_WIDTH = D_MODEL // 4
POOL_GROUP_DIM = POOL_WIDTH // len(POOL_WINDOWS)
HEADS = 4
MLSTM_WIDTH = D_MODEL // 2
DH = MLSTM_WIDTH // HEADS
N_DIR = 2
N_GATE_COLS = N_DIR * 2 * HEADS
D_FF = -(-8 * D_MODEL // (3 * 256)) * 256
RMS_EPS = 1e-6
POS_BASE = 10000.0

OFF_F = 0
OFF_P = OFF_F + FOURIER_WIDTH
OFF_QKVO = OFF_P + POOL_WIDTH
OFF_G = OFF_QKVO + 4 * MLSTM_WIDTH
OFF_M = OFF_G + N_GATE_COLS
D_IN = OFF_M + 3 * D_MODEL

LANES = 128
MLSTM_CHUNK = 256
POOL_BLOCK = 256
TOKEN_TILE = 512
COND_ROWS = 16
VMEM_LIMIT = 60 * 1024 * 1024
MXU_COLS = 256
FFN_CHUNKS = ((0, 5 * MXU_COLS), (5 * MXU_COLS, D_FF))
CAT_M = OFF_G
CAT_G = CAT_M + 3 * D_MODEL
CAT_W = CAT_G + LANES


def _dot(a, b):
    return jnp.dot(a, b, preferred_element_type=F32)


def _rms(x, g):
    return x * lax.rsqrt(jnp.mean(x * x, axis=-1, keepdims=True) + RMS_EPS) * g


def _log_sigmoid(x):
    return jnp.minimum(x, 0.0) - jnp.log(1.0 + jnp.exp(-jnp.abs(x)))


def _sigmoid(x):
    return 0.5 * jnp.tanh(0.5 * x) + 0.5


@functools.lru_cache(maxsize=None)
def _dft_tables(n):
    idx = np.arange(n, dtype=np.int64)
    ang = 2.0 * np.pi * ((idx[:, None] * idx[None, :]) % n).astype(np.float64) / n
    return np.cos(ang).astype(np.float32), np.sin(ang).astype(np.float32)


@functools.lru_cache(maxsize=None)
def _dft_half_tables(n):
    f = np.arange(n // 2, dtype=np.int64)[:, None]
    j = np.arange(n // 2, dtype=np.int64)[None, :]
    out = []
    for l in (2 * j, 2 * j + 1):
        ang = 2.0 * np.pi * ((f * l) % n).astype(np.float64) / n
        out += [np.cos(ang).astype(np.float32), np.sin(ang).astype(np.float32)]
    return tuple(out)


@functools.lru_cache(maxsize=None)
def _group_dft_tables():
    c, s = _dft_tables(FOURIER_GROUP_DIM)
    bc = np.zeros((FOURIER_WIDTH, FOURIER_WIDTH), np.float32)
    bs = np.zeros((FOURIER_WIDTH, FOURIER_WIDTH), np.float32)
    for g in range(FOURIER_GROUPS):
        sl = slice(g * FOURIER_GROUP_DIM, (g + 1) * FOURIER_GROUP_DIM)
        bc[sl, sl] = c
        bs[sl, sl] = s
    return bc, bs


@functools.lru_cache(maxsize=None)
def _pool_tables(width):
    band = np.zeros((len(POOL_WINDOWS), POOL_BLOCK, POOL_BLOCK), np.float32)
    inv_cnt = np.zeros((POOL_BLOCK, POOL_WIDTH), np.float32)
    for g, w in enumerate(POOL_WINDOWS):
        left = w // 2
        right = w - 1 - left
        for t in range(POOL_BLOCK):
            row, pos = divmod(t, width)
            lo = min(max(pos - left, 0), width - 1)
            hi = min(max(pos + right, 0), width - 1)
            band[g, t, row * width + lo:row * width + hi + 1] = 1.0
            inv_cnt[t, g * POOL_GROUP_DIM:(g + 1) * POOL_GROUP_DIM] = 1.0 / (hi - lo + 1)
    return band, inv_cnt


@functools.lru_cache(maxsize=None)
def _pos_table(rows):
    quarter = D_MODEL // 4
    omega = 1.0 / (POS_BASE ** (np.arange(quarter, dtype=np.float64) / quarter))
    t = np.arange(rows * GRID_W)
    r = (t // GRID_W).astype(np.float64)
    col = (t % GRID_W).astype(np.float64)
    ar = r[:, None] * omega[None, :]
    ac = col[:, None] * omega[None, :]
    return np.concatenate([np.sin(ar), np.cos(ar), np.sin(ac), np.cos(ac)], axis=-1).astype(np.float32)


def _mod_kernel(c_ref, w_ref, b_ref, o_ref):
    @pl.when(pl.program_id(1) == 0)
    def _():
        o_ref[...] = jnp.broadcast_to(b_ref[...], o_ref.shape)

    c = c_ref[...]
    a = (c * jax.nn.sigmoid(c)).astype(BF16)
    o_ref[...] += _dot(a, w_ref[...].astype(BF16))


def _modulation(cond, w_ada, b_ada):
    tk = 256
    n = 6 * D_MODEL
    return pl.pallas_call(
        _mod_kernel,
        name="adaln_mod",
        grid=(DEPTH, D_MODEL // tk),
        in_specs=[
            pl.BlockSpec((COND_ROWS, tk), lambda l, k: (0, k)),
            pl.BlockSpec((None, tk, n), lambda l, k: (l, k, 0)),
            pl.BlockSpec((None, 1, n), lambda l, k: (l, 0, 0)),
        ],
        out_specs=pl.BlockSpec((None, COND_ROWS, n), lambda l, k: (l, 0, 0)),
        out_shape=jax.ShapeDtypeStruct((DEPTH, COND_ROWS, 6 * D_MODEL), F32),
        compiler_params=pltpu.CompilerParams(
            dimension_semantics=("arbitrary", "arbitrary"), vmem_limit_bytes=VMEM_LIMIT),
    )(cond, w_ada, b_ada.reshape(DEPTH, 1, 6 * D_MODEL))


def _gate_rows(g, bias_col, rows_ref, chunk):
    x = g.T[:N_GATE_COLS, :] + bias_col
    row = lax.broadcasted_iota(jnp.int32, x.shape, 0)
    fwd = row < 2 * HEADS
    lf = _log_sigmoid(x)
    b_f = jnp.where(fwd, _scan_lanes(lf, False, jnp.add, 0.0), _scan_lanes(lf, True, jnp.add, 0.0))
    b_i = pltpu.roll(b_f, N_GATE_COLS - HEADS, 0)
    a = x - b_i
    p = jnp.where(fwd, _scan_lanes(a, False, jnp.maximum, -jnp.inf), _scan_lanes(a, True, jnp.maximum, -jnp.inf))
    sl = slice(chunk * N_GATE_COLS, (chunk + 1) * N_GATE_COLS)
    rows_ref[0, sl, :] = a
    rows_ref[1, sl, :] = b_i
    rows_ref[2, sl, :] = b_i + p
    rows_ref[3, sl, :] = p


def _inproj_kernel(*refs, add_pos, chunk):
    it = iter(refs)
    x_ref = next(it)
    pos_ref = next(it) if add_pos else None
    mod_ref, ng_ref, w_ref, bg_ref = (next(it) for _ in range(4))
    uf_ref, up_ref, qkvo_ref, rows_ref, mg_ref = (next(it) for _ in range(5))

    x = x_ref[...]
    if add_pos:
        x = x + pos_ref[...]
    blocks = [slice(r * chunk, (r + 1) * chunk) for r in range(x.shape[0] // chunk)]
    gain = ng_ref[0:1, :] * (1.0 + mod_ref[1:2, :])
    ck = 512
    h_blocks = []
    for rows in blocks:
        hb = (_rms(x[rows, :], gain) + mod_ref[0:1, :]).astype(BF16)
        za = _dot(hb, w_ref[:, 0:ck])
        uf_ref[rows, :] = za[:, :FOURIER_WIDTH].astype(BF16)
        up_ref[rows, :] = za[:, FOURIER_WIDTH:].astype(BF16)
        h_blocks.append(hb)
    h = h_blocks[0] if len(blocks) == 1 else jnp.concatenate(h_blocks, axis=0)
    for j in range(4 * MLSTM_WIDTH // ck):
        z = _dot(h, w_ref[:, (j + 1) * ck:(j + 2) * ck])
        if j * ck >= 3 * MLSTM_WIDTH:
            z = _sigmoid(z)
        qkvo_ref[:, j * ck:(j + 1) * ck] = z.astype(BF16)
    g = _dot(h, w_ref[:, CAT_G:CAT_W])
    for c in range(g.shape[0] // chunk):
        _gate_rows(g[c * chunk:(c + 1) * chunk, :], bg_ref[...], rows_ref, c)
    n_mg = 3 * D_MODEL // ck
    for j in range(n_mg):
        cols = slice(j * ck, (j + 1) * ck)
        wcols = slice(CAT_M + j * ck, CAT_M + (j + 1) * ck)
        if j < n_mg - 1:
            mg_ref[:, cols] = _sigmoid(_dot(h, w_ref[:, wcols])).astype(BF16)
        else:
            for rows, hb in zip(blocks, h_blocks):
                mg_ref[rows, cols] = _sigmoid(_dot(hb, w_ref[:, wcols])).astype(BF16)


def _layer_spec(a, layer):
    return pl.BlockSpec((None,) + a.shape[1:], lambda *_: (layer,) + (0,) * (a.ndim - 1),
                        pipeline_mode=pl.Buffered(1))


def _inproj(x, mod, norm_g, w_cat, layer, b_gate, tm, t, pos=None):
    b, l, _ = x.shape
    add_pos = pos is not None
    assert tm % t == 0 and l % tm == 0
    bias_col = b_gate.reshape(N_GATE_COLS, 1)
    per_batch_mod = mod.shape[0] > 1
    tok = lambda w: pl.BlockSpec((None, tm, w), lambda i, j: (i, j, 0))
    full = lambda a: pl.BlockSpec(a.shape, lambda i, j: (0,) * a.ndim)
    in_specs = [tok(D_MODEL)]
    args = [x]
    if add_pos:
        in_specs.append(pl.BlockSpec((tm, D_MODEL), lambda i, j: (j, 0)))
        args.append(pos)
    in_specs += [
        pl.BlockSpec((None, 6, D_MODEL), (lambda i, j: (i, 0, 0)) if per_batch_mod else (lambda i, j: (0, 0, 0))),
        full(norm_g), _layer_spec(w_cat, layer), full(bias_col)]
    args += [mod, norm_g, w_cat, bias_col]
    rows_per_tile = tm // t * N_GATE_COLS
    out_specs = [tok(FOURIER_WIDTH), tok(POOL_WIDTH), tok(4 * MLSTM_WIDTH),
                 pl.BlockSpec((None, 4, rows_per_tile, t), lambda i, j: (i, 0, j, 0)), tok(3 * D_MODEL)]
    out_shape = [
        jax.ShapeDtypeStruct((b, l, FOURIER_WIDTH), BF16),
        jax.ShapeDtypeStruct((b, l, POOL_WIDTH), BF16),
        jax.ShapeDtypeStruct((b, l, 4 * MLSTM_WIDTH), BF16),
        jax.ShapeDtypeStruct((b, 4, l // t * N_GATE_COLS, t), F32),
        jax.ShapeDtypeStruct((b, l, 3 * D_MODEL), BF16)]
    return pl.pallas_call(
        functools.partial(_inproj_kernel, add_pos=add_pos, chunk=t),
        name="inproj",
        grid=(b, l // tm),
        in_specs=in_specs, out_specs=out_specs, out_shape=out_shape,
        compiler_params=pltpu.CompilerParams(
            dimension_semantics=("arbitrary", "arbitrary"), vmem_limit_bytes=VMEM_LIMIT),
    )(*args)


def _fourier_kernel(u_ref, ce_ref, se_ref, co_ref, so_ref, bc_ref, bs_ref, o_ref, vc_ref, vs_ref, *, scale, half):
    u = u_ref[...]
    nblk = u.shape[1] // LANES
    for ref, tab in ((vc_ref, bc_ref), (vs_ref, bs_ref)):
        v = _dot(u, tab[...])
        for k in range(nblk):
            ref[k] = v[:, k * LANES:(k + 1) * LANES]

    def rows(ref, start):
        return jnp.concatenate([ref[k, pl.ds(start, half, stride=2), :] for k in range(nblk)], axis=1).astype(BF16)

    e = _dot(ce_ref[...], rows(vc_ref, 0)) - _dot(se_ref[...], rows(vs_ref, 0))
    o = _dot(co_ref[...], rows(vc_ref, 1)) - _dot(so_ref[...], rows(vs_ref, 1))
    o_ref[0:half, :] = ((e + o) * scale).astype(BF16)
    o_ref[half:, :] = ((e - o) * scale).astype(BF16)


def _fourier(uf):
    b, l, w = uf.shape
    half = l // 2
    bc, bs = (jnp.asarray(t).astype(BF16) for t in _group_dft_tables())
    tables = [jnp.asarray(t).astype(BF16) for t in _dft_half_tables(l)]
    full = lambda a: pl.BlockSpec(a.shape, lambda i: (0,) * a.ndim)
    seq = pl.BlockSpec((None, l, w), lambda i: (i, 0, 0))
    return pl.pallas_call(
        functools.partial(_fourier_kernel, scale=float((l * FOURIER_GROUP_DIM) ** -0.5), half=half),
        name="fourier",
        grid=(b,),
        in_specs=[seq] + [full(t) for t in tables] + [full(bc), full(bs)],
        out_specs=seq,
        out_shape=jax.ShapeDtypeStruct((b, l, w), BF16),
        scratch_shapes=[pltpu.VMEM((w // LANES, l, LANES), F32), pltpu.VMEM((w // LANES, l, LANES), F32)],
        compiler_params=pltpu.CompilerParams(
            dimension_semantics=("arbitrary",), vmem_limit_bytes=VMEM_LIMIT),
    )(uf, *tables, bc, bs)


def _scan_lanes(x, reverse, op, fill):
    n = x.shape[-1]
    lane = lax.broadcasted_iota(jnp.int32, x.shape, x.ndim - 1)
    s = 1
    while s < n:
        if reverse:
            x = op(x, jnp.where(lane < n - s, pltpu.roll(x, n - s, x.ndim - 1), fill))
        else:
            x = op(x, jnp.where(lane >= s, pltpu.roll(x, s, x.ndim - 1), fill))
        s *= 2
    return x


SPLIT_ROWS = 16
CHAIN_GROUP = 2


def _mlstm_kernel(*refs, seq, chunk, hp, seq_group, has_init, emit_state, n_prev, unroll):
    it = iter(refs)
    q_ref, k_ref, v_ref, rows_ref = (next(it) for _ in range(4))
    c0_ref, n0_ref, m0_ref = (next(it) for _ in range(3)) if has_init else (None, None, None)
    pc_ref, pn_ref, pm_ref = (next(it) for _ in range(3)) if n_prev else (None, None, None)
    hm_ref = next(it)
    cs_ref, ns_ref, ms_ref = (next(it) for _ in range(3)) if emit_state else (None, None, None)
    hf_ref, hb_ref = (next(it) for _ in range(2))

    t = chunk
    nc = seq // t
    rep = t // DH
    scale = DH ** -0.5

    row_i = lax.broadcasted_iota(jnp.int32, (t, t), 0)
    col_i = lax.broadcasted_iota(jnp.int32, (t, t), 1)
    eye_dh = jnp.where(lax.broadcasted_iota(jnp.int32, (DH, DH), 0) == lax.broadcasted_iota(jnp.int32, (DH, DH), 1),
                       1.0, 0.0).astype(BF16)
    sub = lax.broadcasted_iota(jnp.int32, (SPLIT_ROWS, t), 0)
    part = sub % 3
    p_sub = lax.broadcasted_iota(jnp.int32, (SPLIT_ROWS, 2 * DH), 0)
    p_lane = lax.broadcasted_iota(jnp.int32, (SPLIT_ROWS, 2 * DH), 1)
    gather_mat = jnp.where(((p_sub < 3) & (p_lane < DH)) | ((p_sub >= 3) & (p_sub < 6) & (p_lane >= DH)),
                           1.0, 0.0).astype(BF16)
    ones_cols = jnp.ones((t, DH), BF16)

    def stage1(c, hh, direction, Cn):
        rows = pl.ds(pl.multiple_of(c * t, t), t)
        cols = slice(hh * DH, (hh + 1) * DH)
        chunk_row = (pl.program_id(0) % seq_group) * nc + c
        r = pl.ds(chunk_row * N_GATE_COLS + direction * 2 * HEADS + pl.program_id(1) * hp + hh, 1)
        a_r = rows_ref[0, r, :]
        b_r = rows_ref[1, r, :]
        g_r = rows_ref[2, r, :]
        p_r = rows_ref[3, r, :]
        if direction == 0:
            end = t - 1
            mask = col_i <= row_i
        else:
            end = 0
            mask = col_i >= row_i
        b_end = b_r[:, end:end + 1]
        p_end = p_r[:, end:end + 1]

        x0 = jnp.where(sub < 3, b_r, jnp.where(sub < 6, g_r, 0.0))
        x1 = x0 - x0.astype(BF16).astype(F32)
        x2 = x1 - x1.astype(BF16).astype(F32)
        xs = jnp.where(part == 0, x0, jnp.where(part == 1, x1, x2)).astype(BF16)
        bg = lax.dot_general(xs, gather_mat, (((0,), (0,)), ((), ())), preferred_element_type=F32)
        qc = q_ref[rows, cols]
        kc = k_ref[rows, cols]
        v1 = jnp.concatenate([v_ref[rows, cols], ones_cols], axis=1)
        qk = lax.dot_general(qc, kc, (((1,), (1,)), ((), ())), preferred_element_type=F32)
        qs = _dot(qc, Cn.astype(BF16))
        kt = lax.dot_general(eye_dh, kc, (((1,), (1,)), ((), ())), preferred_element_type=F32)
        return rows, cols, mask, a_r, b_end, p_end, bg, v1, qk, qs, kt

    def stage2(st, Cn, m, h_ref):
        rows, cols, mask, a_r, b_end, p_end, bg, v1, qk, qs, kt = st
        b_c = bg[:, :DH]
        m_t = jnp.maximum(b_c + m, bg[:, DH:])
        mu = m_t - b_c
        sdec = jnp.exp(m - mu)
        mu_t = jnp.concatenate([mu] * rep, axis=1)
        d = jnp.where(mask, jnp.exp((a_r + np.float32(np.log(scale))) - mu_t), 0.0)
        sv = _dot((qk * d).astype(BF16), v1)
        num = sdec * qs[:, :DH] + sv[:, :DH]
        den = sdec * qs[:, DH:] + sv[:, DH:]
        h_ref[rows, cols] = num * (1.0 / jnp.maximum(jnp.abs(den), jnp.exp(-m_t)))

        mx = jnp.maximum(m, p_end)
        w_r = jnp.exp(a_r - mx) * scale
        Cn_new = jnp.exp(m - mx) * Cn + _dot((kt * w_r).astype(BF16), v1)
        return Cn_new, b_end + mx

    def body(ci, carry):
        chains = [(hh, d) for hh in range(hp) for d in range(N_DIR)]
        chunk_of = lambda d: ci if d == 0 else nc - 1 - ci
        out = []
        for g0 in range(0, len(chains), CHAIN_GROUP):
            group = list(enumerate(chains))[g0:g0 + CHAIN_GROUP]
            firsts = [stage1(chunk_of(d), hh, d, carry[2 * k]) for k, (hh, d) in group]
            for st, (k, (hh, d)) in zip(firsts, group):
                out += stage2(st, carry[2 * k], carry[2 * k + 1], hf_ref if d == 0 else hb_ref)
        return tuple(out)

    init = []
    for hh in range(hp):
        for d in range(N_DIR):
            if has_init:
                n_col = jnp.sum(eye_dh.astype(F32) * n0_ref[d, hh], axis=1, keepdims=True)
                init += [jnp.concatenate([c0_ref[d, hh], jnp.broadcast_to(n_col, (DH, DH))], axis=1),
                         m0_ref[d, hh][:, 0:1]]
            else:
                init += [jnp.zeros((DH, 2 * DH), F32), jnp.zeros((1, 1), F32)]
    final = lax.fori_loop(0, nc, body, tuple(init), unroll=unroll)

    hm_ref[...] = (hf_ref[...] + hb_ref[...]).astype(BF16)
    if emit_state:
        for layer in range(n_prev):
            cs_ref[layer] = pc_ref[layer]
            ns_ref[layer] = pn_ref[layer]
            ms_ref[layer] = pm_ref[layer]
        eye_f = eye_dh.astype(F32)
        for hh in range(hp):
            for d in range(N_DIR):
                Cn, m = final[2 * (hh * N_DIR + d):2 * (hh * N_DIR + d) + 2]
                cs_ref[n_prev, d, hh] = Cn[:, :DH]
                ns_ref[n_prev, d, hh] = jnp.sum(eye_f * Cn[:, DH:], axis=0, keepdims=True)
                ms_ref[n_prev, d, hh] = jnp.broadcast_to(m, (1, LANES))


def _mlstm(qkvo, grows, init, state_out, hp, unroll):
    b, l, _ = qkvo.shape
    emit_state = state_out is not None
    t = grows.shape[-1]
    nc = l // t
    seq_group = b // grows.shape[0]
    assert l % t == 0 and t % DH == 0 and HEADS % hp == 0 and nc % unroll == 0
    assert grows.shape[2] == seq_group * nc * N_GATE_COLS
    has_init = init is not None
    groups = HEADS // hp
    head_cols = lambda k: pl.BlockSpec((None, l, hp * DH), lambda i, h: (i, 0, k * groups + h))
    in_specs = [
        head_cols(0), head_cols(1), head_cols(2),
        pl.BlockSpec((None,) + grows.shape[1:], lambda i, h: (i // seq_group, 0, 0, 0))]
    args = [qkvo, qkvo, qkvo, grows]
    if has_init:
        c0, n0, m0, layer = init
        in_specs += [
            pl.BlockSpec((None, None, N_DIR, hp, DH, DH), lambda i, h: (i, layer, 0, h, 0, 0)),
            pl.BlockSpec((None, None, N_DIR, hp, 1, DH), lambda i, h: (i, layer, 0, h, 0, 0)),
            pl.BlockSpec((None, None, N_DIR, hp, 1, LANES), lambda i, h: (i, layer, 0, h, 0, 0))]
        args += [c0, n0, m0]
    out_specs = [pl.BlockSpec((None, l, hp * DH), lambda i, h: (i, 0, h))]
    out_shape = [jax.ShapeDtypeStruct((b, l, MLSTM_WIDTH), BF16)]
    n_prev = 0
    if emit_state:
        n_prev, prev = state_out
        slabs = lambda n, w: pl.BlockSpec((None, n, N_DIR, hp, w, DH), lambda i, h: (i, 0, 0, h, 0, 0))
        out_specs += [slabs(n_prev + 1, DH), slabs(n_prev + 1, 1), slabs(n_prev + 1, 1)]
        out_shape += [jax.ShapeDtypeStruct((b, n_prev + 1, N_DIR, HEADS, DH, DH), F32),
                      jax.ShapeDtypeStruct((b, n_prev + 1, N_DIR, HEADS, 1, DH), F32),
                      jax.ShapeDtypeStruct((b, n_prev + 1, N_DIR, HEADS, 1, DH), F32)]
        if n_prev:
            in_specs += [slabs(n_prev, DH), slabs(n_prev, 1), slabs(n_prev, 1)]
            args += list(prev)
    return pl.pallas_call(
        functools.partial(_mlstm_kernel, seq=l, chunk=t, hp=hp, seq_group=seq_group, has_init=has_init,
                          emit_state=emit_state, n_prev=n_prev, unroll=unroll),
        name="mlstm",
        grid=(b, groups),
        in_specs=in_specs, out_specs=out_specs, out_shape=out_shape,
        scratch_shapes=[
            pltpu.VMEM((l, hp * DH), F32),
            pltpu.VMEM((l, hp * DH), F32)],
        compiler_params=pltpu.CompilerParams(
            dimension_semantics=("arbitrary", "arbitrary"), vmem_limit_bytes=VMEM_LIMIT),
    )(*args)


def _merge_ffn_kernel(*refs, tm, add_pos):
    it = iter(refs)
    x_ref = next(it)
    pos_ref = next(it) if add_pos else None
    (mod_ref, ng_ref, yf_ref, up_ref, hm_ref, og_ref, mg_ref, band_ref, icnt_ref, wpool_ref, pscale_ref,
     wpf_ref, wpp_ref, wpm_ref, wout_ref, wfi_ref, wfo_ref, o_ref) = (next(it) for _ in range(18))
    x = x_ref[...]
    if add_pos:
        x = x + pos_ref[...]

    blocks = [slice(r * POOL_BLOCK, (r + 1) * POOL_BLOCK) for r in range(tm // POOL_BLOCK)]

    lane_group = lax.broadcasted_iota(jnp.int32, (POOL_BLOCK, POOL_WIDTH), 1) // POOL_GROUP_DIM
    pooled = []
    for rows in blocks:
        u = up_ref[rows, :]
        acc = jnp.zeros((POOL_BLOCK, POOL_WIDTH), F32)
        for g in range(len(POOL_WINDOWS)):
            acc = jnp.where(lane_group == g, _dot(band_ref[g], u), acc)
        pooled.append((acc * icnt_ref[...] - u.astype(F32)).astype(BF16))

    merged = []
    for rows, p in zip(blocks, pooled):
        pp_pre = _dot(p, wpool_ref[...])
        y_f = _dot(yf_ref[rows, :], wpf_ref[...])
        hg = (og_ref[rows, :].astype(F32) * hm_ref[rows, :].astype(F32)).astype(BF16)
        y_m = _dot(hg, wpm_ref[...])
        y_p = _dot((pp_pre * pscale_ref[...]).astype(BF16), wpp_ref[...])
        g_f = mg_ref[rows, 0:D_MODEL].astype(F32)
        g_p = mg_ref[rows, D_MODEL:2 * D_MODEL].astype(F32)
        g_m = mg_ref[rows, 2 * D_MODEL:3 * D_MODEL].astype(F32)
        merged.append((g_f * y_f + g_p * y_p + g_m * y_m).astype(BF16))

    x1_blocks, h2_blocks = [], []
    for rows, y in zip(blocks, merged):
        x1b = x[rows, :] + _rms(_dot(y, wout_ref[...]), ng_ref[1:2, :] * mod_ref[2:3, :])
        x1_blocks.append(x1b)
        h2_blocks.append((_rms(x1b, ng_ref[2:3, :] * (1.0 + mod_ref[4:5, :])) + mod_ref[3:4, :]).astype(BF16))
    h2 = h2_blocks[0] if len(blocks) == 1 else jnp.concatenate(h2_blocks, axis=0)
    acc = jnp.zeros((tm, D_MODEL), F32)
    for k, (c0, c1) in enumerate(FFN_CHUNKS):
        if k == 0:
            ab = [(_dot(hb, wfi_ref[:, c0:c1]), _dot(hb, wfi_ref[:, D_FF + c0:D_FF + c1])) for hb in h2_blocks]
            a = jnp.concatenate([t[0] for t in ab], axis=0)
            bb = jnp.concatenate([t[1] for t in ab], axis=0)
        else:
            a = _dot(h2, wfi_ref[:, c0:c1])
            bb = _dot(h2, wfi_ref[:, D_FF + c0:D_FF + c1])
        act = (a * _sigmoid(a) * bb).astype(BF16)
        if k < len(FFN_CHUNKS) - 1:
            acc = acc + _dot(act, wfo_ref[c0:c1, :])
        else:
            for rows, x1b in zip(blocks, x1_blocks):
                yb = acc[rows, :] + _dot(act[rows, :], wfo_ref[c0:c1, :])
                o_ref[rows, :] = x1b + _rms(yb, ng_ref[3:4, :] * mod_ref[5:6, :])


def _merge_ffn(x, mod, norm_g, yf, up, hm, qkvo, mg, band, icnt, wpool, pscale, lw, layer, tm, pos=None):
    b, l, _ = x.shape
    add_pos = pos is not None
    ogate = pl.BlockSpec((None, tm, MLSTM_WIDTH), lambda i, j: (i, j, 3))
    per_batch_mod = mod.shape[0] > 1
    tok = lambda w: pl.BlockSpec((None, tm, w), lambda i, j: (i, j, 0))
    full = lambda a: pl.BlockSpec(a.shape, lambda i, j: (0,) * a.ndim)
    stacked = [lw[k] for k in ("wpf", "wpp", "wpm", "wout", "wfi", "wfo")]
    in_specs = [tok(D_MODEL)]
    args = [x]
    if add_pos:
        in_specs.append(pl.BlockSpec((tm, D_MODEL), lambda i, j: (j, 0)))
        args.append(pos)
    in_specs += [
        pl.BlockSpec((None, 6, D_MODEL), (lambda i, j: (i, 0, 0)) if per_batch_mod else (lambda i, j: (0, 0, 0))),
        full(norm_g), tok(FOURIER_WIDTH), tok(POOL_WIDTH), tok(MLSTM_WIDTH), ogate, tok(3 * D_MODEL),
        full(band), full(icnt), full(wpool), full(pscale)]
    in_specs += [_layer_spec(a, layer) for a in stacked]
    args += [mod, norm_g, yf, up, hm, qkvo, mg, band, icnt, wpool, pscale] + stacked
    return pl.pallas_call(
        functools.partial(_merge_ffn_kernel, tm=tm, add_pos=add_pos),
        name="merge_ffn",
        grid=(b, l // tm),
        in_specs=in_specs,
        out_specs=tok(D_MODEL),
        out_shape=jax.ShapeDtypeStruct((b, l, D_MODEL), F32),
        compiler_params=pltpu.CompilerParams(
            dimension_semantics=("arbitrary", "arbitrary"), vmem_limit_bytes=VMEM_LIMIT),
    )(*args)


def _repack_kernel(w_ref, o_ref):
    o_ref[:, 0:CAT_M] = w_ref[:, 0:OFF_G]
    o_ref[:, CAT_M:CAT_G] = w_ref[:, OFF_M:D_IN]
    tail = w_ref[:, OFF_G:OFF_G + LANES].astype(F32)
    lane = lax.broadcasted_iota(jnp.int32, tail.shape, 1)
    o_ref[:, CAT_G:CAT_W] = jnp.where(lane < N_GATE_COLS, tail, 0.0).astype(BF16)


def _repack_w_in(w_in):
    wp = jnp.pad(w_in.astype(BF16), ((0, 0), (0, 0), (0, CAT_W - D_IN)))
    tk = 256
    blk = pl.BlockSpec((None, tk, CAT_W), lambda l, k: (l, k, 0))
    return pl.pallas_call(
        _repack_kernel,
        name="repack_w_in",
        grid=(DEPTH, D_MODEL // tk),
        in_specs=[blk],
        out_specs=blk,
        out_shape=jax.ShapeDtypeStruct((DEPTH, D_MODEL, CAT_W), BF16),
        compiler_params=pltpu.CompilerParams(
            dimension_semantics=("arbitrary", "arbitrary"), vmem_limit_bytes=VMEM_LIMIT),
    )(wp)


def _stacked_weights(w_in, w_proj_f, w_proj_p, w_proj_m, w_out, w_ffn_in, w_ffn_out):
    return dict(
        w_cat=_repack_w_in(w_in),
        wpf=w_proj_f.astype(BF16), wpp=w_proj_p.astype(BF16), wpm=w_proj_m.astype(BF16),
        wout=w_out.astype(BF16), wfi=w_ffn_in.astype(BF16), wfo=w_ffn_out.astype(BF16))


def _block(x, mod, lw, layer, norm_g, b_gate, wpool, pscale, width, tm, fold, init, state_out, hp, unroll, pos=None):
    b, l, _ = x.shape
    folded = lambda a: a.reshape(b // fold, fold * l, a.shape[-1])
    unfolded = lambda a: a.reshape(b, l, a.shape[-1])
    uf, up, qkvo, grows, mg = _inproj(folded(x), mod, norm_g, lw["w_cat"], layer, b_gate, tm, min(MLSTM_CHUNK, l), pos)
    yf = _fourier(unfolded(uf))
    mouts = _mlstm(unfolded(qkvo), grows, init, state_out, hp, unroll)
    band, icnt = _pool_tables(width)
    x = _merge_ffn(folded(x), mod, norm_g, folded(yf), up, folded(mouts[0]), qkvo, mg,
                   jnp.asarray(band).astype(BF16), jnp.asarray(icnt), wpool, pscale, lw, layer, tm, pos)
    return unfolded(x), mouts[1:]


def kernel(x_prompt, x_sample, state_C, state_n, state_m, c, c_ctx, w_ada, b_ada, norm_g, w_in, b_gate, w_proj_f, w_pool, pool_scale, w_proj_p, w_proj_m, w_out, w_ffn_in, w_ffn_out):
    bp, lp, _ = x_prompt.shape
    bs, ls, _ = x_sample.shape
    cond = jnp.concatenate([c_ctx[None, :], c, jnp.zeros((COND_ROWS - 1 - bs, D_MODEL), F32)], axis=0)
    mod = _modulation(cond, w_ada, b_ada).reshape(DEPTH, COND_ROWS, 6, D_MODEL)
    lw = _stacked_weights(w_in, w_proj_f, w_proj_p, w_proj_m, w_out, w_ffn_in, w_ffn_out)
    group_eye = jnp.eye(len(POOL_WINDOWS), dtype=w_pool.dtype)[None, :, None, :, None]
    wpool = (w_pool[:, :, :, None, :] * group_eye).reshape(DEPTH, POOL_WIDTH, POOL_WIDTH).astype(BF16)
    layer_args = [(lw, i, norm_g[i], b_gate[i], wpool[i], pool_scale[i].reshape(1, POOL_WIDTH))
                  for i in range(DEPTH)]

    xp = x_prompt
    states = None
    for i in range(DEPTH):
        xp, states = _block(xp, mod[i, 0:1], *layer_args[i], lp, TOKEN_TILE, TOKEN_TILE // lp, None, (i, states),
                            HEADS, 1)
    new_c, new_n, new_m = states

    xs = x_sample
    pos = jnp.asarray(_pos_table(ls // GRID_W))
    n0 = state_n.reshape(bs, DEPTH, N_DIR, HEADS, 1, DH)
    m0 = jnp.broadcast_to(state_m[..., None, None], (bs, DEPTH, N_DIR, HEADS, 1, LANES))
    for i in range(DEPTH):
        init = (state_C, n0, m0, i)
        xs, _ = _block(xs, mod[i, 1:1 + bs], *layer_args[i], GRID_W, TOKEN_TILE, 1, init, None, HEADS, 1,
                       pos if i == 0 else None)

    return (xp, xs, new_c, new_n[:, :, :, :, 0, :], new_m[:, :, :, :, 0, 0])
```

```python
import functools

import numpy as np
import jax
import jax.numpy as jnp
from jax import lax
from jax.experimental import pallas as pl
from jax.experimental.pallas import tpu as pltpu

F32 = jnp.float32
BF16 = jnp.bfloat16

D_MODEL = 1024
DEPTH = 2
GRID_W = 64
FOURIER_GROUPS = 4
FOURIER_WIDTH = D_MODEL // 4
FOURIER_GROUP_DIM = FOURIER_WIDTH // FOURIER_GROUPS
POOL_WINDOWS = (2, 4, 8, 16)
POOL_WIDTH = D_MODEL // 4
POOL_GROUP_DIM = POOL_WIDTH // len(POOL_WINDOWS)
HEADS = 4
MLSTM_WIDTH = D_MODEL // 2
DH = MLSTM_WIDTH // HEADS
N_DIR = 2
N_GATE_COLS = N_DIR * 2 * HEADS
D_FF = -(-8 * D_MODEL // (3 * 256)) * 256
RMS_EPS = 1e-6
POS_BASE = 10000.0

OFF_F = 0
OFF_P = OFF_F + FOURIER_WIDTH
OFF_QKVO = OFF_P + POOL_WIDTH
OFF_G = OFF_QKVO + 4 * MLSTM_WIDTH
OFF_M = OFF_G + N_GATE_COLS
D_IN = OFF_M + 3 * D_MODEL

LANES = 128
MLSTM_CHUNK = 256
POOL_BLOCK = 256
TOKEN_TILE = 512
INPROJ_TILE = 1024
PROJ_CHUNK = 512
WEIGHT_ROW_SLAB = 256
COND_ROWS = 16
VMEM_LIMIT = 60 * 1024 * 1024
MXU_COLS = 256
FFN_CHUNKS = ((0, 5 * MXU_COLS), (5 * MXU_COLS, D_FF))
CAT_M = OFF_G
CAT_G = CAT_M + 3 * D_MODEL
CAT_W = CAT_G + LANES


def _dot(a, b):
    return jnp.dot(a, b, preferred_element_type=F32)


def _rms(x, g):
    return x * lax.rsqrt(jnp.mean(x * x, axis=-1, keepdims=True) + RMS_EPS) * g


def _log_sigmoid(x):
    return jnp.minimum(x, 0.0) - jnp.log(1.0 + jnp.exp(-jnp.abs(x)))


def _sigmoid(x):
    return 0.5 * jnp.tanh(0.5 * x) + 0.5


@functools.lru_cache(maxsize=None)
def _dft_tables(n):
    idx = np.arange(n, dtype=np.int64)
    ang = 2.0 * np.pi * ((idx[:, None] * idx[None, :]) % n).astype(np.float64) / n
    return np.cos(ang).astype(np.float32), np.sin(ang).astype(np.float32)


@functools.lru_cache(maxsize=None)
def _dft_half_tables(n):
    f = np.arange(n // 2, dtype=np.int64)[:, None]
    j = np.arange(n // 2, dtype=np.int64)[None, :]
    out = []
    for l in (2 * j, 2 * j + 1):
        ang = 2.0 * np.pi * ((f * l) % n).astype(np.float64) / n
        out += [np.cos(ang).astype(np.float32), np.sin(ang).astype(np.float32)]
    return tuple(out)


@functools.lru_cache(maxsize=None)
def _group_dft_tables():
    c, s = _dft_tables(FOURIER_GROUP_DIM)
    bc = np.zeros((FOURIER_WIDTH, FOURIER_WIDTH), np.float32)
    bs = np.zeros((FOURIER_WIDTH, FOURIER_WIDTH), np.float32)
    for g in range(FOURIER_GROUPS):
        sl = slice(g * FOURIER_GROUP_DIM, (g + 1) * FOURIER_GROUP_DIM)
        bc[sl, sl] = c
        bs[sl, sl] = s
    return bc, bs


@functools.lru_cache(maxsize=None)
def _pool_tables(width):
    band = np.zeros((len(POOL_WINDOWS), POOL_BLOCK, POOL_BLOCK), np.float32)
    inv_cnt = np.zeros((POOL_BLOCK, POOL_WIDTH), np.float32)
    for g, w in enumerate(POOL_WINDOWS):
        left = w // 2
        right = w - 1 - left
        for t in range(POOL_BLOCK):
            row, pos = divmod(t, width)
            lo = min(max(pos - left, 0), width - 1)
            hi = min(max(pos + right, 0), width - 1)
            band[g, t, row * width + lo:row * width + hi + 1] = 1.0
            inv_cnt[t, g * POOL_GROUP_DIM:(g + 1) * POOL_GROUP_DIM] = 1.0 / (hi - lo + 1)
    return band, inv_cnt


@functools.lru_cache(maxsize=None)
def _pos_table(rows):
    quarter = D_MODEL // 4
    omega = 1.0 / (POS_BASE ** (np.arange(quarter, dtype=np.float64) / quarter))
    t = np.arange(rows * GRID_W)
    r = (t // GRID_W).astype(np.float64)
    col = (t % GRID_W).astype(np.float64)
    ar = r[:, None] * omega[None, :]
    ac = col[:, None] * omega[None, :]
    return np.concatenate([np.sin(ar), np.cos(ar), np.sin(ac), np.cos(ac)], axis=-1).astype(np.float32)


def _mod_kernel(c_ref, w_ref, b_ref, o_ref):
    @pl.when(pl.program_id(1) == 0)
    def _():
        o_ref[...] = jnp.broadcast_to(b_ref[...], o_ref.shape)

    c = c_ref[...]
    a = (c * jax.nn.sigmoid(c)).astype(BF16)
    o_ref[...] += _dot(a, w_ref[...].astype(BF16))


def _modulation(cond, w_ada, b_ada):
    tk = WEIGHT_ROW_SLAB
    n = 6 * D_MODEL
    return pl.pallas_call(
        _mod_kernel,
        name="adaln_mod",
        grid=(DEPTH, D_MODEL // tk),
        in_specs=[
            pl.BlockSpec((COND_ROWS, tk), lambda l, k: (0, k)),
            pl.BlockSpec((None, tk, n), lambda l, k: (l, k, 0)),
            pl.BlockSpec((None, 1, n), lambda l, k: (l, 0, 0)),
        ],
        out_specs=pl.BlockSpec((None, COND_ROWS, n), lambda l, k: (l, 0, 0)),
        out_shape=jax.ShapeDtypeStruct((DEPTH, COND_ROWS, 6 * D_MODEL), F32),
        compiler_params=pltpu.CompilerParams(
            dimension_semantics=("arbitrary", "arbitrary"), vmem_limit_bytes=VMEM_LIMIT),
    )(cond, w_ada, b_ada.reshape(DEPTH, 1, 6 * D_MODEL))


def _gate_rows(g, bias_col, rows_ref, chunk):
    x = g.T[:N_GATE_COLS, :] + bias_col
    row = lax.broadcasted_iota(jnp.int32, x.shape, 0)
    fwd = row < 2 * HEADS
    lf = _log_sigmoid(x)
    b_f = jnp.where(fwd, _scan_lanes(lf, False, jnp.add, 0.0), _scan_lanes(lf, True, jnp.add, 0.0))
    b_i = pltpu.roll(b_f, N_GATE_COLS - HEADS, 0)
    a = x - b_i
    p = jnp.where(fwd, _scan_lanes(a, False, jnp.maximum, -jnp.inf), _scan_lanes(a, True, jnp.maximum, -jnp.inf))
    sl = slice(chunk * N_GATE_COLS, (chunk + 1) * N_GATE_COLS)
    rows_ref[0, sl, :] = a
    rows_ref[1, sl, :] = b_i
    rows_ref[2, sl, :] = b_i + p
    rows_ref[3, sl, :] = p


def _inproj_kernel(*refs, add_pos, chunk):
    it = iter(refs)
    x_ref = next(it)
    pos_ref = next(it) if add_pos else None
    mod_ref, ng_ref, w_ref, bg_ref = (next(it) for _ in range(4))
    uf_ref, up_ref, qkvo_ref, rows_ref, mg_ref = (next(it) for _ in range(5))

    x = x_ref[...]
    if add_pos:
        x = x + pos_ref[...]
    blocks = [slice(r * chunk, (r + 1) * chunk) for r in range(x.shape[0] // chunk)]
    gain = ng_ref[0:1, :] * (1.0 + mod_ref[1:2, :])
    ck = PROJ_CHUNK
    h_blocks = []
    for rows in blocks:
        hb = (_rms(x[rows, :], gain) + mod_ref[0:1, :]).astype(BF16)
        za = _dot(hb, w_ref[:, 0:ck])
        uf_ref[rows, :] = za[:, :FOURIER_WIDTH].astype(BF16)
        up_ref[rows, :] = za[:, FOURIER_WIDTH:].astype(BF16)
        h_blocks.append(hb)
    h = h_blocks[0] if len(blocks) == 1 else jnp.concatenate(h_blocks, axis=0)
    for j in range(4 * MLSTM_WIDTH // ck):
        z = _dot(h, w_ref[:, (j + 1) * ck:(j + 2) * ck])
        if j * ck >= 3 * MLSTM_WIDTH:
            z = _sigmoid(z)
        qkvo_ref[:, j * ck:(j + 1) * ck] = z.astype(BF16)
    g = _dot(h, w_ref[:, CAT_G:CAT_W])
    for c in range(g.shape[0] // chunk):
        _gate_rows(g[c * chunk:(c + 1) * chunk, :], bg_ref[...], rows_ref, c)
    n_mg = 3 * D_MODEL // ck
    for j in range(n_mg):
        cols = slice(j * ck, (j + 1) * ck)
        wcols = slice(CAT_M + j * ck, CAT_M + (j + 1) * ck)
        if j < n_mg - 1:
            mg_ref[:, cols] = _sigmoid(_dot(h, w_ref[:, wcols])).astype(BF16)
        else:
            for rows, hb in zip(blocks, h_blocks):
                mg_ref[rows, cols] = _sigmoid(_dot(hb, w_ref[:, wcols])).astype(BF16)


def _layer_spec(a, layer):
    return pl.BlockSpec((None,) + a.shape[1:], lambda *_: (layer,) + (0,) * (a.ndim - 1),
                        pipeline_mode=pl.Buffered(1))


def _inproj(x, mod, norm_g, w_cat, layer, b_gate, tm, t, pos=None):
    b, l, _ = x.shape
    add_pos = pos is not None
    assert tm % t == 0 and l % tm == 0
    bias_col = b_gate.reshape(N_GATE_COLS, 1)
    per_batch_mod = mod.shape[0] > 1
    tok = lambda w: pl.BlockSpec((None, tm, w), lambda i, j: (i, j, 0))
    full = lambda a: pl.BlockSpec(a.shape, lambda i, j: (0,) * a.ndim)
    in_specs = [tok(D_MODEL)]
    args = [x]
    if add_pos:
        in_specs.append(pl.BlockSpec((tm, D_MODEL), lambda i, j: (j, 0)))
        args.append(pos)
    in_specs += [
        pl.BlockSpec((None, 6, D_MODEL), (lambda i, j: (i, 0, 0)) if per_batch_mod else (lambda i, j: (0, 0, 0))),
        full(norm_g), _layer_spec(w_cat, layer), full(bias_col)]
    args += [mod, norm_g, w_cat, bias_col]
    rows_per_tile = tm // t * N_GATE_COLS
    out_specs = [tok(FOURIER_WIDTH), tok(POOL_WIDTH), tok(4 * MLSTM_WIDTH),
                 pl.BlockSpec((None, 4, rows_per_tile, t), lambda i, j: (i, 0, j, 0)), tok(3 * D_MODEL)]
    out_shape = [
        jax.ShapeDtypeStruct((b, l, FOURIER_WIDTH), BF16),
        jax.ShapeDtypeStruct((b, l, POOL_WIDTH), BF16),
        jax.ShapeDtypeStruct((b, l, 4 * MLSTM_WIDTH), BF16),
        jax.ShapeDtypeStruct((b, 4, l // t * N_GATE_COLS, t), F32),
        jax.ShapeDtypeStruct((b, l, 3 * D_MODEL), BF16)]
    return pl.pallas_call(
        functools.partial(_inproj_kernel, add_pos=add_pos, chunk=t),
        name="inproj",
        grid=(b, l // tm),
        in_specs=in_specs, out_specs=out_specs, out_shape=out_shape,
        compiler_params=pltpu.CompilerParams(
            dimension_semantics=("arbitrary", "arbitrary"), vmem_limit_bytes=VMEM_LIMIT),
    )(*args)


def _fourier_kernel(u_ref, ce_ref, se_ref, co_ref, so_ref, bc_ref, bs_ref, o_ref, vc_ref, vs_ref, *, scale, half):
    nseq, l, w = u_ref.shape
    u = u_ref[...].reshape(nseq * l, w)
    nblk = w // LANES
    for ref, tab in ((vc_ref, bc_ref), (vs_ref, bs_ref)):
        v = _dot(u, tab[...])
        for k in range(nblk):
            ref[k] = v[:, k * LANES:(k + 1) * LANES]

    def rows(ref, start):
        return jnp.concatenate([ref[k, pl.ds(start, half, stride=2), :] for k in range(nblk)], axis=1).astype(BF16)

    for s in range(nseq):
        e = _dot(ce_ref[...], rows(vc_ref, s * l)) - _dot(se_ref[...], rows(vs_ref, s * l))
        o = _dot(co_ref[...], rows(vc_ref, s * l + 1)) - _dot(so_ref[...], rows(vs_ref, s * l + 1))
        o_ref[s, 0:half, :] = ((e + o) * scale).astype(BF16)
        o_ref[s, half:, :] = ((e - o) * scale).astype(BF16)


def _fourier(uf):
    b, l, w = uf.shape
    half = l // 2
    bc, bs = (jnp.asarray(t).astype(BF16) for t in _group_dft_tables())
    tables = [jnp.asarray(t).astype(BF16) for t in _dft_half_tables(l)]
    full = lambda a: pl.BlockSpec(a.shape, lambda i: (0,) * a.ndim)
    nseq = max(1, min(b, INPROJ_TILE // l))
    assert b % nseq == 0
    seq = pl.BlockSpec((nseq, l, w), lambda i: (i, 0, 0))
    return pl.pallas_call(
        functools.partial(_fourier_kernel, scale=float((l * FOURIER_GROUP_DIM) ** -0.5), half=half),
        name="fourier",
        grid=(b // nseq,),
        in_specs=[seq] + [full(t) for t in tables] + [full(bc), full(bs)],
        out_specs=seq,
        out_shape=jax.ShapeDtypeStruct((b, l, w), BF16),
        scratch_shapes=[pltpu.VMEM((w // LANES, nseq * l, LANES), F32),
                        pltpu.VMEM((w // LANES, nseq * l, LANES), F32)],
        compiler_params=pltpu.CompilerParams(
            dimension_semantics=("arbitrary",), vmem_limit_bytes=VMEM_LIMIT),
    )(uf, *tables, bc, bs)


def _scan_lanes(x, reverse, op, fill):
    n = x.shape[-1]
    lane = lax.broadcasted_iota(jnp.int32, x.shape, x.ndim - 1)
    s = 1
    while s < n:
        if reverse:
            x = op(x, jnp.where(lane < n - s, pltpu.roll(x, n - s, x.ndim - 1), fill))
        else:
            x = op(x, jnp.where(lane >= s, pltpu.roll(x, s, x.ndim - 1), fill))
        s *= 2
    return x


SPLIT_ROWS = 16
CHAIN_GROUP = 2


def _mlstm_kernel(*refs, seq, chunk, hp, seq_group, has_init, emit_state, n_prev, unroll):
    it = iter(refs)
    q_ref, k_ref, v_ref, rows_ref = (next(it) for _ in range(4))
    c0_ref, n0_ref, m0_ref = (next(it) for _ in range(3)) if has_init else (None, None, None)
    pc_ref, pn_ref, pm_ref = (next(it) for _ in range(3)) if n_prev else (None, None, None)
    hm_ref = next(it)
    cs_ref, ns_ref, ms_ref = (next(it) for _ in range(3)) if emit_state else (None, None, None)
    hf_ref, hb_ref = (next(it) for _ in range(2))

    t = chunk
    nc = seq // t
    rep = t // DH
    scale = DH ** -0.5

    row_i = lax.broadcasted_iota(jnp.int32, (t, t), 0)
    col_i = lax.broadcasted_iota(jnp.int32, (t, t), 1)
    eye_dh = jnp.where(lax.broadcasted_iota(jnp.int32, (DH, DH), 0) == lax.broadcasted_iota(jnp.int32, (DH, DH), 1),
                       1.0, 0.0).astype(BF16)
    sub = lax.broadcasted_iota(jnp.int32, (SPLIT_ROWS, t), 0)
    part = sub % 3
    p_sub = lax.broadcasted_iota(jnp.int32, (SPLIT_ROWS, 2 * DH), 0)
    p_lane = lax.broadcasted_iota(jnp.int32, (SPLIT_ROWS, 2 * DH), 1)
    gather_mat = jnp.where(((p_sub < 3) & (p_lane < DH)) | ((p_sub >= 3) & (p_sub < 6) & (p_lane >= DH)),
                           1.0, 0.0).astype(BF16)
    ones_cols = jnp.ones((t, DH), BF16)

    def stage1(c, hh, direction, Cn):
        rows = pl.ds(pl.multiple_of(c * t, t), t)
        cols = slice(hh * DH, (hh + 1) * DH)
        chunk_row = (pl.program_id(0) % seq_group) * nc + c
        r = pl.ds(chunk_row * N_GATE_COLS + direction * 2 * HEADS + pl.program_id(1) * hp + hh, 1)
        a_r = rows_ref[0, r, :]
        b_r = rows_ref[1, r, :]
        g_r = rows_ref[2, r, :]
        p_r = rows_ref[3, r, :]
        if direction == 0:
            end = t - 1
            mask = col_i <= row_i
        else:
            end = 0
            mask = col_i >= row_i
        b_end = b_r[:, end:end + 1]
        p_end = p_r[:, end:end + 1]

        x0 = jnp.where(sub < 3, b_r, jnp.where(sub < 6, g_r, 0.0))
        x1 = x0 - x0.astype(BF16).astype(F32)
        x2 = x1 - x1.astype(BF16).astype(F32)
        xs = jnp.where(part == 0, x0, jnp.where(part == 1, x1, x2)).astype(BF16)
        bg = lax.dot_general(xs, gather_mat, (((0,), (0,)), ((), ())), preferred_element_type=F32)
        qc = q_ref[rows, cols]
        kc = k_ref[rows, cols]
        v1 = jnp.concatenate([v_ref[rows, cols], ones_cols], axis=1)
        qk = lax.dot_general(qc, kc, (((1,), (1,)), ((), ())), preferred_element_type=F32)
        qs = _dot(qc, Cn.astype(BF16))
        kt = lax.dot_general(eye_dh, kc, (((1,), (1,)), ((), ())), preferred_element_type=F32)
        return rows, cols, mask, a_r, b_end, p_end, bg, v1, qk, qs, kt

    def stage2(st, Cn, m, h_ref, other_ref):
        rows, cols, mask, a_r, b_end, p_end, bg, v1, qk, qs, kt = st
        b_c = bg[:, :DH]
        m_t = jnp.maximum(b_c + m, bg[:, DH:])
        mu = m_t - b_c
        sdec = jnp.exp(m - mu)
        mu_t = jnp.concatenate([mu] * rep, axis=1)
        d = jnp.where(mask, jnp.exp((a_r + np.float32(np.log(scale))) - mu_t), 0.0)
        sv = _dot((qk * d).astype(BF16), v1)
        num = sdec * qs[:, :DH] + sv[:, :DH]
        den = sdec * qs[:, DH:] + sv[:, DH:]
        h = num * (1.0 / jnp.maximum(jnp.abs(den), jnp.exp(-m_t)))
        if other_ref is None:
            h_ref[rows, cols] = h
        else:
            hm_ref[rows, cols] = (h + other_ref[rows, cols]).astype(BF16)

        mx = jnp.maximum(m, p_end)
        w_r = jnp.exp(a_r - mx) * scale
        Cn_new = jnp.exp(m - mx) * Cn + _dot((kt * w_r).astype(BF16), v1)
        return Cn_new, b_end + mx

    def body(ci, carry, second_half):
        chains = [(hh, d) for hh in range(hp) for d in range(N_DIR)]
        chunk_of = lambda d: ci if d == 0 else nc - 1 - ci
        out = []
        for g0 in range(0, len(chains), CHAIN_GROUP):
            group = list(enumerate(chains))[g0:g0 + CHAIN_GROUP]
            firsts = [stage1(chunk_of(d), hh, d, carry[2 * k]) for k, (hh, d) in group]
            for st, (k, (hh, d)) in zip(firsts, group):
                mine, other = (hf_ref, hb_ref) if d == 0 else (hb_ref, hf_ref)
                out += stage2(st, carry[2 * k], carry[2 * k + 1], mine, other if second_half else None)
        return tuple(out)

    init = []
    for hh in range(hp):
        for d in range(N_DIR):
            if has_init:
                n_col = jnp.sum(eye_dh.astype(F32) * n0_ref[d, hh], axis=1, keepdims=True)
                init += [jnp.concatenate([c0_ref[d, hh], jnp.broadcast_to(n_col, (DH, DH))], axis=1),
                         m0_ref[d, hh][:, 0:1]]
            else:
                init += [jnp.zeros((DH, 2 * DH), F32), jnp.zeros((1, 1), F32)]
    if nc % 2 == 0:
        mid = lax.fori_loop(0, nc // 2, functools.partial(body, second_half=False), tuple(init), unroll=unroll)
        final = lax.fori_loop(nc // 2, nc, functools.partial(body, second_half=True), mid, unroll=unroll)
    else:
        final = lax.fori_loop(0, nc, functools.partial(body, second_half=False), tuple(init), unroll=unroll)
        hm_ref[...] = (hf_ref[...] + hb_ref[...]).astype(BF16)
    if emit_state:
        for layer in range(n_prev):
            cs_ref[layer] = pc_ref[layer]
            ns_ref[layer] = pn_ref[layer]
            ms_ref[layer] = pm_ref[layer]
        eye_f = eye_dh.astype(F32)
        for hh in range(hp):
            for d in range(N_DIR):
                Cn, m = final[2 * (hh * N_DIR + d):2 * (hh * N_DIR + d) + 2]
                cs_ref[n_prev, d, hh] = Cn[:, :DH]
                ns_ref[n_prev, d, hh] = jnp.sum(eye_f * Cn[:, DH:], axis=0, keepdims=True)
                ms_ref[n_prev, d, hh] = jnp.broadcast_to(m, (1, LANES))


def _mlstm(qkvo, grows, init, state_out, hp, unroll):
    b, l, _ = qkvo.shape
    emit_state = state_out is not None
    t = grows.shape[-1]
    nc = l // t
    seq_group = b // grows.shape[0]
    assert l % t == 0 and t % DH == 0 and HEADS % hp == 0 and nc % unroll == 0
    assert grows.shape[2] == seq_group * nc * N_GATE_COLS
    has_init = init is not None
    groups = HEADS // hp
    head_cols = lambda k: pl.BlockSpec((None, l, hp * DH), lambda i, h: (i, 0, k * groups + h))
    in_specs = [
        head_cols(0), head_cols(1), head_cols(2),
        pl.BlockSpec((None,) + grows.shape[1:], lambda i, h: (i // seq_group, 0, 0, 0))]
    args = [qkvo, qkvo, qkvo, grows]
    if has_init:
        c0, n0, m0, layer = init
        in_specs += [
            pl.BlockSpec((None, None, N_DIR, hp, DH, DH), lambda i, h: (i, layer, 0, h, 0, 0)),
            pl.BlockSpec((None, None, N_DIR, hp, 1, DH), lambda i, h: (i, layer, 0, h, 0, 0)),
            pl.BlockSpec((None, None, N_DIR, hp, 1, LANES), lambda i, h: (i, layer, 0, h, 0, 0))]
        args += [c0, n0, m0]
    out_specs = [pl.BlockSpec((None, l, hp * DH), lambda i, h: (i, 0, h))]
    out_shape = [jax.ShapeDtypeStruct((b, l, MLSTM_WIDTH), BF16)]
    n_prev = 0
    if emit_state:
        n_prev, prev = state_out
        slabs = lambda n, w: pl.BlockSpec((None, n, N_DIR, hp, w, DH), lambda i, h: (i, 0, 0, h, 0, 0))
        out_specs += [slabs(n_prev + 1, DH), slabs(n_prev + 1, 1), slabs(n_prev + 1, 1)]
        out_shape += [jax.ShapeDtypeStruct((b, n_prev + 1, N_DIR, HEADS, DH, DH), F32),
                      jax.ShapeDtypeStruct((b, n_prev + 1, N_DIR, HEADS, 1, DH), F32),
                      jax.ShapeDtypeStruct((b, n_prev + 1, N_DIR, HEADS, 1, DH), F32)]
        if n_prev:
            in_specs += [slabs(n_prev, DH), slabs(n_prev, 1), slabs(n_prev, 1)]
            args += list(prev)
    return pl.pallas_call(
        functools.partial(_mlstm_kernel, seq=l, chunk=t, hp=hp, seq_group=seq_group, has_init=has_init,
                          emit_state=emit_state, n_prev=n_prev, unroll=unroll),
        name="mlstm",
        grid=(b, groups),
        in_specs=in_specs, out_specs=out_specs, out_shape=out_shape,
        scratch_shapes=[
            pltpu.VMEM((l, hp * DH), F32),
            pltpu.VMEM((l, hp * DH), F32)],
        compiler_params=pltpu.CompilerParams(
            dimension_semantics=("arbitrary", "arbitrary"), vmem_limit_bytes=VMEM_LIMIT),
    )(*args)


def _merge_ffn_kernel(*refs, tm, add_pos):
    it = iter(refs)
    x_ref = next(it)
    pos_ref = next(it) if add_pos else None
    (mod_ref, ng_ref, yf_ref, up_ref, hm_ref, og_ref, mg_ref, band_ref, icnt_ref, wpool_ref, pscale_ref,
     wpf_ref, wpp_ref, wpm_ref, wout_ref, wfi_ref, wfo_ref, o_ref) = (next(it) for _ in range(18))
    x = x_ref[...]
    if add_pos:
        x = x + pos_ref[...]

    blocks = [slice(r * POOL_BLOCK, (r + 1) * POOL_BLOCK) for r in range(tm // POOL_BLOCK)]

    lane_group = lax.broadcasted_iota(jnp.int32, (POOL_BLOCK, POOL_WIDTH), 1) // POOL_GROUP_DIM
    pooled = []
    for rows in blocks:
        u = up_ref[rows, :]
        acc = jnp.zeros((POOL_BLOCK, POOL_WIDTH), F32)
        for g in range(len(POOL_WINDOWS)):
            acc = jnp.where(lane_group == g, _dot(band_ref[g], u), acc)
        pooled.append((acc * icnt_ref[...] - u.astype(F32)).astype(BF16))

    merged = []
    for rows, p in zip(blocks, pooled):
        pp_pre = _dot(p, wpool_ref[...])
        hg = (og_ref[rows, :].astype(F32) * hm_ref[rows, :].astype(F32)).astype(BF16)
        pp = (pp_pre * pscale_ref[...]).astype(BF16)
        yf_b = yf_ref[rows, :]
        pieces = []
        for c0 in range(0, D_MODEL, MXU_COLS):
            cs = slice(c0, c0 + MXU_COLS)
            g_f = mg_ref[rows, c0:c0 + MXU_COLS].astype(F32)
            g_p = mg_ref[rows, D_MODEL + c0:D_MODEL + c0 + MXU_COLS].astype(F32)
            g_m = mg_ref[rows, 2 * D_MODEL + c0:2 * D_MODEL + c0 + MXU_COLS].astype(F32)
            pieces.append((g_f * _dot(yf_b, wpf_ref[:, cs]) + g_p * _dot(pp, wpp_ref[:, cs])
                           + g_m * _dot(hg, wpm_ref[:, cs])).astype(BF16))
        merged.append(jnp.concatenate(pieces, axis=1))

    x1_blocks, h2_blocks = [], []
    for rows, y in zip(blocks, merged):
        x1b = x[rows, :] + _rms(_dot(y, wout_ref[...]), ng_ref[1:2, :] * mod_ref[2:3, :])
        x1_blocks.append(x1b)
        h2_blocks.append((_rms(x1b, ng_ref[2:3, :] * (1.0 + mod_ref[4:5, :])) + mod_ref[3:4, :]).astype(BF16))
    h2 = h2_blocks[0] if len(blocks) == 1 else jnp.concatenate(h2_blocks, axis=0)
    acc = jnp.zeros((tm, D_MODEL), F32)
    for k, (c0, c1) in enumerate(FFN_CHUNKS):
        if k == 0:
            ab = [(_dot(hb, wfi_ref[:, c0:c1]), _dot(hb, wfi_ref[:, D_FF + c0:D_FF + c1])) for hb in h2_blocks]
            a = jnp.concatenate([t[0] for t in ab], axis=0)
            bb = jnp.concatenate([t[1] for t in ab], axis=0)
        else:
            a = _dot(h2, wfi_ref[:, c0:c1])
            bb = _dot(h2, wfi_ref[:, D_FF + c0:D_FF + c1])
        act = (a * _sigmoid(a) * bb).astype(BF16)
        if k < len(FFN_CHUNKS) - 1:
            acc = acc + _dot(act, wfo_ref[c0:c1, :])
        else:
            for rows, x1b in zip(blocks, x1_blocks):
                yb = acc[rows, :] + _dot(act[rows, :], wfo_ref[c0:c1, :])
                o_ref[rows, :] = x1b + _rms(yb, ng_ref[3:4, :] * mod_ref[5:6, :])


def _merge_ffn(x, mod, norm_g, yf, up, hm, qkvo, mg, band, icnt, wpool, pscale, lw, layer, tm, pos=None):
    b, l, _ = x.shape
    add_pos = pos is not None
    ogate = pl.BlockSpec((None, tm, MLSTM_WIDTH), lambda i, j: (i, j, 3))
    per_batch_mod = mod.shape[0] > 1
    tok = lambda w: pl.BlockSpec((None, tm, w), lambda i, j: (i, j, 0))
    full = lambda a: pl.BlockSpec(a.shape, lambda i, j: (0,) * a.ndim)
    stacked = [lw[k] for k in ("wpf", "wpp", "wpm", "wout", "wfi", "wfo")]
    in_specs = [tok(D_MODEL)]
    args = [x]
    if add_pos:
        in_specs.append(pl.BlockSpec((tm, D_MODEL), lambda i, j: (j, 0)))
        args.append(pos)
    in_specs += [
        pl.BlockSpec((None, 6, D_MODEL), (lambda i, j: (i, 0, 0)) if per_batch_mod else (lambda i, j: (0, 0, 0))),
        full(norm_g), tok(FOURIER_WIDTH), tok(POOL_WIDTH), tok(MLSTM_WIDTH), ogate, tok(3 * D_MODEL),
        full(band), full(icnt), full(wpool), full(pscale)]
    in_specs += [_layer_spec(a, layer) for a in stacked]
    args += [mod, norm_g, yf, up, hm, qkvo, mg, band, icnt, wpool, pscale] + stacked
    return pl.pallas_call(
        functools.partial(_merge_ffn_kernel, tm=tm, add_pos=add_pos),
        name="merge_ffn",
        grid=(b, l // tm),
        in_specs=in_specs,
        out_specs=tok(D_MODEL),
        out_shape=jax.ShapeDtypeStruct((b, l, D_MODEL), F32),
        compiler_params=pltpu.CompilerParams(
            dimension_semantics=("arbitrary", "arbitrary"), vmem_limit_bytes=VMEM_LIMIT),
    )(*args)


def _repack_kernel(w_ref, o_ref):
    o_ref[:, 0:CAT_M] = w_ref[:, 0:OFF_G]
    o_ref[:, CAT_M:CAT_G] = w_ref[:, OFF_M:D_IN]
    tail = w_ref[:, OFF_G:OFF_G + LANES].astype(F32)
    lane = lax.broadcasted_iota(jnp.int32, tail.shape, 1)
    o_ref[:, CAT_G:CAT_W] = jnp.where(lane < N_GATE_COLS, tail, 0.0).astype(BF16)


def _repack_w_in(w_in):
    wp = jnp.pad(w_in, ((0, 0), (0, 0), (0, CAT_W - D_IN))).astype(BF16)
    tk = WEIGHT_ROW_SLAB
    blk = pl.BlockSpec((None, tk, CAT_W), lambda l, k: (l, k, 0))
    return pl.pallas_call(
        _repack_kernel,
        name="repack_w_in",
        grid=(DEPTH, D_MODEL // tk),
        in_specs=[blk],
        out_specs=blk,
        out_shape=jax.ShapeDtypeStruct((DEPTH, D_MODEL, CAT_W), BF16),
        compiler_params=pltpu.CompilerParams(
            dimension_semantics=("arbitrary", "arbitrary"), vmem_limit_bytes=VMEM_LIMIT),
    )(wp)


def _stacked_weights(w_in, w_proj_f, w_proj_p, w_proj_m, w_out, w_ffn_in, w_ffn_out):
    return dict(
        w_cat=_repack_w_in(w_in),
        wpf=w_proj_f.astype(BF16), wpp=w_proj_p.astype(BF16), wpm=w_proj_m.astype(BF16),
        wout=w_out.astype(BF16), wfi=w_ffn_in.astype(BF16), wfo=w_ffn_out.astype(BF16))


def _block(x, mod, lw, layer, norm_g, b_gate, wpool, pscale, width, tm, fold, init, state_out, hp, unroll, pos=None):
    b, l, _ = x.shape
    assert b % fold == 0 and (fold == 1 or mod.shape[0] == 1)
    folded = lambda a: a.reshape(b // fold, fold * l, a.shape[-1])
    unfolded = lambda a: a.reshape(b, l, a.shape[-1])
    tm_in = min(INPROJ_TILE, fold * l)
    uf, up, qkvo, grows, mg = _inproj(folded(x), mod, norm_g, lw["w_cat"], layer, b_gate, tm_in, min(MLSTM_CHUNK, l), pos)
    yf = _fourier(unfolded(uf))
    mouts = _mlstm(unfolded(qkvo), grows, init, state_out, hp, unroll)
    band, icnt = _pool_tables(width)
    x = _merge_ffn(folded(x), mod, norm_g, folded(yf), up, folded(mouts[0]), qkvo, mg,
                   jnp.asarray(band).astype(BF16), jnp.asarray(icnt), wpool, pscale, lw, layer, tm, pos)
    return unfolded(x), mouts[1:]


def kernel(x_prompt, x_sample, state_C, state_n, state_m, c, c_ctx, w_ada, b_ada, norm_g, w_in, b_gate, w_proj_f, w_pool, pool_scale, w_proj_p, w_proj_m, w_out, w_ffn_in, w_ffn_out):
    bp, lp, _ = x_prompt.shape
    bs, ls, _ = x_sample.shape
    cond = jnp.concatenate([c_ctx[None, :], c, jnp.zeros((COND_ROWS - 1 - bs, D_MODEL), F32)], axis=0)
    mod = _modulation(cond, w_ada, b_ada).reshape(DEPTH, COND_ROWS, 6, D_MODEL)
    lw = _stacked_weights(w_in, w_proj_f, w_proj_p, w_proj_m, w_out, w_ffn_in, w_ffn_out)
    group_eye = jnp.eye(len(POOL_WINDOWS), dtype=w_pool.dtype)[None, :, None, :, None]
    wpool = (w_pool[:, :, :, None, :] * group_eye).reshape(DEPTH, POOL_WIDTH, POOL_WIDTH).astype(BF16)
    layer_args = [(lw, i, norm_g[i], b_gate[i], wpool[i], pool_scale[i].reshape(1, POOL_WIDTH))
                  for i in range(DEPTH)]

    xp = x_prompt
    states = None
    for i in range(DEPTH):
        xp, states = _block(xp, mod[i, 0:1], *layer_args[i], lp, TOKEN_TILE, TOKEN_TILE // lp, None, (i, states),
                            HEADS, 1)
    new_c, new_n, new_m = states

    xs = x_sample
    pos = jnp.asarray(_pos_table(ls // GRID_W))
    n0 = state_n.reshape(bs, DEPTH, N_DIR, HEADS, 1, DH)
    m0 = jnp.broadcast_to(state_m[..., None, None], (bs, DEPTH, N_DIR, HEADS, 1, LANES))
    for i in range(DEPTH):
        init = (state_C, n0, m0, i)
        xs, _ = _block(xs, mod[i, 1:1 + bs], *layer_args[i], GRID_W, TOKEN_TILE, 1, init, None, HEADS, 1,
                       pos if i == 0 else None)

    return (xp, xs, new_c, new_n[:, :, :, :, 0, :], new_m[:, :, :, :, 0, 0])
```

```python
import functools

import numpy as np
import jax
import jax.numpy as jnp
from jax import lax
from jax.experimental import pallas as pl
from jax.experimental.pallas import tpu as pltpu

F32 = jnp.float32
BF16 = jnp.bfloat16

D_MODEL = 1024
DEPTH = 2
GRID_W = 64
FOURIER_GROUPS = 4
FOURIER_WIDTH = D_MODEL // 4
FOURIER_GROUP_DIM = FOURIER_WIDTH // FOURIER_GROUPS
POOL_WINDOWS = (2, 4, 8, 16)
POOL_WIDTH = D_MODEL // 4
POOL_GROUP_DIM = POOL_WIDTH // len(POOL_WINDOWS)
HEADS = 4
MLSTM_WIDTH = D_MODEL // 2
DH = MLSTM_WIDTH // HEADS
N_DIR = 2
N_GATE_COLS = N_DIR * 2 * HEADS
D_FF = -(-8 * D_MODEL // (3 * 256)) * 256
RMS_EPS = 1e-6
POS_BASE = 10000.0

OFF_F = 0
OFF_P = OFF_F + FOURIER_WIDTH
OFF_QKVO = OFF_P + POOL_WIDTH
OFF_G = OFF_QKVO + 4 * MLSTM_WIDTH
OFF_M = OFF_G + N_GATE_COLS
D_IN = OFF_M + 3 * D_MODEL

LANES = 128
MLSTM_CHUNK = 256
POOL_BLOCK = 256
TOKEN_TILE = 512
INPROJ_TILE = 1024
PROJ_CHUNK = 512
WEIGHT_ROW_SLAB = 256
COND_ROWS = 16
VMEM_LIMIT = 60 * 1024 * 1024
MXU_COLS = 256
FFN_CHUNKS = ((0, 5 * MXU_COLS), (5 * MXU_COLS, D_FF))
CAT_M = OFF_G
CAT_G = CAT_M + 3 * D_MODEL
CAT_W = CAT_G + LANES


def _dot(a, b):
    return jnp.dot(a, b, preferred_element_type=F32)


def _rms(x, g):
    return x * lax.rsqrt(jnp.mean(x * x, axis=-1, keepdims=True) + RMS_EPS) * g


def _log_sigmoid(x):
    return jnp.minimum(x, 0.0) - jnp.log(1.0 + jnp.exp(-jnp.abs(x)))


def _sigmoid(x):
    return 0.5 * jnp.tanh(0.5 * x) + 0.5


@functools.lru_cache(maxsize=None)
def _dft_tables(n):
    idx = np.arange(n, dtype=np.int64)
    ang = 2.0 * np.pi * ((idx[:, None] * idx[None, :]) % n).astype(np.float64) / n
    return np.cos(ang).astype(np.float32), np.sin(ang).astype(np.float32)


@functools.lru_cache(maxsize=None)
def _dft_half_tables(n):
    f = np.arange(n // 2, dtype=np.int64)[:, None]
    j = np.arange(n // 2, dtype=np.int64)[None, :]
    out = []
    for l in (2 * j, 2 * j + 1):
        ang = 2.0 * np.pi * ((f * l) % n).astype(np.float64) / n
        out += [np.cos(ang).astype(np.float32), np.sin(ang).astype(np.float32)]
    return tuple(out)


@functools.lru_cache(maxsize=None)
def _group_dft_tables():
    c, s = _dft_tables(FOURIER_GROUP_DIM)
    bc = np.zeros((FOURIER_WIDTH, FOURIER_WIDTH), np.float32)
    bs = np.zeros((FOURIER_WIDTH, FOURIER_WIDTH), np.float32)
    for g in range(FOURIER_GROUPS):
        sl = slice(g * FOURIER_GROUP_DIM, (g + 1) * FOURIER_GROUP_DIM)
        bc[sl, sl] = c
        bs[sl, sl] = s
    return bc, bs


@functools.lru_cache(maxsize=None)
def _pool_tables(width):
    band = np.zeros((len(POOL_WINDOWS), POOL_BLOCK, POOL_BLOCK), np.float32)
    inv_cnt = np.zeros((POOL_BLOCK, POOL_WIDTH), np.float32)
    for g, w in enumerate(POOL_WINDOWS):
        left = w // 2
        right = w - 1 - left
        for t in range(POOL_BLOCK):
            row, pos = divmod(t, width)
            lo = min(max(pos - left, 0), width - 1)
            hi = min(max(pos + right, 0), width - 1)
            band[g, t, row * width + lo:row * width + hi + 1] = 1.0
            inv_cnt[t, g * POOL_GROUP_DIM:(g + 1) * POOL_GROUP_DIM] = 1.0 / (hi - lo + 1)
    return band, inv_cnt


@functools.lru_cache(maxsize=None)
def _pos_table(rows):
    quarter = D_MODEL // 4
    omega = 1.0 / (POS_BASE ** (np.arange(quarter, dtype=np.float64) / quarter))
    t = np.arange(rows * GRID_W)
    r = (t // GRID_W).astype(np.float64)
    col = (t % GRID_W).astype(np.float64)
    ar = r[:, None] * omega[None, :]
    ac = col[:, None] * omega[None, :]
    return np.concatenate([np.sin(ar), np.cos(ar), np.sin(ac), np.cos(ac)], axis=-1).astype(np.float32)


def _mod_kernel(c_ref, w_ref, b_ref, o_ref):
    @pl.when(pl.program_id(1) == 0)
    def _():
        o_ref[...] = jnp.broadcast_to(b_ref[...], o_ref.shape)

    c = c_ref[...]
    a = (c * jax.nn.sigmoid(c)).astype(BF16)
    o_ref[...] += _dot(a, w_ref[...].astype(BF16))


def _modulation(cond, w_ada, b_ada):
    tk = WEIGHT_ROW_SLAB
    n = 6 * D_MODEL
    return pl.pallas_call(
        _mod_kernel,
        name="adaln_mod",
        grid=(DEPTH, D_MODEL // tk),
        in_specs=[
            pl.BlockSpec((COND_ROWS, tk), lambda l, k: (0, k)),
            pl.BlockSpec((None, tk, n), lambda l, k: (l, k, 0)),
            pl.BlockSpec((None, 1, n), lambda l, k: (l, 0, 0)),
        ],
        out_specs=pl.BlockSpec((None, COND_ROWS, n), lambda l, k: (l, 0, 0)),
        out_shape=jax.ShapeDtypeStruct((DEPTH, COND_ROWS, 6 * D_MODEL), F32),
        compiler_params=pltpu.CompilerParams(
            dimension_semantics=("arbitrary", "arbitrary"), vmem_limit_bytes=VMEM_LIMIT),
    )(cond, w_ada, b_ada.reshape(DEPTH, 1, 6 * D_MODEL))


def _gate_rows(g, bias_col, rows_ref, chunk):
    x = g.T[:N_GATE_COLS, :] + bias_col
    row = lax.broadcasted_iota(jnp.int32, x.shape, 0)
    fwd = row < 2 * HEADS
    lf = _log_sigmoid(x)
    b_f = jnp.where(fwd, _scan_lanes(lf, False, jnp.add, 0.0), _scan_lanes(lf, True, jnp.add, 0.0))
    b_i = pltpu.roll(b_f, N_GATE_COLS - HEADS, 0)
    a = x - b_i
    p = jnp.where(fwd, _scan_lanes(a, False, jnp.maximum, -jnp.inf), _scan_lanes(a, True, jnp.maximum, -jnp.inf))
    sl = slice(chunk * N_GATE_COLS, (chunk + 1) * N_GATE_COLS)
    rows_ref[0, sl, :] = a
    rows_ref[1, sl, :] = b_i
    rows_ref[2, sl, :] = b_i + p
    rows_ref[3, sl, :] = p


def _inproj_kernel(*refs, add_pos, chunk):
    it = iter(refs)
    x_ref = next(it)
    pos_ref = next(it) if add_pos else None
    mod_ref, ng_ref, w_ref, bg_ref = (next(it) for _ in range(4))
    uf_ref, up_ref, qkvo_ref, rows_ref, mg_ref = (next(it) for _ in range(5))

    x = x_ref[...]
    if add_pos:
        x = x + pos_ref[...]
    blocks = [slice(r * chunk, (r + 1) * chunk) for r in range(x.shape[0] // chunk)]
    gain = ng_ref[0:1, :] * (1.0 + mod_ref[1:2, :])
    ck = PROJ_CHUNK
    h_blocks = []
    for rows in blocks:
        hb = (_rms(x[rows, :], gain) + mod_ref[0:1, :]).astype(BF16)
        za = _dot(hb, w_ref[:, 0:ck])
        uf_ref[rows, :] = za[:, :FOURIER_WIDTH].astype(BF16)
        up_ref[rows, :] = za[:, FOURIER_WIDTH:].astype(BF16)
        h_blocks.append(hb)
    h = h_blocks[0] if len(blocks) == 1 else jnp.concatenate(h_blocks, axis=0)
    for j in range(4 * MLSTM_WIDTH // ck):
        z = _dot(h, w_ref[:, (j + 1) * ck:(j + 2) * ck])
        if j * ck >= 3 * MLSTM_WIDTH:
            z = _sigmoid(z)
        qkvo_ref[:, j * ck:(j + 1) * ck] = z.astype(BF16)
    g = _dot(h, w_ref[:, CAT_G:CAT_W])
    for c in range(g.shape[0] // chunk):
        _gate_rows(g[c * chunk:(c + 1) * chunk, :], bg_ref[...], rows_ref, c)
    n_mg = 3 * D_MODEL // ck
    for j in range(n_mg):
        cols = slice(j * ck, (j + 1) * ck)
        wcols = slice(CAT_M + j * ck, CAT_M + (j + 1) * ck)
        if j < n_mg - 1:
            mg_ref[:, cols] = _sigmoid(_dot(h, w_ref[:, wcols])).astype(BF16)
        else:
            for rows, hb in zip(blocks, h_blocks):
                mg_ref[rows, cols] = _sigmoid(_dot(hb, w_ref[:, wcols])).astype(BF16)


def _layer_spec(a, layer):
    return pl.BlockSpec((None,) + a.shape[1:], lambda *_: (layer,) + (0,) * (a.ndim - 1),
                        pipeline_mode=pl.Buffered(1))


def _inproj(x, mod, norm_g, w_cat, layer, b_gate, tm, t, pos=None):
    b, l, _ = x.shape
    add_pos = pos is not None
    assert tm % t == 0 and l % tm == 0
    bias_col = b_gate.reshape(N_GATE_COLS, 1)
    per_batch_mod = mod.shape[0] > 1
    tok = lambda w: pl.BlockSpec((None, tm, w), lambda i, j: (i, j, 0))
    full = lambda a: pl.BlockSpec(a.shape, lambda i, j: (0,) * a.ndim)
    in_specs = [tok(D_MODEL)]
    args = [x]
    if add_pos:
        in_specs.append(pl.BlockSpec((tm, D_MODEL), lambda i, j: (j, 0)))
        args.append(pos)
    in_specs += [
        pl.BlockSpec((None, 6, D_MODEL), (lambda i, j: (i, 0, 0)) if per_batch_mod else (lambda i, j: (0, 0, 0))),
        full(norm_g), _layer_spec(w_cat, layer), full(bias_col)]
    args += [mod, norm_g, w_cat, bias_col]
    rows_per_tile = tm // t * N_GATE_COLS
    out_specs = [tok(FOURIER_WIDTH), tok(POOL_WIDTH), tok(4 * MLSTM_WIDTH),
                 pl.BlockSpec((None, 4, rows_per_tile, t), lambda i, j: (i, 0, j, 0)), tok(3 * D_MODEL)]
    out_shape = [
        jax.ShapeDtypeStruct((b, l, FOURIER_WIDTH), BF16),
        jax.ShapeDtypeStruct((b, l, POOL_WIDTH), BF16),
        jax.ShapeDtypeStruct((b, l, 4 * MLSTM_WIDTH), BF16),
        jax.ShapeDtypeStruct((b, 4, l // t * N_GATE_COLS, t), F32),
        jax.ShapeDtypeStruct((b, l, 3 * D_MODEL), BF16)]
    return pl.pallas_call(
        functools.partial(_inproj_kernel, add_pos=add_pos, chunk=t),
        name="inproj",
        grid=(b, l // tm),
        in_specs=in_specs, out_specs=out_specs, out_shape=out_shape,
        compiler_params=pltpu.CompilerParams(
            dimension_semantics=("arbitrary", "arbitrary"), vmem_limit_bytes=VMEM_LIMIT),
    )(*args)


def _fourier_kernel(u_ref, ce_ref, se_ref, co_ref, so_ref, bc_ref, bs_ref, o_ref, vc_ref, vs_ref, *, scale, half):
    nseq, l, w = u_ref.shape
    u = u_ref[...].reshape(nseq * l, w)
    nblk = w // LANES
    for ref, tab in ((vc_ref, bc_ref), (vs_ref, bs_ref)):
        v = _dot(u, tab[...])
        for k in range(nblk):
            ref[k] = v[:, k * LANES:(k + 1) * LANES]

    def rows(ref, start):
        return jnp.concatenate([ref[k, pl.ds(start, half, stride=2), :] for k in range(nblk)], axis=1).astype(BF16)

    for s in range(nseq):
        e = _dot(ce_ref[...], rows(vc_ref, s * l)) - _dot(se_ref[...], rows(vs_ref, s * l))
        o = _dot(co_ref[...], rows(vc_ref, s * l + 1)) - _dot(so_ref[...], rows(vs_ref, s * l + 1))
        o_ref[s, 0:half, :] = ((e + o) * scale).astype(BF16)
        o_ref[s, half:, :] = ((e - o) * scale).astype(BF16)


def _fourier(uf):
    b, l, w = uf.shape
    half = l // 2
    bc, bs = (jnp.asarray(t).astype(BF16) for t in _group_dft_tables())
    tables = [jnp.asarray(t).astype(BF16) for t in _dft_half_tables(l)]
    full = lambda a: pl.BlockSpec(a.shape, lambda i: (0,) * a.ndim)
    nseq = max(1, min(b, INPROJ_TILE // l))
    assert b % nseq == 0
    seq = pl.BlockSpec((nseq, l, w), lambda i: (i, 0, 0))
    return pl.pallas_call(
        functools.partial(_fourier_kernel, scale=float((l * FOURIER_GROUP_DIM) ** -0.5), half=half),
        name="fourier",
        grid=(b // nseq,),
        in_specs=[seq] + [full(t) for t in tables] + [full(bc), full(bs)],
        out_specs=seq,
        out_shape=jax.ShapeDtypeStruct((b, l, w), BF16),
        scratch_shapes=[pltpu.VMEM((w // LANES, nseq * l, LANES), F32),
                        pltpu.VMEM((w // LANES, nseq * l, LANES), F32)],
        compiler_params=pltpu.CompilerParams(
            dimension_semantics=("arbitrary",), vmem_limit_bytes=VMEM_LIMIT),
    )(uf, *tables, bc, bs)


def _scan_lanes(x, reverse, op, fill):
    n = x.shape[-1]
    lane = lax.broadcasted_iota(jnp.int32, x.shape, x.ndim - 1)
    s = 1
    while s < n:
        if reverse:
            x = op(x, jnp.where(lane < n - s, pltpu.roll(x, n - s, x.ndim - 1), fill))
        else:
            x = op(x, jnp.where(lane >= s, pltpu.roll(x, s, x.ndim - 1), fill))
        s *= 2
    return x


SPLIT_ROWS = 16
CHAIN_GROUP = 2


def _mlstm_kernel(*refs, seq, chunk, hp, seq_group, has_init, emit_state, n_prev, unroll):
    it = iter(refs)
    q_ref, k_ref, v_ref, rows_ref = (next(it) for _ in range(4))
    c0_ref, n0_ref, m0_ref = (next(it) for _ in range(3)) if has_init else (None, None, None)
    pc_ref, pn_ref, pm_ref = (next(it) for _ in range(3)) if n_prev else (None, None, None)
    hm_ref = next(it)
    cs_ref, ns_ref, ms_ref = (next(it) for _ in range(3)) if emit_state else (None, None, None)
    hf_ref, hb_ref = (next(it) for _ in range(2))

    t = chunk
    nc = seq // t
    rep = t // DH
    scale = DH ** -0.5

    row_i = lax.broadcasted_iota(jnp.int32, (t, t), 0)
    col_i = lax.broadcasted_iota(jnp.int32, (t, t), 1)
    eye_dh = jnp.where(lax.broadcasted_iota(jnp.int32, (DH, DH), 0) == lax.broadcasted_iota(jnp.int32, (DH, DH), 1),
                       1.0, 0.0).astype(BF16)
    sub = lax.broadcasted_iota(jnp.int32, (SPLIT_ROWS, t), 0)
    part = sub % 3
    p_sub = lax.broadcasted_iota(jnp.int32, (SPLIT_ROWS, 2 * DH), 0)
    p_lane = lax.broadcasted_iota(jnp.int32, (SPLIT_ROWS, 2 * DH), 1)
    gather_mat = jnp.where(((p_sub < 3) & (p_lane < DH)) | ((p_sub >= 3) & (p_sub < 6) & (p_lane >= DH)),
                           1.0, 0.0).astype(BF16)
    ones_cols = jnp.ones((t, DH), BF16)

    def stage1(c, hh, direction, Cn):
        rows = pl.ds(pl.multiple_of(c * t, t), t)
        cols = slice(hh * DH, (hh + 1) * DH)
        chunk_row = (pl.program_id(0) % seq_group) * nc + c
        r = pl.ds(chunk_row * N_GATE_COLS + direction * 2 * HEADS + pl.program_id(1) * hp + hh, 1)
        a_r = rows_ref[0, r, :]
        b_r = rows_ref[1, r, :]
        g_r = rows_ref[2, r, :]
        p_r = rows_ref[3, r, :]
        if direction == 0:
            end = t - 1
            mask = col_i <= row_i
        else:
            end = 0
            mask = col_i >= row_i
        b_end = b_r[:, end:end + 1]
        p_end = p_r[:, end:end + 1]

        x0 = jnp.where(sub < 3, b_r, jnp.where(sub < 6, g_r, 0.0))
        x1 = x0 - x0.astype(BF16).astype(F32)
        x2 = x1 - x1.astype(BF16).astype(F32)
        xs = jnp.where(part == 0, x0, jnp.where(part == 1, x1, x2)).astype(BF16)
        bg = lax.dot_general(xs, gather_mat, (((0,), (0,)), ((), ())), preferred_element_type=F32)
        qc = q_ref[rows, cols]
        kc = k_ref[rows, cols]
        v1 = jnp.concatenate([v_ref[rows, cols], ones_cols], axis=1)
        qk = lax.dot_general(qc, kc, (((1,), (1,)), ((), ())), preferred_element_type=F32)
        qs = _dot(qc, Cn.astype(BF16))
        kt = lax.dot_general(eye_dh, kc, (((1,), (1,)), ((), ())), preferred_element_type=F32)
        return rows, cols, mask, a_r, b_end, p_end, bg, v1, qk, qs, kt

    def stage2(st, Cn, m, h_ref, other_ref):
        rows, cols, mask, a_r, b_end, p_end, bg, v1, qk, qs, kt = st
        b_c = bg[:, :DH]
        m_t = jnp.maximum(b_c + m, bg[:, DH:])
        mu = m_t - b_c
        sdec = jnp.exp(m - mu)
        mu_t = jnp.concatenate([mu] * rep, axis=1)
        d = jnp.where(mask, jnp.exp((a_r + np.float32(np.log(scale))) - mu_t), 0.0)
        sv = _dot((qk * d).astype(BF16), v1)
        num = sdec * qs[:, :DH] + sv[:, :DH]
        den = sdec * qs[:, DH:] + sv[:, DH:]
        h = num * (1.0 / jnp.maximum(jnp.abs(den), jnp.exp(-m_t)))
        if other_ref is None:
            h_ref[rows, cols] = h
        else:
            hm_ref[rows, cols] = (h + other_ref[rows, cols]).astype(BF16)

        mx = jnp.maximum(m, p_end)
        w_r = jnp.exp(a_r - mx) * scale
        Cn_new = jnp.exp(m - mx) * Cn + _dot((kt * w_r).astype(BF16), v1)
        return Cn_new, b_end + mx

    def body(ci, carry, second_half):
        chains = [(hh, d) for hh in range(hp) for d in range(N_DIR)]
        chunk_of = lambda d: ci if d == 0 else nc - 1 - ci
        out = []
        for g0 in range(0, len(chains), CHAIN_GROUP):
            group = list(enumerate(chains))[g0:g0 + CHAIN_GROUP]
            firsts = [stage1(chunk_of(d), hh, d, carry[2 * k]) for k, (hh, d) in group]
            for st, (k, (hh, d)) in zip(firsts, group):
                mine, other = (hf_ref, hb_ref) if d == 0 else (hb_ref, hf_ref)
                out += stage2(st, carry[2 * k], carry[2 * k + 1], mine, other if second_half else None)
        return tuple(out)

    init = []
    for hh in range(hp):
        for d in range(N_DIR):
            if has_init:
                n_col = jnp.sum(eye_dh.astype(F32) * n0_ref[d, hh], axis=1, keepdims=True)
                init += [jnp.concatenate([c0_ref[d, hh], jnp.broadcast_to(n_col, (DH, DH))], axis=1),
                         m0_ref[d, hh][:, 0:1]]
            else:
                init += [jnp.zeros((DH, 2 * DH), F32), jnp.zeros((1, 1), F32)]
    if nc % 2 == 0:
        mid = lax.fori_loop(0, nc // 2, functools.partial(body, second_half=False), tuple(init), unroll=unroll)
        final = lax.fori_loop(nc // 2, nc, functools.partial(body, second_half=True), mid, unroll=unroll)
    else:
        final = lax.fori_loop(0, nc, functools.partial(body, second_half=False), tuple(init), unroll=unroll)
        hm_ref[...] = (hf_ref[...] + hb_ref[...]).astype(BF16)
    if emit_state:
        for layer in range(n_prev):
            cs_ref[layer] = pc_ref[layer]
            ns_ref[layer] = pn_ref[layer]
            ms_ref[layer] = pm_ref[layer]
        eye_f = eye_dh.astype(F32)
        for hh in range(hp):
            for d in range(N_DIR):
                Cn, m = final[2 * (hh * N_DIR + d):2 * (hh * N_DIR + d) + 2]
                cs_ref[n_prev, d, hh] = Cn[:, :DH]
                ns_ref[n_prev, d, hh] = jnp.sum(eye_f * Cn[:, DH:], axis=0, keepdims=True)
                ms_ref[n_prev, d, hh] = jnp.broadcast_to(m, (1, LANES))


def _mlstm(qkvo, grows, init, state_out, hp, unroll):
    b, l, _ = qkvo.shape
    emit_state = state_out is not None
    t = grows.shape[-1]
    nc = l // t
    seq_group = b // grows.shape[0]
    assert l % t == 0 and t % DH == 0 and HEADS % hp == 0 and nc % unroll == 0
    assert grows.shape[2] == seq_group * nc * N_GATE_COLS
    has_init = init is not None
    groups = HEADS // hp
    head_cols = lambda k: pl.BlockSpec((None, l, hp * DH), lambda i, h: (i, 0, k * groups + h))
    in_specs = [
        head_cols(0), head_cols(1), head_cols(2),
        pl.BlockSpec((None,) + grows.shape[1:], lambda i, h: (i // seq_group, 0, 0, 0))]
    args = [qkvo, qkvo, qkvo, grows]
    if has_init:
        c0, n0, m0, layer = init
        in_specs += [
            pl.BlockSpec((None, None, N_DIR, hp, DH, DH), lambda i, h: (i, layer, 0, h, 0, 0)),
            pl.BlockSpec((None, None, N_DIR, hp, 1, DH), lambda i, h: (i, layer, 0, h, 0, 0)),
            pl.BlockSpec((None, None, N_DIR, hp, 1, LANES), lambda i, h: (i, layer, 0, h, 0, 0))]
        args += [c0, n0, m0]
    out_specs = [pl.BlockSpec((None, l, hp * DH), lambda i, h: (i, 0, h))]
    out_shape = [jax.ShapeDtypeStruct((b, l, MLSTM_WIDTH), BF16)]
    n_prev = 0
    if emit_state:
        n_prev, prev = state_out
        slabs = lambda n, w: pl.BlockSpec((None, n, N_DIR, hp, w, DH), lambda i, h: (i, 0, 0, h, 0, 0))
        out_specs += [slabs(n_prev + 1, DH), slabs(n_prev + 1, 1), slabs(n_prev + 1, 1)]
        out_shape += [jax.ShapeDtypeStruct((b, n_prev + 1, N_DIR, HEADS, DH, DH), F32),
                      jax.ShapeDtypeStruct((b, n_prev + 1, N_DIR, HEADS, 1, DH), F32),
                      jax.ShapeDtypeStruct((b, n_prev + 1, N_DIR, HEADS, 1, DH), F32)]
        if n_prev:
            in_specs += [slabs(n_prev, DH), slabs(n_prev, 1), slabs(n_prev, 1)]
            args += list(prev)
    return pl.pallas_call(
        functools.partial(_mlstm_kernel, seq=l, chunk=t, hp=hp, seq_group=seq_group, has_init=has_init,
                          emit_state=emit_state, n_prev=n_prev, unroll=unroll),
        name="mlstm",
        grid=(b, groups),
        in_specs=in_specs, out_specs=out_specs, out_shape=out_shape,
        scratch_shapes=[
            pltpu.VMEM((l, hp * DH), F32),
            pltpu.VMEM((l, hp * DH), F32)],
        compiler_params=pltpu.CompilerParams(
            dimension_semantics=("arbitrary", "arbitrary"), vmem_limit_bytes=VMEM_LIMIT),
    )(*args)


def _merge_ffn_kernel(*refs, tm, add_pos):
    it = iter(refs)
    x_ref = next(it)
    pos_ref = next(it) if add_pos else None
    (mod_ref, ng_ref, yf_ref, up_ref, hm_ref, og_ref, mg_ref, band_ref, icnt_ref, wpool_ref, pscale_ref,
     wpf_ref, wpp_ref, wpm_ref, wout_ref, wfi_ref, wfo_ref, o_ref) = (next(it) for _ in range(18))
    x = x_ref[...]
    if add_pos:
        x = x + pos_ref[...]

    blocks = [slice(r * POOL_BLOCK, (r + 1) * POOL_BLOCK) for r in range(tm // POOL_BLOCK)]

    lane_group = lax.broadcasted_iota(jnp.int32, (POOL_BLOCK, POOL_WIDTH), 1) // POOL_GROUP_DIM
    pooled = []
    for rows in blocks:
        u = up_ref[rows, :]
        acc = jnp.zeros((POOL_BLOCK, POOL_WIDTH), F32)
        for g in range(len(POOL_WINDOWS)):
            acc = jnp.where(lane_group == g, _dot(band_ref[g], u), acc)
        pooled.append((acc * icnt_ref[...] - u.astype(F32)).astype(BF16))

    merged = []
    for rows, p in zip(blocks, pooled):
        pp_pre = _dot(p, wpool_ref[...])
        hg = (og_ref[rows, :].astype(F32) * hm_ref[rows, :].astype(F32)).astype(BF16)
        pp = (pp_pre * pscale_ref[...]).astype(BF16)
        yf_b = yf_ref[rows, :]
        pieces = []
        for c0 in range(0, D_MODEL, MXU_COLS):
            cs = slice(c0, c0 + MXU_COLS)
            g_f = mg_ref[rows, c0:c0 + MXU_COLS].astype(F32)
            g_p = mg_ref[rows, D_MODEL + c0:D_MODEL + c0 + MXU_COLS].astype(F32)
            g_m = mg_ref[rows, 2 * D_MODEL + c0:2 * D_MODEL + c0 + MXU_COLS].astype(F32)
            pieces.append((g_f * _dot(yf_b, wpf_ref[:, cs]) + g_p * _dot(pp, wpp_ref[:, cs])
                           + g_m * _dot(hg, wpm_ref[:, cs])).astype(BF16))
        merged.append(jnp.concatenate(pieces, axis=1))

    x1_blocks, h2_blocks = [], []
    for rows, y in zip(blocks, merged):
        x1b = x[rows, :] + _rms(_dot(y, wout_ref[...]), ng_ref[1:2, :] * mod_ref[2:3, :])
        x1_blocks.append(x1b)
        h2_blocks.append((_rms(x1b, ng_ref[2:3, :] * (1.0 + mod_ref[4:5, :])) + mod_ref[3:4, :]).astype(BF16))
    h2 = h2_blocks[0] if len(blocks) == 1 else jnp.concatenate(h2_blocks, axis=0)
    acc = jnp.zeros((tm, D_MODEL), F32)
    def swiglu(hrows, c0, c1):
        tiles = []
        for t0 in range(c0, c1, MXU_COLS):
            a = _dot(hrows, wfi_ref[:, t0:t0 + MXU_COLS])
            bb = _dot(hrows, wfi_ref[:, D_FF + t0:D_FF + t0 + MXU_COLS])
            tiles.append((a * _sigmoid(a) * bb).astype(BF16))
        return jnp.concatenate(tiles, axis=1)

    for k, (c0, c1) in enumerate(FFN_CHUNKS):
        if k == 0:
            act = jnp.concatenate([swiglu(hb, c0, c1) for hb in h2_blocks], axis=0)
        else:
            act = swiglu(h2, c0, c1)
        if k < len(FFN_CHUNKS) - 1:
            acc = acc + _dot(act, wfo_ref[c0:c1, :])
        else:
            for rows, x1b in zip(blocks, x1_blocks):
                yb = acc[rows, :] + _dot(act[rows, :], wfo_ref[c0:c1, :])
                o_ref[rows, :] = x1b + _rms(yb, ng_ref[3:4, :] * mod_ref[5:6, :])


def _merge_ffn(x, mod, norm_g, yf, up, hm, qkvo, mg, band, icnt, wpool, pscale, lw, layer, tm, pos=None):
    b, l, _ = x.shape
    add_pos = pos is not None
    ogate = pl.BlockSpec((None, tm, MLSTM_WIDTH), lambda i, j: (i, j, 3))
    per_batch_mod = mod.shape[0] > 1
    tok = lambda w: pl.BlockSpec((None, tm, w), lambda i, j: (i, j, 0))
    full = lambda a: pl.BlockSpec(a.shape, lambda i, j: (0,) * a.ndim)
    stacked = [lw[k] for k in ("wpf", "wpp", "wpm", "wout", "wfi", "wfo")]
    in_specs = [tok(D_MODEL)]
    args = [x]
    if add_pos:
        in_specs.append(pl.BlockSpec((tm, D_MODEL), lambda i, j: (j, 0)))
        args.append(pos)
    in_specs += [
        pl.BlockSpec((None, 6, D_MODEL), (lambda i, j: (i, 0, 0)) if per_batch_mod else (lambda i, j: (0, 0, 0))),
        full(norm_g), tok(FOURIER_WIDTH), tok(POOL_WIDTH), tok(MLSTM_WIDTH), ogate, tok(3 * D_MODEL),
        full(band), full(icnt), full(wpool), full(pscale)]
    in_specs += [_layer_spec(a, layer) for a in stacked]
    args += [mod, norm_g, yf, up, hm, qkvo, mg, band, icnt, wpool, pscale] + stacked
    return pl.pallas_call(
        functools.partial(_merge_ffn_kernel, tm=tm, add_pos=add_pos),
        name="merge_ffn",
        grid=(b, l // tm),
        in_specs=in_specs,
        out_specs=tok(D_MODEL),
        out_shape=jax.ShapeDtypeStruct((b, l, D_MODEL), F32),
        compiler_params=pltpu.CompilerParams(
            dimension_semantics=("arbitrary", "arbitrary"), vmem_limit_bytes=VMEM_LIMIT),
    )(*args)


def _repack_kernel(w_ref, o_ref):
    o_ref[:, 0:CAT_M] = w_ref[:, 0:OFF_G]
    o_ref[:, CAT_M:CAT_G] = w_ref[:, OFF_M:D_IN]
    tail = w_ref[:, OFF_G:OFF_G + LANES].astype(F32)
    lane = lax.broadcasted_iota(jnp.int32, tail.shape, 1)
    o_ref[:, CAT_G:CAT_W] = jnp.where(lane < N_GATE_COLS, tail, 0.0).astype(BF16)


def _repack_w_in(w_in):
    wp = jnp.pad(w_in, ((0, 0), (0, 0), (0, CAT_W - D_IN))).astype(BF16)
    tk = WEIGHT_ROW_SLAB
    blk = pl.BlockSpec((None, tk, CAT_W), lambda l, k: (l, k, 0))
    return pl.pallas_call(
        _repack_kernel,
        name="repack_w_in",
        grid=(DEPTH, D_MODEL // tk),
        in_specs=[blk],
        out_specs=blk,
        out_shape=jax.ShapeDtypeStruct((DEPTH, D_MODEL, CAT_W), BF16),
        compiler_params=pltpu.CompilerParams(
            dimension_semantics=("arbitrary", "arbitrary"), vmem_limit_bytes=VMEM_LIMIT),
    )(wp)


def _stacked_weights(w_in, w_proj_f, w_proj_p, w_proj_m, w_out, w_ffn_in, w_ffn_out):
    return dict(
        w_cat=_repack_w_in(w_in),
        wpf=w_proj_f.astype(BF16), wpp=w_proj_p.astype(BF16), wpm=w_proj_m.astype(BF16),
        wout=w_out.astype(BF16), wfi=w_ffn_in.astype(BF16), wfo=w_ffn_out.astype(BF16))


def _block(x, mod, lw, layer, norm_g, b_gate, wpool, pscale, width, tm, fold, init, state_out, hp, unroll, pos=None):
    b, l, _ = x.shape
    assert b % fold == 0 and (fold == 1 or mod.shape[0] == 1)
    folded = lambda a: a.reshape(b // fold, fold * l, a.shape[-1])
    unfolded = lambda a: a.reshape(b, l, a.shape[-1])
    tm_in = min(INPROJ_TILE, fold * l)
    uf, up, qkvo, grows, mg = _inproj(folded(x), mod, norm_g, lw["w_cat"], layer, b_gate, tm_in, min(MLSTM_CHUNK, l), pos)
    yf = _fourier(unfolded(uf))
    mouts = _mlstm(unfolded(qkvo), grows, init, state_out, hp, unroll)
    band, icnt = _pool_tables(width)
    x = _merge_ffn(folded(x), mod, norm_g, folded(yf), up, folded(mouts[0]), qkvo, mg,
                   jnp.asarray(band).astype(BF16), jnp.asarray(icnt), wpool, pscale, lw, layer, tm, pos)
    return unfolded(x), mouts[1:]


def kernel(x_prompt, x_sample, state_C, state_n, state_m, c, c_ctx, w_ada, b_ada, norm_g, w_in, b_gate, w_proj_f, w_pool, pool_scale, w_proj_p, w_proj_m, w_out, w_ffn_in, w_ffn_out):
    bp, lp, _ = x_prompt.shape
    bs, ls, _ = x_sample.shape
    cond = jnp.concatenate([c_ctx[None, :], c, jnp.zeros((COND_ROWS - 1 - bs, D_MODEL), F32)], axis=0)
    mod = _modulation(cond, w_ada, b_ada).reshape(DEPTH, COND_ROWS, 6, D_MODEL)
    lw = _stacked_weights(w_in, w_proj_f, w_proj_p, w_proj_m, w_out, w_ffn_in, w_ffn_out)
    group_eye = jnp.eye(len(POOL_WINDOWS), dtype=w_pool.dtype)[None, :, None, :, None]
    wpool = (w_pool[:, :, :, None, :] * group_eye).reshape(DEPTH, POOL_WIDTH, POOL_WIDTH).astype(BF16)
    layer_args = [(lw, i, norm_g[i], b_gate[i], wpool[i], pool_scale[i].reshape(1, POOL_WIDTH))
                  for i in range(DEPTH)]

    xp = x_prompt
    states = None
    for i in range(DEPTH):
        xp, states = _block(xp, mod[i, 0:1], *layer_args[i], lp, TOKEN_TILE, TOKEN_TILE // lp, None, (i, states),
                            HEADS, 1)
    new_c, new_n, new_m = states

    xs = x_sample
    pos = jnp.asarray(_pos_table(ls // GRID_W))
    n0 = state_n.reshape(bs, DEPTH, N_DIR, HEADS, 1, DH)
    m0 = jnp.broadcast_to(state_m[..., None, None], (bs, DEPTH, N_DIR, HEADS, 1, LANES))
    for i in range(DEPTH):
        init = (state_C, n0, m0, i)
        xs, _ = _block(xs, mod[i, 1:1 + bs], *layer_args[i], GRID_W, TOKEN_TILE, 1, init, None, HEADS, 1,
                       pos if i == 0 else None)

    return (xp, xs, new_c, new_n[:, :, :, :, 0, :], new_m[:, :, :, :, 0, 0])
```

```python
import functools

import numpy as np
import jax
import jax.numpy as jnp
from jax import lax
from jax.experimental import pallas as pl
from jax.experimental.pallas import tpu as pltpu

F32 = jnp.float32
BF16 = jnp.bfloat16

D_MODEL = 1024
DEPTH = 2
GRID_W = 64
FOURIER_GROUPS = 4
FOURIER_WIDTH = D_MODEL // 4
FOURIER_GROUP_DIM = FOURIER_WIDTH // FOURIER_GROUPS
POOL_WINDOWS = (2, 4, 8, 16)
POOL_WIDTH = D_MODEL // 4
POOL_GROUP_DIM = POOL_WIDTH // len(POOL_WINDOWS)
HEADS = 4
MLSTM_WIDTH = D_MODEL // 2
DH = MLSTM_WIDTH // HEADS
N_DIR = 2
N_GATE_COLS = N_DIR * 2 * HEADS
D_FF = -(-8 * D_MODEL // (3 * 256)) * 256
RMS_EPS = 1e-6
POS_BASE = 10000.0

OFF_F = 0
OFF_P = OFF_F + FOURIER_WIDTH
OFF_QKVO = OFF_P + POOL_WIDTH
OFF_G = OFF_QKVO + 4 * MLSTM_WIDTH
OFF_M = OFF_G + N_GATE_COLS
D_IN = OFF_M + 3 * D_MODEL

LANES = 128
MLSTM_CHUNK = 256
POOL_BLOCK = 256
TOKEN_TILE = 512
INPROJ_TILE = 1024
PROJ_CHUNK = 512
WEIGHT_ROW_SLAB = 256
COND_ROWS = 16
VMEM_LIMIT = 60 * 1024 * 1024
MXU_COLS = 256
FFN_CHUNKS = ((0, 5 * MXU_COLS), (5 * MXU_COLS, D_FF))
CAT_M = OFF_G
CAT_G = CAT_M + 3 * D_MODEL
CAT_W = CAT_G + LANES


def _dot(a, b):
    return jnp.dot(a, b, preferred_element_type=F32)


def _rms(x, g):
    return x * lax.rsqrt(jnp.mean(x * x, axis=-1, keepdims=True) + RMS_EPS) * g


def _log_sigmoid(x):
    return jnp.minimum(x, 0.0) - jnp.log(1.0 + jnp.exp(-jnp.abs(x)))


def _sigmoid(x):
    return 0.5 * jnp.tanh(0.5 * x) + 0.5


@functools.lru_cache(maxsize=None)
def _dft_tables(n):
    idx = np.arange(n, dtype=np.int64)
    ang = 2.0 * np.pi * ((idx[:, None] * idx[None, :]) % n).astype(np.float64) / n
    return np.cos(ang).astype(np.float32), np.sin(ang).astype(np.float32)


@functools.lru_cache(maxsize=None)
def _dft_half_tables(n):
    f = np.arange(n // 2, dtype=np.int64)[:, None]
    j = np.arange(n // 2, dtype=np.int64)[None, :]
    out = []
    for l in (2 * j, 2 * j + 1):
        ang = 2.0 * np.pi * ((f * l) % n).astype(np.float64) / n
        out += [np.cos(ang).astype(np.float32), np.sin(ang).astype(np.float32)]
    return tuple(out)


@functools.lru_cache(maxsize=None)
def _group_dft_tables():
    c, s = _dft_tables(FOURIER_GROUP_DIM)
    bc = np.zeros((FOURIER_WIDTH, FOURIER_WIDTH), np.float32)
    bs = np.zeros((FOURIER_WIDTH, FOURIER_WIDTH), np.float32)
    for g in range(FOURIER_GROUPS):
        sl = slice(g * FOURIER_GROUP_DIM, (g + 1) * FOURIER_GROUP_DIM)
        bc[sl, sl] = c
        bs[sl, sl] = s
    return bc, bs


@functools.lru_cache(maxsize=None)
def _pool_tables(width):
    band = np.zeros((len(POOL_WINDOWS), POOL_BLOCK, POOL_BLOCK), np.float32)
    inv_cnt = np.zeros((POOL_BLOCK, POOL_WIDTH), np.float32)
    for g, w in enumerate(POOL_WINDOWS):
        left = w // 2
        right = w - 1 - left
        for t in range(POOL_BLOCK):
            row, pos = divmod(t, width)
            lo = min(max(pos - left, 0), width - 1)
            hi = min(max(pos + right, 0), width - 1)
            band[g, t, row * width + lo:row * width + hi + 1] = 1.0
            inv_cnt[t, g * POOL_GROUP_DIM:(g + 1) * POOL_GROUP_DIM] = 1.0 / (hi - lo + 1)
    return band, inv_cnt


@functools.lru_cache(maxsize=None)
def _pos_table(rows):
    quarter = D_MODEL // 4
    omega = 1.0 / (POS_BASE ** (np.arange(quarter, dtype=np.float64) / quarter))
    t = np.arange(rows * GRID_W)
    r = (t // GRID_W).astype(np.float64)
    col = (t % GRID_W).astype(np.float64)
    ar = r[:, None] * omega[None, :]
    ac = col[:, None] * omega[None, :]
    return np.concatenate([np.sin(ar), np.cos(ar), np.sin(ac), np.cos(ac)], axis=-1).astype(np.float32)


def _mod_kernel(c_ref, w_ref, b_ref, o_ref):
    @pl.when(pl.program_id(1) == 0)
    def _():
        o_ref[...] = jnp.broadcast_to(b_ref[...], o_ref.shape)

    c = c_ref[...]
    a = (c * jax.nn.sigmoid(c)).astype(BF16)
    o_ref[...] += _dot(a, w_ref[...].astype(BF16))


def _modulation(cond, w_ada, b_ada):
    tk = WEIGHT_ROW_SLAB
    n = 6 * D_MODEL
    return pl.pallas_call(
        _mod_kernel,
        name="adaln_mod",
        grid=(DEPTH, D_MODEL // tk),
        in_specs=[
            pl.BlockSpec((COND_ROWS, tk), lambda l, k: (0, k)),
            pl.BlockSpec((None, tk, n), lambda l, k: (l, k, 0)),
            pl.BlockSpec((None, 1, n), lambda l, k: (l, 0, 0)),
        ],
        out_specs=pl.BlockSpec((None, COND_ROWS, n), lambda l, k: (l, 0, 0)),
        out_shape=jax.ShapeDtypeStruct((DEPTH, COND_ROWS, 6 * D_MODEL), F32),
        compiler_params=pltpu.CompilerParams(
            dimension_semantics=("arbitrary", "arbitrary"), vmem_limit_bytes=VMEM_LIMIT),
    )(cond, w_ada, b_ada.reshape(DEPTH, 1, 6 * D_MODEL))


def _gate_rows(g, bias_col, rows_ref, chunk):
    x = g.T[:N_GATE_COLS, :] + bias_col
    row = lax.broadcasted_iota(jnp.int32, x.shape, 0)
    fwd = row < 2 * HEADS
    lf = _log_sigmoid(x)
    b_f = jnp.where(fwd, _scan_lanes(lf, False, jnp.add, 0.0), _scan_lanes(lf, True, jnp.add, 0.0))
    b_i = pltpu.roll(b_f, N_GATE_COLS - HEADS, 0)
    a = x - b_i
    p = jnp.where(fwd, _scan_lanes(a, False, jnp.maximum, -jnp.inf), _scan_lanes(a, True, jnp.maximum, -jnp.inf))
    sl = slice(chunk * N_GATE_COLS, (chunk + 1) * N_GATE_COLS)
    rows_ref[0, sl, :] = a
    rows_ref[1, sl, :] = b_i
    rows_ref[2, sl, :] = b_i + p
    rows_ref[3, sl, :] = p


def _inproj_kernel(*refs, add_pos, chunk):
    it = iter(refs)
    x_ref = next(it)
    pos_ref = next(it) if add_pos else None
    mod_ref, ng_ref, w_ref, bg_ref = (next(it) for _ in range(4))
    uf_ref, up_ref, qkvo_ref, rows_ref, mg_ref = (next(it) for _ in range(5))

    x = x_ref[...]
    if add_pos:
        x = x + pos_ref[...]
    blocks = [slice(r * chunk, (r + 1) * chunk) for r in range(x.shape[0] // chunk)]
    gain = ng_ref[0:1, :] * (1.0 + mod_ref[1:2, :])
    ck = PROJ_CHUNK
    h_blocks = []
    for rows in blocks:
        hb = (_rms(x[rows, :], gain) + mod_ref[0:1, :]).astype(BF16)
        za = _dot(hb, w_ref[:, 0:ck])
        uf_ref[rows, :] = za[:, :FOURIER_WIDTH].astype(BF16)
        up_ref[rows, :] = za[:, FOURIER_WIDTH:].astype(BF16)
        h_blocks.append(hb)
    h = h_blocks[0] if len(blocks) == 1 else jnp.concatenate(h_blocks, axis=0)
    for j in range(4 * MLSTM_WIDTH // ck):
        z = _dot(h, w_ref[:, (j + 1) * ck:(j + 2) * ck])
        if j * ck >= 3 * MLSTM_WIDTH:
            z = _sigmoid(z)
        qkvo_ref[:, j * ck:(j + 1) * ck] = z.astype(BF16)
    g = _dot(h, w_ref[:, CAT_G:CAT_W])
    for c in range(g.shape[0] // chunk):
        _gate_rows(g[c * chunk:(c + 1) * chunk, :], bg_ref[...], rows_ref, c)
    n_mg = 3 * D_MODEL // ck
    for j in range(n_mg):
        cols = slice(j * ck, (j + 1) * ck)
        wcols = slice(CAT_M + j * ck, CAT_M + (j + 1) * ck)
        if j < n_mg - 1:
            mg_ref[:, cols] = _sigmoid(_dot(h, w_ref[:, wcols])).astype(BF16)
        else:
            for rows, hb in zip(blocks, h_blocks):
                mg_ref[rows, cols] = _sigmoid(_dot(hb, w_ref[:, wcols])).astype(BF16)


def _layer_spec(a, layer):
    return pl.BlockSpec((None,) + a.shape[1:], lambda *_: (layer,) + (0,) * (a.ndim - 1),
                        pipeline_mode=pl.Buffered(1))


def _inproj(x, mod, norm_g, w_cat, layer, b_gate, tm, t, pos=None):
    b, l, _ = x.shape
    add_pos = pos is not None
    assert tm % t == 0 and l % tm == 0
    bias_col = b_gate.reshape(N_GATE_COLS, 1)
    per_batch_mod = mod.shape[0] > 1
    tok = lambda w: pl.BlockSpec((None, tm, w), lambda i, j: (i, j, 0))
    full = lambda a: pl.BlockSpec(a.shape, lambda i, j: (0,) * a.ndim)
    in_specs = [tok(D_MODEL)]
    args = [x]
    if add_pos:
        in_specs.append(pl.BlockSpec((tm, D_MODEL), lambda i, j: (j, 0)))
        args.append(pos)
    in_specs += [
        pl.BlockSpec((None, 6, D_MODEL), (lambda i, j: (i, 0, 0)) if per_batch_mod else (lambda i, j: (0, 0, 0))),
        full(norm_g), _layer_spec(w_cat, layer), full(bias_col)]
    args += [mod, norm_g, w_cat, bias_col]
    rows_per_tile = tm // t * N_GATE_COLS
    out_specs = [tok(FOURIER_WIDTH), tok(POOL_WIDTH), tok(4 * MLSTM_WIDTH),
                 pl.BlockSpec((None, 4, rows_per_tile, t), lambda i, j: (i, 0, j, 0)), tok(3 * D_MODEL)]
    out_shape = [
        jax.ShapeDtypeStruct((b, l, FOURIER_WIDTH), BF16),
        jax.ShapeDtypeStruct((b, l, POOL_WIDTH), BF16),
        jax.ShapeDtypeStruct((b, l, 4 * MLSTM_WIDTH), BF16),
        jax.ShapeDtypeStruct((b, 4, l // t * N_GATE_COLS, t), F32),
        jax.ShapeDtypeStruct((b, l, 3 * D_MODEL), BF16)]
    return pl.pallas_call(
        functools.partial(_inproj_kernel, add_pos=add_pos, chunk=t),
        name="inproj",
        grid=(b, l // tm),
        in_specs=in_specs, out_specs=out_specs, out_shape=out_shape,
        compiler_params=pltpu.CompilerParams(
            dimension_semantics=("arbitrary", "arbitrary"), vmem_limit_bytes=VMEM_LIMIT),
    )(*args)


def _fourier_kernel(u_ref, ce_ref, se_ref, co_ref, so_ref, bc_ref, bs_ref, o_ref, vc_ref, vs_ref, *, scale, half):
    nseq, l, w = u_ref.shape
    u = u_ref[...].reshape(nseq * l, w)
    nblk = w // LANES
    for ref, tab in ((vc_ref, bc_ref), (vs_ref, bs_ref)):
        v = _dot(u, tab[...])
        for k in range(nblk):
            ref[k] = v[:, k * LANES:(k + 1) * LANES]

    def rows(ref, start):
        return jnp.concatenate([ref[k, pl.ds(start, half, stride=2), :] for k in range(nblk)], axis=1).astype(BF16)

    for s in range(nseq):
        e = _dot(ce_ref[...], rows(vc_ref, s * l)) - _dot(se_ref[...], rows(vs_ref, s * l))
        o = _dot(co_ref[...], rows(vc_ref, s * l + 1)) - _dot(so_ref[...], rows(vs_ref, s * l + 1))
        o_ref[s, 0:half, :] = ((e + o) * scale).astype(BF16)
        o_ref[s, half:, :] = ((e - o) * scale).astype(BF16)


def _fourier(uf):
    b, l, w = uf.shape
    half = l // 2
    bc, bs = (jnp.asarray(t).astype(BF16) for t in _group_dft_tables())
    tables = [jnp.asarray(t).astype(BF16) for t in _dft_half_tables(l)]
    full = lambda a: pl.BlockSpec(a.shape, lambda i: (0,) * a.ndim)
    nseq = max(1, min(b, INPROJ_TILE // l))
    assert b % nseq == 0
    seq = pl.BlockSpec((nseq, l, w), lambda i: (i, 0, 0))
    return pl.pallas_call(
        functools.partial(_fourier_kernel, scale=float((l * FOURIER_GROUP_DIM) ** -0.5), half=half),
        name="fourier",
        grid=(b // nseq,),
        in_specs=[seq] + [full(t) for t in tables] + [full(bc), full(bs)],
        out_specs=seq,
        out_shape=jax.ShapeDtypeStruct((b, l, w), BF16),
        scratch_shapes=[pltpu.VMEM((w // LANES, nseq * l, LANES), F32),
                        pltpu.VMEM((w // LANES, nseq * l, LANES), F32)],
        compiler_params=pltpu.CompilerParams(
            dimension_semantics=("arbitrary",), vmem_limit_bytes=VMEM_LIMIT),
    )(uf, *tables, bc, bs)


def _scan_lanes(x, reverse, op, fill):
    n = x.shape[-1]
    lane = lax.broadcasted_iota(jnp.int32, x.shape, x.ndim - 1)
    s = 1
    while s < n:
        if reverse:
            x = op(x, jnp.where(lane < n - s, pltpu.roll(x, n - s, x.ndim - 1), fill))
        else:
            x = op(x, jnp.where(lane >= s, pltpu.roll(x, s, x.ndim - 1), fill))
        s *= 2
    return x


SPLIT_ROWS = 16
CHAIN_GROUP = 2


def _mlstm_kernel(*refs, seq, chunk, hp, seq_group, has_init, emit_state, n_prev, unroll):
    it = iter(refs)
    q_ref, k_ref, v_ref, rows_ref = (next(it) for _ in range(4))
    c0_ref, n0_ref, m0_ref = (next(it) for _ in range(3)) if has_init else (None, None, None)
    pc_ref, pn_ref, pm_ref = (next(it) for _ in range(3)) if n_prev else (None, None, None)
    hm_ref = next(it)
    cs_ref, ns_ref, ms_ref = (next(it) for _ in range(3)) if emit_state else (None, None, None)
    hf_ref, hb_ref = (next(it) for _ in range(2))

    t = chunk
    nc = seq // t
    rep = t // DH
    scale = DH ** -0.5

    row_i = lax.broadcasted_iota(jnp.int32, (t, t), 0)
    col_i = lax.broadcasted_iota(jnp.int32, (t, t), 1)
    eye_dh = jnp.where(lax.broadcasted_iota(jnp.int32, (DH, DH), 0) == lax.broadcasted_iota(jnp.int32, (DH, DH), 1),
                       1.0, 0.0).astype(BF16)
    sub = lax.broadcasted_iota(jnp.int32, (SPLIT_ROWS, t), 0)
    part = sub % 3
    p_sub = lax.broadcasted_iota(jnp.int32, (SPLIT_ROWS, 2 * DH), 0)
    p_lane = lax.broadcasted_iota(jnp.int32, (SPLIT_ROWS, 2 * DH), 1)
    gather_mat = jnp.where(((p_sub < 3) & (p_lane < DH)) | ((p_sub >= 3) & (p_sub < 6) & (p_lane >= DH)),
                           1.0, 0.0).astype(BF16)
    ones_cols = jnp.ones((t, DH), BF16)

    def stage1(c, hh, direction, Cn):
        rows = pl.ds(pl.multiple_of(c * t, t), t)
        cols = slice(hh * DH, (hh + 1) * DH)
        chunk_row = (pl.program_id(0) % seq_group) * nc + c
        r = pl.ds(chunk_row * N_GATE_COLS + direction * 2 * HEADS + pl.program_id(1) * hp + hh, 1)
        a_r = rows_ref[0, r, :]
        b_r = rows_ref[1, r, :]
        g_r = rows_ref[2, r, :]
        p_r = rows_ref[3, r, :]
        if direction == 0:
            end = t - 1
            mask = col_i <= row_i
        else:
            end = 0
            mask = col_i >= row_i
        b_end = b_r[:, end:end + 1]
        p_end = p_r[:, end:end + 1]

        x0 = jnp.where(sub < 3, b_r, jnp.where(sub < 6, g_r, 0.0))
        x1 = x0 - x0.astype(BF16).astype(F32)
        x2 = x1 - x1.astype(BF16).astype(F32)
        xs = jnp.where(part == 0, x0, jnp.where(part == 1, x1, x2)).astype(BF16)
        bg = lax.dot_general(xs, gather_mat, (((0,), (0,)), ((), ())), preferred_element_type=F32)
        qc = q_ref[rows, cols]
        kc = k_ref[rows, cols]
        v1 = jnp.concatenate([v_ref[rows, cols], ones_cols], axis=1)
        qk = lax.dot_general(qc, kc, (((1,), (1,)), ((), ())), preferred_element_type=F32)
        qs = _dot(qc, Cn.astype(BF16))
        kt = lax.dot_general(eye_dh, kc, (((1,), (1,)), ((), ())), preferred_element_type=F32)
        return rows, cols, mask, a_r, b_end, p_end, bg, v1, qk, qs, kt

    def stage2(st, Cn, m, h_ref, other_ref):
        rows, cols, mask, a_r, b_end, p_end, bg, v1, qk, qs, kt = st
        b_c = bg[:, :DH]
        m_t = jnp.maximum(b_c + m, bg[:, DH:])
        mu = m_t - b_c
        sdec = jnp.exp(m - mu)
        mu_t = jnp.concatenate([mu] * rep, axis=1)
        d = jnp.where(mask, jnp.exp((a_r + np.float32(np.log(scale))) - mu_t), 0.0)
        sv = _dot((qk * d).astype(BF16), v1)
        num = sdec * qs[:, :DH] + sv[:, :DH]
        den = sdec * qs[:, DH:] + sv[:, DH:]
        h = num * (1.0 / jnp.maximum(jnp.abs(den), jnp.exp(-m_t)))
        if other_ref is None:
            h_ref[rows, cols] = h
        else:
            hm_ref[rows, cols] = (h + other_ref[rows, cols]).astype(BF16)

        mx = jnp.maximum(m, p_end)
        w_r = jnp.exp(a_r - mx) * scale
        Cn_new = jnp.exp(m - mx) * Cn + _dot((kt * w_r).astype(BF16), v1)
        return Cn_new, b_end + mx

    def body(ci, carry, second_half):
        chains = [(hh, d) for hh in range(hp) for d in range(N_DIR)]
        chunk_of = lambda d: ci if d == 0 else nc - 1 - ci
        out = []
        for g0 in range(0, len(chains), CHAIN_GROUP):
            group = list(enumerate(chains))[g0:g0 + CHAIN_GROUP]
            firsts = [stage1(chunk_of(d), hh, d, carry[2 * k]) for k, (hh, d) in group]
            for st, (k, (hh, d)) in zip(firsts, group):
                mine, other = (hf_ref, hb_ref) if d == 0 else (hb_ref, hf_ref)
                out += stage2(st, carry[2 * k], carry[2 * k + 1], mine, other if second_half else None)
        return tuple(out)

    init = []
    for hh in range(hp):
        for d in range(N_DIR):
            if has_init:
                n_col = jnp.sum(eye_dh.astype(F32) * n0_ref[d, hh], axis=1, keepdims=True)
                init += [jnp.concatenate([c0_ref[d, hh], jnp.broadcast_to(n_col, (DH, DH))], axis=1),
                         m0_ref[d, hh][:, 0:1]]
            else:
                init += [jnp.zeros((DH, 2 * DH), F32), jnp.zeros((1, 1), F32)]
    if nc % 2 == 0:
        mid = lax.fori_loop(0, nc // 2, functools.partial(body, second_half=False), tuple(init), unroll=unroll)
        final = lax.fori_loop(nc // 2, nc, functools.partial(body, second_half=True), mid, unroll=unroll)
    else:
        final = lax.fori_loop(0, nc, functools.partial(body, second_half=False), tuple(init), unroll=unroll)
        hm_ref[...] = (hf_ref[...] + hb_ref[...]).astype(BF16)
    if emit_state:
        for layer in range(n_prev):
            cs_ref[layer] = pc_ref[layer]
            ns_ref[layer] = pn_ref[layer]
            ms_ref[layer] = pm_ref[layer]
        eye_f = eye_dh.astype(F32)
        for hh in range(hp):
            for d in range(N_DIR):
                Cn, m = final[2 * (hh * N_DIR + d):2 * (hh * N_DIR + d) + 2]
                cs_ref[n_prev, d, hh] = Cn[:, :DH]
                ns_ref[n_prev, d, hh] = jnp.sum(eye_f * Cn[:, DH:], axis=0, keepdims=True)
                ms_ref[n_prev, d, hh] = jnp.broadcast_to(m, (1, LANES))


def _mlstm(qkvo, grows, init, state_out, hp, unroll):
    b, l, _ = qkvo.shape
    emit_state = state_out is not None
    t = grows.shape[-1]
    nc = l // t
    seq_group = b // grows.shape[0]
    assert l % t == 0 and t % DH == 0 and HEADS % hp == 0 and nc % unroll == 0
    assert grows.shape[2] == seq_group * nc * N_GATE_COLS
    has_init = init is not None
    groups = HEADS // hp
    head_cols = lambda k: pl.BlockSpec((None, l, hp * DH), lambda i, h: (i, 0, k * groups + h))
    in_specs = [
        head_cols(0), head_cols(1), head_cols(2),
        pl.BlockSpec((None,) + grows.shape[1:], lambda i, h: (i // seq_group, 0, 0, 0))]
    args = [qkvo, qkvo, qkvo, grows]
    if has_init:
        c0, n0, m0, layer = init
        in_specs += [
            pl.BlockSpec((None, None, N_DIR, hp, DH, DH), lambda i, h: (i, layer, 0, h, 0, 0)),
            pl.BlockSpec((None, None, N_DIR, hp, 1, DH), lambda i, h: (i, layer, 0, h, 0, 0)),
            pl.BlockSpec((None, None, N_DIR, hp, 1, LANES), lambda i, h: (i, layer, 0, h, 0, 0))]
        args += [c0, n0, m0]
    out_specs = [pl.BlockSpec((None, l, hp * DH), lambda i, h: (i, 0, h))]
    out_shape = [jax.ShapeDtypeStruct((b, l, MLSTM_WIDTH), BF16)]
    n_prev = 0
    if emit_state:
        n_prev, prev = state_out
        slabs = lambda n, w: pl.BlockSpec((None, n, N_DIR, hp, w, DH), lambda i, h: (i, 0, 0, h, 0, 0))
        out_specs += [slabs(n_prev + 1, DH), slabs(n_prev + 1, 1), slabs(n_prev + 1, 1)]
        out_shape += [jax.ShapeDtypeStruct((b, n_prev + 1, N_DIR, HEADS, DH, DH), F32),
                      jax.ShapeDtypeStruct((b, n_prev + 1, N_DIR, HEADS, 1, DH), F32),
                      jax.ShapeDtypeStruct((b, n_prev + 1, N_DIR, HEADS, 1, DH), F32)]
        if n_prev:
            in_specs += [slabs(n_prev, DH), slabs(n_prev, 1), slabs(n_prev, 1)]
            args += list(prev)
    return pl.pallas_call(
        functools.partial(_mlstm_kernel, seq=l, chunk=t, hp=hp, seq_group=seq_group, has_init=has_init,
                          emit_state=emit_state, n_prev=n_prev, unroll=unroll),
        name="mlstm",
        grid=(b, groups),
        in_specs=in_specs, out_specs=out_specs, out_shape=out_shape,
        scratch_shapes=[
            pltpu.VMEM((l, hp * DH), F32),
            pltpu.VMEM((l, hp * DH), F32)],
        compiler_params=pltpu.CompilerParams(
            dimension_semantics=("arbitrary", "arbitrary"), vmem_limit_bytes=VMEM_LIMIT),
    )(*args)


def _merge_ffn_kernel(*refs, tm, add_pos):
    it = iter(refs)
    x_ref = next(it)
    pos_ref = next(it) if add_pos else None
    (mod_ref, ng_ref, yf_ref, up_ref, hm_ref, og_ref, mg_ref, band_ref, icnt_ref, wpool_ref, pscale_ref,
     wpf_ref, wpp_ref, wpm_ref, wout_ref, wfi_ref, wfo_ref, o_ref) = (next(it) for _ in range(18))
    x = x_ref[...]
    if add_pos:
        x = x + pos_ref[...]

    blocks = [slice(r * POOL_BLOCK, (r + 1) * POOL_BLOCK) for r in range(tm // POOL_BLOCK)]

    lane_group = lax.broadcasted_iota(jnp.int32, (POOL_BLOCK, POOL_WIDTH), 1) // POOL_GROUP_DIM
    pooled = []
    for rows in blocks:
        u = up_ref[rows, :]
        acc = jnp.zeros((POOL_BLOCK, POOL_WIDTH), F32)
        for g in range(len(POOL_WINDOWS)):
            acc = jnp.where(lane_group == g, _dot(band_ref[g], u), acc)
        pooled.append((acc * icnt_ref[...] - u.astype(F32)).astype(BF16))

    merged = []
    for rows, p in zip(blocks, pooled):
        pp_pre = _dot(p, wpool_ref[...])
        hg = (og_ref[rows, :].astype(F32) * hm_ref[rows, :].astype(F32)).astype(BF16)
        pp = (pp_pre * pscale_ref[...]).astype(BF16)
        yf_b = yf_ref[rows, :]
        pieces = []
        for c0 in range(0, D_MODEL, MXU_COLS):
            cs = slice(c0, c0 + MXU_COLS)
            g_f = mg_ref[rows, c0:c0 + MXU_COLS].astype(F32)
            g_p = mg_ref[rows, D_MODEL + c0:D_MODEL + c0 + MXU_COLS].astype(F32)
            g_m = mg_ref[rows, 2 * D_MODEL + c0:2 * D_MODEL + c0 + MXU_COLS].astype(F32)
            pieces.append((g_f * _dot(yf_b, wpf_ref[:, cs]) + g_p * _dot(pp, wpp_ref[:, cs])
                           + g_m * _dot(hg, wpm_ref[:, cs])).astype(BF16))
        merged.append(jnp.concatenate(pieces, axis=1))

    x1_blocks, h2_blocks = [], []
    for rows, y in zip(blocks, merged):
        x1b = x[rows, :] + _rms(_dot(y, wout_ref[...]), ng_ref[1:2, :] * mod_ref[2:3, :])
        x1_blocks.append(x1b)
        h2_blocks.append((_rms(x1b, ng_ref[2:3, :] * (1.0 + mod_ref[4:5, :])) + mod_ref[3:4, :]).astype(BF16))
    h2 = h2_blocks[0] if len(blocks) == 1 else jnp.concatenate(h2_blocks, axis=0)
    acc = jnp.zeros((tm, D_MODEL), F32)
    def swiglu(hrows, c0, c1):
        tiles = []
        for t0 in range(c0, c1, MXU_COLS):
            a = _dot(hrows, wfi_ref[:, t0:t0 + MXU_COLS])
            bb = _dot(hrows, wfi_ref[:, D_FF + t0:D_FF + t0 + MXU_COLS])
            tiles.append((a * _sigmoid(a) * bb).astype(BF16))
        return jnp.concatenate(tiles, axis=1)

    for k, (c0, c1) in enumerate(FFN_CHUNKS):
        if k == 0:
            act = jnp.concatenate([swiglu(hb, c0, c1) for hb in h2_blocks], axis=0)
        else:
            act = swiglu(h2, c0, c1)
        if k < len(FFN_CHUNKS) - 1:
            acc = acc + _dot(act, wfo_ref[c0:c1, :])
        else:
            for rows, x1b in zip(blocks, x1_blocks):
                yb = acc[rows, :] + _dot(act[rows, :], wfo_ref[c0:c1, :])
                o_ref[rows, :] = x1b + _rms(yb, ng_ref[3:4, :] * mod_ref[5:6, :])


def _merge_ffn(x, mod, norm_g, yf, up, hm, qkvo, mg, band, icnt, wpool, pscale, lw, layer, tm, pos=None):
    b, l, _ = x.shape
    add_pos = pos is not None
    ogate = pl.BlockSpec((None, tm, MLSTM_WIDTH), lambda i, j: (i, j, 3))
    per_batch_mod = mod.shape[0] > 1
    tok = lambda w: pl.BlockSpec((None, tm, w), lambda i, j: (i, j, 0))
    full = lambda a: pl.BlockSpec(a.shape, lambda i, j: (0,) * a.ndim)
    stacked = [lw[k] for k in ("wpf", "wpp", "wpm", "wout", "wfi", "wfo")]
    in_specs = [tok(D_MODEL)]
    args = [x]
    if add_pos:
        in_specs.append(pl.BlockSpec((tm, D_MODEL), lambda i, j: (j, 0)))
        args.append(pos)
    in_specs += [
        pl.BlockSpec((None, 6, D_MODEL), (lambda i, j: (i, 0, 0)) if per_batch_mod else (lambda i, j: (0, 0, 0))),
        full(norm_g), tok(FOURIER_WIDTH), tok(POOL_WIDTH), tok(MLSTM_WIDTH), ogate, tok(3 * D_MODEL),
        full(band), full(icnt), full(wpool), full(pscale)]
    in_specs += [_layer_spec(a, layer) for a in stacked]
    args += [mod, norm_g, yf, up, hm, qkvo, mg, band, icnt, wpool, pscale] + stacked
    return pl.pallas_call(
        functools.partial(_merge_ffn_kernel, tm=tm, add_pos=add_pos),
        name="merge_ffn",
        grid=(b, l // tm),
        in_specs=in_specs,
        out_specs=tok(D_MODEL),
        out_shape=jax.ShapeDtypeStruct((b, l, D_MODEL), F32),
        compiler_params=pltpu.CompilerParams(
            dimension_semantics=("arbitrary", "arbitrary"), vmem_limit_bytes=VMEM_LIMIT),
    )(*args)


def _repack_kernel(w_ref, o_ref):
    o_ref[:, 0:CAT_M] = w_ref[:, 0:OFF_G]
    o_ref[:, CAT_M:CAT_G] = w_ref[:, OFF_M:D_IN]
    tail = w_ref[:, OFF_G:OFF_G + LANES].astype(F32)
    lane = lax.broadcasted_iota(jnp.int32, tail.shape, 1)
    o_ref[:, CAT_G:CAT_W] = jnp.where(lane < N_GATE_COLS, tail, 0.0).astype(BF16)


def _repack_w_in(w_in):
    wp = jnp.pad(w_in, ((0, 0), (0, 0), (0, CAT_W - D_IN))).astype(BF16)
    tk = WEIGHT_ROW_SLAB
    blk = pl.BlockSpec((None, tk, CAT_W), lambda l, k: (l, k, 0))
    return pl.pallas_call(
        _repack_kernel,
        name="repack_w_in",
        grid=(DEPTH, D_MODEL // tk),
        in_specs=[blk],
        out_specs=blk,
        out_shape=jax.ShapeDtypeStruct((DEPTH, D_MODEL, CAT_W), BF16),
        compiler_params=pltpu.CompilerParams(
            dimension_semantics=("arbitrary", "arbitrary"), vmem_limit_bytes=VMEM_LIMIT),
    )(wp)


def _stacked_weights(w_in, w_proj_f, w_proj_p, w_proj_m, w_out, w_ffn_in, w_ffn_out):
    return dict(
        w_cat=_repack_w_in(w_in),
        wpf=w_proj_f.astype(BF16), wpp=w_proj_p.astype(BF16), wpm=w_proj_m.astype(BF16),
        wout=w_out.astype(BF16), wfi=w_ffn_in.astype(BF16), wfo=w_ffn_out.astype(BF16))


def _block(x, mod, lw, layer, norm_g, b_gate, wpool, pscale, width, tm, fold, init, state_out, hp, unroll, pos=None):
    b, l, _ = x.shape
    assert b % fold == 0 and (fold == 1 or mod.shape[0] == 1)
    folded = lambda a: a.reshape(b // fold, fold * l, a.shape[-1])
    unfolded = lambda a: a.reshape(b, l, a.shape[-1])
    tm_in = min(INPROJ_TILE, fold * l)
    uf, up, qkvo, grows, mg = _inproj(folded(x), mod, norm_g, lw["w_cat"], layer, b_gate, tm_in, min(MLSTM_CHUNK, l), pos)
    yf = _fourier(unfolded(uf))
    mouts = _mlstm(unfolded(qkvo), grows, init, state_out, hp, unroll)
    band, icnt = _pool_tables(width)
    x = _merge_ffn(folded(x), mod, norm_g, folded(yf), up, folded(mouts[0]), qkvo, mg,
                   jnp.asarray(band).astype(BF16), jnp.asarray(icnt), wpool, pscale, lw, layer, tm, pos)
    return unfolded(x), mouts[1:]


def kernel(x_prompt, x_sample, state_C, state_n, state_m, c, c_ctx, w_ada, b_ada, norm_g, w_in, b_gate, w_proj_f, w_pool, pool_scale, w_proj_p, w_proj_m, w_out, w_ffn_in, w_ffn_out):
    bp, lp, _ = x_prompt.shape
    bs, ls, _ = x_sample.shape
    cond = jnp.concatenate([c_ctx[None, :], c, jnp.zeros((COND_ROWS - 1 - bs, D_MODEL), F32)], axis=0)
    mod = _modulation(cond, w_ada, b_ada).reshape(DEPTH, COND_ROWS, 6, D_MODEL)
    lw = _stacked_weights(w_in, w_proj_f, w_proj_p, w_proj_m, w_out, w_ffn_in, w_ffn_out)
    group_eye = jnp.eye(len(POOL_WINDOWS), dtype=w_pool.dtype)[None, :, None, :, None]
    wpool = (w_pool[:, :, :, None, :] * group_eye).reshape(DEPTH, POOL_WIDTH, POOL_WIDTH).astype(BF16)
    layer_args = [(lw, i, norm_g[i], b_gate[i], wpool[i], pool_scale[i].reshape(1, POOL_WIDTH))
                  for i in range(DEPTH)]

    xp = x_prompt
    states = None
    for i in range(DEPTH):
        xp, states = _block(xp, mod[i, 0:1], *layer_args[i], lp, TOKEN_TILE, TOKEN_TILE // lp, None, (i, states),
                            HEADS, 1)
    new_c, new_n, new_m = states

    xs = x_sample
    pos = jnp.asarray(_pos_table(ls // GRID_W))
    n0 = state_n.reshape(bs, DEPTH, N_DIR, HEADS, 1, DH)
    m0 = jnp.broadcast_to(state_m[..., None, None], (bs, DEPTH, N_DIR, HEADS, 1, LANES))
    for i in range(DEPTH):
        init = (state_C, n0, m0, i)
        xs, _ = _block(xs, mod[i, 1:1 + bs], *layer_args[i], GRID_W, TOKEN_TILE, 1, init, None, HEADS, 2,
                       pos if i == 0 else None)

    return (xp, xs, new_c, new_n[:, :, :, :, 0, :], new_m[:, :, :, :, 0, 0])
```

```python
import functools

import numpy as np
import jax
import jax.numpy as jnp
from jax import lax
from jax.experimental import pallas as pl
from jax.experimental.pallas import tpu as pltpu

F32 = jnp.float32
BF16 = jnp.bfloat16

D_MODEL = 1024
DEPTH = 2
GRID_W = 64
FOURIER_GROUPS = 4
FOURIER_WIDTH = D_MODEL // 4
FOURIER_GROUP_DIM = FOURIER_WIDTH // FOURIER_GROUPS
POOL_WINDOWS = (2, 4, 8, 16)
POOL_WIDTH = D_MODEL // 4
POOL_GROUP_DIM = POOL_WIDTH // len(POOL_WINDOWS)
HEADS = 4
MLSTM_WIDTH = D_MODEL // 2
DH = MLSTM_WIDTH // HEADS
N_DIR = 2
N_GATE_COLS = N_DIR * 2 * HEADS
D_FF = -(-8 * D_MODEL // (3 * 256)) * 256
RMS_EPS = 1e-6
POS_BASE = 10000.0

OFF_F = 0
OFF_P = OFF_F + FOURIER_WIDTH
OFF_QKVO = OFF_P + POOL_WIDTH
OFF_G = OFF_QKVO + 4 * MLSTM_WIDTH
OFF_M = OFF_G + N_GATE_COLS
D_IN = OFF_M + 3 * D_MODEL

LANES = 128
MLSTM_CHUNK = 256
POOL_BLOCK = 256
TOKEN_TILE = 512
INPROJ_TILE = 1024
PROJ_CHUNK = 512
WEIGHT_ROW_SLAB = 256
COND_ROWS = 16
VMEM_LIMIT = 60 * 1024 * 1024
MXU_COLS = 256
FFN_CHUNKS = ((0, 5 * MXU_COLS), (5 * MXU_COLS, D_FF))
CAT_M = OFF_G
CAT_G = CAT_M + 3 * D_MODEL
CAT_W = CAT_G + LANES


def _dot(a, b):
    return jnp.dot(a, b, preferred_element_type=F32)


def _rms(x, g):
    return x * lax.rsqrt(jnp.mean(x * x, axis=-1, keepdims=True) + RMS_EPS) * g


def _log_sigmoid(x):
    return jnp.minimum(x, 0.0) - jnp.log(1.0 + jnp.exp(-jnp.abs(x)))


def _sigmoid(x):
    return 0.5 * jnp.tanh(0.5 * x) + 0.5


@functools.lru_cache(maxsize=None)
def _dft_tables(n):
    idx = np.arange(n, dtype=np.int64)
    ang = 2.0 * np.pi * ((idx[:, None] * idx[None, :]) % n).astype(np.float64) / n
    return np.cos(ang).astype(np.float32), np.sin(ang).astype(np.float32)


@functools.lru_cache(maxsize=None)
def _dft_half_tables(n):
    f = np.arange(n // 2, dtype=np.int64)[:, None]
    j = np.arange(n // 2, dtype=np.int64)[None, :]
    out = []
    for l in (2 * j, 2 * j + 1):
        ang = 2.0 * np.pi * ((f * l) % n).astype(np.float64) / n
        out += [np.cos(ang).astype(np.float32), np.sin(ang).astype(np.float32)]
    return tuple(out)


@functools.lru_cache(maxsize=None)
def _group_dft_tables():
    c, s = _dft_tables(FOURIER_GROUP_DIM)
    bc = np.zeros((FOURIER_WIDTH, FOURIER_WIDTH), np.float32)
    bs = np.zeros((FOURIER_WIDTH, FOURIER_WIDTH), np.float32)
    for g in range(FOURIER_GROUPS):
        sl = slice(g * FOURIER_GROUP_DIM, (g + 1) * FOURIER_GROUP_DIM)
        bc[sl, sl] = c
        bs[sl, sl] = s
    return bc, bs


@functools.lru_cache(maxsize=None)
def _pool_tables(width):
    band = np.zeros((len(POOL_WINDOWS), POOL_BLOCK, POOL_BLOCK), np.float32)
    inv_cnt = np.zeros((POOL_BLOCK, POOL_WIDTH), np.float32)
    for g, w in enumerate(POOL_WINDOWS):
        left = w // 2
        right = w - 1 - left
        for t in range(POOL_BLOCK):
            row, pos = divmod(t, width)
            lo = min(max(pos - left, 0), width - 1)
            hi = min(max(pos + right, 0), width - 1)
            band[g, t, row * width + lo:row * width + hi + 1] = 1.0
            inv_cnt[t, g * POOL_GROUP_DIM:(g + 1) * POOL_GROUP_DIM] = 1.0 / (hi - lo + 1)
    return band, inv_cnt


@functools.lru_cache(maxsize=None)
def _pos_table(rows):
    quarter = D_MODEL // 4
    omega = 1.0 / (POS_BASE ** (np.arange(quarter, dtype=np.float64) / quarter))
    t = np.arange(rows * GRID_W)
    r = (t // GRID_W).astype(np.float64)
    col = (t % GRID_W).astype(np.float64)
    ar = r[:, None] * omega[None, :]
    ac = col[:, None] * omega[None, :]
    return np.concatenate([np.sin(ar), np.cos(ar), np.sin(ac), np.cos(ac)], axis=-1).astype(np.float32)


def _mod_kernel(c_ref, w_ref, b_ref, o_ref):
    @pl.when(pl.program_id(1) == 0)
    def _():
        o_ref[...] = jnp.broadcast_to(b_ref[...], o_ref.shape)

    c = c_ref[...]
    a = (c * jax.nn.sigmoid(c)).astype(BF16)
    o_ref[...] += _dot(a, w_ref[...].astype(BF16))


def _modulation(cond, w_ada, b_ada):
    tk = WEIGHT_ROW_SLAB
    n = 6 * D_MODEL
    return pl.pallas_call(
        _mod_kernel,
        name="adaln_mod",
        grid=(DEPTH, D_MODEL // tk),
        in_specs=[
            pl.BlockSpec((COND_ROWS, tk), lambda l, k: (0, k)),
            pl.BlockSpec((None, tk, n), lambda l, k: (l, k, 0)),
            pl.BlockSpec((None, 1, n), lambda l, k: (l, 0, 0)),
        ],
        out_specs=pl.BlockSpec((None, COND_ROWS, n), lambda l, k: (l, 0, 0)),
        out_shape=jax.ShapeDtypeStruct((DEPTH, COND_ROWS, 6 * D_MODEL), F32),
        compiler_params=pltpu.CompilerParams(
            dimension_semantics=("arbitrary", "arbitrary"), vmem_limit_bytes=VMEM_LIMIT),
    )(cond, w_ada, b_ada.reshape(DEPTH, 1, 6 * D_MODEL))


def _gate_rows(g, bias_col, rows_ref, chunk):
    x = g.T[:N_GATE_COLS, :] + bias_col
    row = lax.broadcasted_iota(jnp.int32, x.shape, 0)
    fwd = row < 2 * HEADS
    lf = _log_sigmoid(x)
    b_f = jnp.where(fwd, _scan_lanes(lf, False, jnp.add, 0.0), _scan_lanes(lf, True, jnp.add, 0.0))
    b_i = pltpu.roll(b_f, N_GATE_COLS - HEADS, 0)
    a = x - b_i
    p = jnp.where(fwd, _scan_lanes(a, False, jnp.maximum, -jnp.inf), _scan_lanes(a, True, jnp.maximum, -jnp.inf))
    sl = slice(chunk * N_GATE_COLS, (chunk + 1) * N_GATE_COLS)
    rows_ref[0, sl, :] = a
    rows_ref[1, sl, :] = b_i
    rows_ref[2, sl, :] = b_i + p
    rows_ref[3, sl, :] = p


def _inproj_kernel(*refs, add_pos, chunk):
    it = iter(refs)
    x_ref = next(it)
    pos_ref = next(it) if add_pos else None
    mod_ref, ng_ref, w_ref, bg_ref = (next(it) for _ in range(4))
    uf_ref, up_ref, qkvo_ref, rows_ref, mg_ref = (next(it) for _ in range(5))

    x = x_ref[...]
    if add_pos:
        x = x + pos_ref[...]
    blocks = [slice(r * chunk, (r + 1) * chunk) for r in range(x.shape[0] // chunk)]
    gain = ng_ref[0:1, :] * (1.0 + mod_ref[1:2, :])
    ck = PROJ_CHUNK
    h_blocks = []
    for rows in blocks:
        hb = (_rms(x[rows, :], gain) + mod_ref[0:1, :]).astype(BF16)
        za = _dot(hb, w_ref[:, 0:ck])
        uf_ref[rows, :] = za[:, :FOURIER_WIDTH].astype(BF16)
        up_ref[rows, :] = za[:, FOURIER_WIDTH:].astype(BF16)
        h_blocks.append(hb)
    h = h_blocks[0] if len(blocks) == 1 else jnp.concatenate(h_blocks, axis=0)
    for j in range(4 * MLSTM_WIDTH // ck):
        z = _dot(h, w_ref[:, (j + 1) * ck:(j + 2) * ck])
        if j * ck >= 3 * MLSTM_WIDTH:
            z = _sigmoid(z)
        qkvo_ref[:, j * ck:(j + 1) * ck] = z.astype(BF16)
    g = _dot(h, w_ref[:, CAT_G:CAT_W])
    for c in range(g.shape[0] // chunk):
        _gate_rows(g[c * chunk:(c + 1) * chunk, :], bg_ref[...], rows_ref, c)
    n_mg = 3 * D_MODEL // ck
    for j in range(n_mg):
        cols = slice(j * ck, (j + 1) * ck)
        wcols = slice(CAT_M + j * ck, CAT_M + (j + 1) * ck)
        if j < n_mg - 1:
            mg_ref[:, cols] = _sigmoid(_dot(h, w_ref[:, wcols])).astype(BF16)
        else:
            for rows, hb in zip(blocks, h_blocks):
                mg_ref[rows, cols] = _sigmoid(_dot(hb, w_ref[:, wcols])).astype(BF16)


def _layer_spec(a, layer):
    return pl.BlockSpec((None,) + a.shape[1:], lambda *_: (layer,) + (0,) * (a.ndim - 1),
                        pipeline_mode=pl.Buffered(1))


def _inproj(x, mod, norm_g, w_cat, layer, b_gate, tm, t, pos=None):
    b, l, _ = x.shape
    add_pos = pos is not None
    assert tm % t == 0 and l % tm == 0
    bias_col = b_gate.reshape(N_GATE_COLS, 1)
    per_batch_mod = mod.shape[0] > 1
    tok = lambda w: pl.BlockSpec((None, tm, w), lambda i, j: (i, j, 0))
    full = lambda a: pl.BlockSpec(a.shape, lambda i, j: (0,) * a.ndim)
    in_specs = [tok(D_MODEL)]
    args = [x]
    if add_pos:
        in_specs.append(pl.BlockSpec((tm, D_MODEL), lambda i, j: (j, 0)))
        args.append(pos)
    in_specs += [
        pl.BlockSpec((None, 6, D_MODEL), (lambda i, j: (i, 0, 0)) if per_batch_mod else (lambda i, j: (0, 0, 0))),
        full(norm_g), _layer_spec(w_cat, layer), full(bias_col)]
    args += [mod, norm_g, w_cat, bias_col]
    rows_per_tile = tm // t * N_GATE_COLS
    out_specs = [tok(FOURIER_WIDTH), tok(POOL_WIDTH), tok(4 * MLSTM_WIDTH),
                 pl.BlockSpec((None, 4, rows_per_tile, t), lambda i, j: (i, 0, j, 0)), tok(3 * D_MODEL)]
    out_shape = [
        jax.ShapeDtypeStruct((b, l, FOURIER_WIDTH), BF16),
        jax.ShapeDtypeStruct((b, l, POOL_WIDTH), BF16),
        jax.ShapeDtypeStruct((b, l, 4 * MLSTM_WIDTH), BF16),
        jax.ShapeDtypeStruct((b, 4, l // t * N_GATE_COLS, t), F32),
        jax.ShapeDtypeStruct((b, l, 3 * D_MODEL), BF16)]
    return pl.pallas_call(
        functools.partial(_inproj_kernel, add_pos=add_pos, chunk=t),
        name="inproj",
        grid=(b, l // tm),
        in_specs=in_specs, out_specs=out_specs, out_shape=out_shape,
        compiler_params=pltpu.CompilerParams(
            dimension_semantics=("arbitrary", "arbitrary"), vmem_limit_bytes=VMEM_LIMIT),
    )(*args)


def _fourier_kernel(u_ref, ce_ref, se_ref, co_ref, so_ref, bc_ref, bs_ref, o_ref, vc_ref, vs_ref, *, scale, half):
    nseq, l, w = u_ref.shape
    u = u_ref[...].reshape(nseq * l, w)
    nblk = w // LANES
    for ref, tab in ((vc_ref, bc_ref), (vs_ref, bs_ref)):
        v = _dot(u, tab[...])
        for k in range(nblk):
            ref[k] = v[:, k * LANES:(k + 1) * LANES]

    def rows(ref, start):
        return jnp.concatenate([ref[k, pl.ds(start, half, stride=2), :] for k in range(nblk)], axis=1).astype(BF16)

    for s in range(nseq):
        e = _dot(ce_ref[...], rows(vc_ref, s * l)) - _dot(se_ref[...], rows(vs_ref, s * l))
        o = _dot(co_ref[...], rows(vc_ref, s * l + 1)) - _dot(so_ref[...], rows(vs_ref, s * l + 1))
        o_ref[s, 0:half, :] = ((e + o) * scale).astype(BF16)
        o_ref[s, half:, :] = ((e - o) * scale).astype(BF16)


def _fourier(uf):
    b, l, w = uf.shape
    half = l // 2
    bc, bs = (jnp.asarray(t).astype(BF16) for t in _group_dft_tables())
    tables = [jnp.asarray(t).astype(BF16) for t in _dft_half_tables(l)]
    full = lambda a: pl.BlockSpec(a.shape, lambda i: (0,) * a.ndim)
    nseq = max(1, min(b, INPROJ_TILE // l))
    assert b % nseq == 0
    seq = pl.BlockSpec((nseq, l, w), lambda i: (i, 0, 0))
    return pl.pallas_call(
        functools.partial(_fourier_kernel, scale=float((l * FOURIER_GROUP_DIM) ** -0.5), half=half),
        name="fourier",
        grid=(b // nseq,),
        in_specs=[seq] + [full(t) for t in tables] + [full(bc), full(bs)],
        out_specs=seq,
        out_shape=jax.ShapeDtypeStruct((b, l, w), BF16),
        scratch_shapes=[pltpu.VMEM((w // LANES, nseq * l, LANES), F32),
                        pltpu.VMEM((w // LANES, nseq * l, LANES), F32)],
        compiler_params=pltpu.CompilerParams(
            dimension_semantics=("arbitrary",), vmem_limit_bytes=VMEM_LIMIT),
    )(uf, *tables, bc, bs)


def _scan_lanes(x, reverse, op, fill):
    n = x.shape[-1]
    lane = lax.broadcasted_iota(jnp.int32, x.shape, x.ndim - 1)
    s = 1
    while s < n:
        if reverse:
            x = op(x, jnp.where(lane < n - s, pltpu.roll(x, n - s, x.ndim - 1), fill))
        else:
            x = op(x, jnp.where(lane >= s, pltpu.roll(x, s, x.ndim - 1), fill))
        s *= 2
    return x


SPLIT_ROWS = 16
CHAIN_GROUP = 2


def _mlstm_kernel(*refs, seq, chunk, hp, seq_group, has_init, emit_state, n_prev, unroll):
    it = iter(refs)
    q_ref, k_ref, v_ref, rows_ref = (next(it) for _ in range(4))
    c0_ref, n0_ref, m0_ref = (next(it) for _ in range(3)) if has_init else (None, None, None)
    pc_ref, pn_ref, pm_ref = (next(it) for _ in range(3)) if n_prev else (None, None, None)
    hm_ref = next(it)
    cs_ref, ns_ref, ms_ref = (next(it) for _ in range(3)) if emit_state else (None, None, None)
    hf_ref, hb_ref = (next(it) for _ in range(2))

    t = chunk
    nc = seq // t
    rep = t // DH
    scale = DH ** -0.5

    row_i = lax.broadcasted_iota(jnp.int32, (t, t), 0)
    col_i = lax.broadcasted_iota(jnp.int32, (t, t), 1)
    eye_dh = jnp.where(lax.broadcasted_iota(jnp.int32, (DH, DH), 0) == lax.broadcasted_iota(jnp.int32, (DH, DH), 1),
                       1.0, 0.0).astype(BF16)
    sub = lax.broadcasted_iota(jnp.int32, (SPLIT_ROWS, t), 0)
    part = sub % 3
    p_sub = lax.broadcasted_iota(jnp.int32, (SPLIT_ROWS, 2 * DH), 0)
    p_lane = lax.broadcasted_iota(jnp.int32, (SPLIT_ROWS, 2 * DH), 1)
    gather_mat = jnp.where(((p_sub < 3) & (p_lane < DH)) | ((p_sub >= 3) & (p_sub < 6) & (p_lane >= DH)),
                           1.0, 0.0).astype(BF16)
    ones_cols = jnp.ones((t, DH), BF16)

    def stage1(c, hh, direction, Cn):
        rows = pl.ds(pl.multiple_of(c * t, t), t)
        cols = slice(hh * DH, (hh + 1) * DH)
        chunk_row = (pl.program_id(0) % seq_group) * nc + c
        r = pl.ds(chunk_row * N_GATE_COLS + direction * 2 * HEADS + pl.program_id(1) * hp + hh, 1)
        a_r = rows_ref[0, r, :]
        b_r = rows_ref[1, r, :]
        g_r = rows_ref[2, r, :]
        p_r = rows_ref[3, r, :]
        if direction == 0:
            end = t - 1
            mask = col_i <= row_i
        else:
            end = 0
            mask = col_i >= row_i
        b_end = b_r[:, end:end + 1]
        p_end = p_r[:, end:end + 1]

        x0 = jnp.where(sub < 3, b_r, jnp.where(sub < 6, g_r, 0.0))
        x1 = x0 - x0.astype(BF16).astype(F32)
        x2 = x1 - x1.astype(BF16).astype(F32)
        xs = jnp.where(part == 0, x0, jnp.where(part == 1, x1, x2)).astype(BF16)
        bg = lax.dot_general(xs, gather_mat, (((0,), (0,)), ((), ())), preferred_element_type=F32)
        qc = q_ref[rows, cols]
        kc = k_ref[rows, cols]
        v1 = jnp.concatenate([v_ref[rows, cols], ones_cols], axis=1)
        qk = lax.dot_general(qc, kc, (((1,), (1,)), ((), ())), preferred_element_type=F32)
        qs = _dot(qc, Cn.astype(BF16))
        kt = lax.dot_general(eye_dh, kc, (((1,), (1,)), ((), ())), preferred_element_type=F32)
        return rows, cols, mask, a_r, b_end, p_end, bg, v1, qk, qs, kt

    def stage2(st, Cn, m, h_ref, other_ref):
        rows, cols, mask, a_r, b_end, p_end, bg, v1, qk, qs, kt = st
        b_c = bg[:, :DH]
        m_t = jnp.maximum(b_c + m, bg[:, DH:])
        mu = m_t - b_c
        sdec = jnp.exp(m - mu)
        mu_t = jnp.concatenate([mu] * rep, axis=1)
        d = jnp.where(mask, jnp.exp((a_r + np.float32(np.log(scale))) - mu_t), 0.0)
        sv = _dot((qk * d).astype(BF16), v1)
        num = sdec * qs[:, :DH] + sv[:, :DH]
        den = sdec * qs[:, DH:] + sv[:, DH:]
        h = num * (1.0 / jnp.maximum(jnp.abs(den), jnp.exp(-m_t)))
        if other_ref is None:
            h_ref[rows, cols] = h
        else:
            hm_ref[rows, cols] = (h + other_ref[rows, cols]).astype(BF16)

        mx = jnp.maximum(m, p_end)
        w_r = jnp.exp(a_r - mx) * scale
        Cn_new = jnp.exp(m - mx) * Cn + _dot((kt * w_r).astype(BF16), v1)
        return Cn_new, b_end + mx

    def body(ci, carry, second_half):
        chains = [(hh, d) for hh in range(hp) for d in range(N_DIR)]
        chunk_of = lambda d: ci if d == 0 else nc - 1 - ci
        out = []
        for g0 in range(0, len(chains), CHAIN_GROUP):
            group = list(enumerate(chains))[g0:g0 + CHAIN_GROUP]
            firsts = [stage1(chunk_of(d), hh, d, carry[2 * k]) for k, (hh, d) in group]
            for st, (k, (hh, d)) in zip(firsts, group):
                mine, other = (hf_ref, hb_ref) if d == 0 else (hb_ref, hf_ref)
                out += stage2(st, carry[2 * k], carry[2 * k + 1], mine, other if second_half else None)
        return tuple(out)

    init = []
    for hh in range(hp):
        for d in range(N_DIR):
            if has_init:
                n_col = jnp.sum(eye_dh.astype(F32) * n0_ref[d, hh], axis=1, keepdims=True)
                init += [jnp.concatenate([c0_ref[d, hh], jnp.broadcast_to(n_col, (DH, DH))], axis=1),
                         m0_ref[d, hh][:, 0:1]]
            else:
                init += [jnp.zeros((DH, 2 * DH), F32), jnp.zeros((1, 1), F32)]
    if nc % 2 == 0:
        mid = lax.fori_loop(0, nc // 2, functools.partial(body, second_half=False), tuple(init), unroll=unroll)
        final = lax.fori_loop(nc // 2, nc, functools.partial(body, second_half=True), mid, unroll=unroll)
    else:
        final = lax.fori_loop(0, nc, functools.partial(body, second_half=False), tuple(init), unroll=unroll)
        hm_ref[...] = (hf_ref[...] + hb_ref[...]).astype(BF16)
    if emit_state:
        for layer in range(n_prev):
            cs_ref[layer] = pc_ref[layer]
            ns_ref[layer] = pn_ref[layer]
            ms_ref[layer] = pm_ref[layer]
        eye_f = eye_dh.astype(F32)
        for hh in range(hp):
            for d in range(N_DIR):
                Cn, m = final[2 * (hh * N_DIR + d):2 * (hh * N_DIR + d) + 2]
                cs_ref[n_prev, d, hh] = Cn[:, :DH]
                ns_ref[n_prev, d, hh] = jnp.sum(eye_f * Cn[:, DH:], axis=0, keepdims=True)
                ms_ref[n_prev, d, hh] = jnp.broadcast_to(m, (1, LANES))


def _mlstm(qkvo, grows, init, state_out, hp, unroll):
    b, l, _ = qkvo.shape
    emit_state = state_out is not None
    t = grows.shape[-1]
    nc = l // t
    seq_group = b // grows.shape[0]
    assert l % t == 0 and t % DH == 0 and HEADS % hp == 0 and nc % unroll == 0
    assert grows.shape[2] == seq_group * nc * N_GATE_COLS
    has_init = init is not None
    groups = HEADS // hp
    head_cols = lambda k: pl.BlockSpec((None, l, hp * DH), lambda i, h: (i, 0, k * groups + h))
    in_specs = [
        head_cols(0), head_cols(1), head_cols(2),
        pl.BlockSpec((None,) + grows.shape[1:], lambda i, h: (i // seq_group, 0, 0, 0))]
    args = [qkvo, qkvo, qkvo, grows]
    if has_init:
        c0, n0, m0, layer = init
        in_specs += [
            pl.BlockSpec((None, None, N_DIR, hp, DH, DH), lambda i, h: (i, layer, 0, h, 0, 0)),
            pl.BlockSpec((None, None, N_DIR, hp, 1, DH), lambda i, h: (i, layer, 0, h, 0, 0)),
            pl.BlockSpec((None, None, N_DIR, hp, 1, LANES), lambda i, h: (i, layer, 0, h, 0, 0))]
        args += [c0, n0, m0]
    out_specs = [pl.BlockSpec((None, l, hp * DH), lambda i, h: (i, 0, h))]
    out_shape = [jax.ShapeDtypeStruct((b, l, MLSTM_WIDTH), BF16)]
    n_prev = 0
    if emit_state:
        n_prev, prev = state_out
        slabs = lambda n, w: pl.BlockSpec((None, n, N_DIR, hp, w, DH), lambda i, h: (i, 0, 0, h, 0, 0))
        out_specs += [slabs(n_prev + 1, DH), slabs(n_prev + 1, 1), slabs(n_prev + 1, 1)]
        out_shape += [jax.ShapeDtypeStruct((b, n_prev + 1, N_DIR, HEADS, DH, DH), F32),
                      jax.ShapeDtypeStruct((b, n_prev + 1, N_DIR, HEADS, 1, DH), F32),
                      jax.ShapeDtypeStruct((b, n_prev + 1, N_DIR, HEADS, 1, DH), F32)]
        if n_prev:
            in_specs += [slabs(n_prev, DH), slabs(n_prev, 1), slabs(n_prev, 1)]
            args += list(prev)
    return pl.pallas_call(
        functools.partial(_mlstm_kernel, seq=l, chunk=t, hp=hp, seq_group=seq_group, has_init=has_init,
                          emit_state=emit_state, n_prev=n_prev, unroll=unroll),
        name="mlstm",
        grid=(b, groups),
        in_specs=in_specs, out_specs=out_specs, out_shape=out_shape,
        scratch_shapes=[
            pltpu.VMEM((l, hp * DH), F32),
            pltpu.VMEM((l, hp * DH), F32)],
        compiler_params=pltpu.CompilerParams(
            dimension_semantics=("arbitrary", "arbitrary"), vmem_limit_bytes=VMEM_LIMIT),
    )(*args)


def _merge_ffn_kernel(*refs, tm, add_pos):
    it = iter(refs)
    x_ref = next(it)
    pos_ref = next(it) if add_pos else None
    (mod_ref, ng_ref, yf_ref, up_ref, hm_ref, og_ref, mg_ref, band_ref, icnt_ref, wpool_ref, pscale_ref,
     wpf_ref, wpp_ref, wpm_ref, wout_ref, wfi_ref, wfo_ref, o_ref) = (next(it) for _ in range(18))
    x = x_ref[...]
    if add_pos:
        x = x + pos_ref[...]

    blocks = [slice(r * POOL_BLOCK, (r + 1) * POOL_BLOCK) for r in range(tm // POOL_BLOCK)]

    lane_group = lax.broadcasted_iota(jnp.int32, (POOL_BLOCK, POOL_WIDTH), 1) // POOL_GROUP_DIM
    pooled = []
    for rows in blocks:
        u = up_ref[rows, :]
        acc = jnp.zeros((POOL_BLOCK, POOL_WIDTH), F32)
        for g in range(len(POOL_WINDOWS)):
            acc = jnp.where(lane_group == g, _dot(band_ref[g], u), acc)
        pooled.append((acc * icnt_ref[...] - u.astype(F32)).astype(BF16))

    merged = []
    for rows, p in zip(blocks, pooled):
        pp_pre = _dot(p, wpool_ref[...])
        hg = (og_ref[rows, :].astype(F32) * hm_ref[rows, :].astype(F32)).astype(BF16)
        pp = (pp_pre * pscale_ref[...]).astype(BF16)
        yf_b = yf_ref[rows, :]
        pieces = []
        for c0 in range(0, D_MODEL, MXU_COLS):
            cs = slice(c0, c0 + MXU_COLS)
            g_f = mg_ref[rows, c0:c0 + MXU_COLS].astype(F32)
            g_p = mg_ref[rows, D_MODEL + c0:D_MODEL + c0 + MXU_COLS].astype(F32)
            g_m = mg_ref[rows, 2 * D_MODEL + c0:2 * D_MODEL + c0 + MXU_COLS].astype(F32)
            pieces.append((g_f * _dot(yf_b, wpf_ref[:, cs]) + g_p * _dot(pp, wpp_ref[:, cs])
                           + g_m * _dot(hg, wpm_ref[:, cs])).astype(BF16))
        merged.append(jnp.concatenate(pieces, axis=1))

    x1_blocks, h2_blocks = [], []
    for rows, y in zip(blocks, merged):
        x1b = x[rows, :] + _rms(_dot(y, wout_ref[...]), ng_ref[1:2, :] * mod_ref[2:3, :])
        x1_blocks.append(x1b)
        h2_blocks.append((_rms(x1b, ng_ref[2:3, :] * (1.0 + mod_ref[4:5, :])) + mod_ref[3:4, :]).astype(BF16))
    h2 = h2_blocks[0] if len(blocks) == 1 else jnp.concatenate(h2_blocks, axis=0)
    acc = jnp.zeros((tm, D_MODEL), F32)
    def swiglu(hrows, c0, c1):
        tiles = []
        for t0 in range(c0, c1, MXU_COLS):
            a = _dot(hrows, wfi_ref[:, t0:t0 + MXU_COLS])
            bb = _dot(hrows, wfi_ref[:, D_FF + t0:D_FF + t0 + MXU_COLS])
            tiles.append((a * _sigmoid(a) * bb).astype(BF16))
        return jnp.concatenate(tiles, axis=1)

    for k, (c0, c1) in enumerate(FFN_CHUNKS):
        if k == 0:
            act = jnp.concatenate([swiglu(hb, c0, c1) for hb in h2_blocks], axis=0)
        else:
            act = swiglu(h2, c0, c1)
        if k < len(FFN_CHUNKS) - 1:
            acc = acc + _dot(act, wfo_ref[c0:c1, :])
        else:
            for rows, x1b in zip(blocks, x1_blocks):
                yb = acc[rows, :] + _dot(act[rows, :], wfo_ref[c0:c1, :])
                o_ref[rows, :] = x1b + _rms(yb, ng_ref[3:4, :] * mod_ref[5:6, :])


def _merge_ffn(x, mod, norm_g, yf, up, hm, qkvo, mg, band, icnt, wpool, pscale, lw, layer, tm, pos=None):
    b, l, _ = x.shape
    add_pos = pos is not None
    ogate = pl.BlockSpec((None, tm, MLSTM_WIDTH), lambda i, j: (i, j, 3))
    per_batch_mod = mod.shape[0] > 1
    tok = lambda w: pl.BlockSpec((None, tm, w), lambda i, j: (i, j, 0))
    full = lambda a: pl.BlockSpec(a.shape, lambda i, j: (0,) * a.ndim)
    stacked = [lw[k] for k in ("wpf", "wpp", "wpm", "wout", "wfi", "wfo")]
    in_specs = [tok(D_MODEL)]
    args = [x]
    if add_pos:
        in_specs.append(pl.BlockSpec((tm, D_MODEL), lambda i, j: (j, 0)))
        args.append(pos)
    in_specs += [
        pl.BlockSpec((None, 6, D_MODEL), (lambda i, j: (i, 0, 0)) if per_batch_mod else (lambda i, j: (0, 0, 0))),
        full(norm_g), tok(FOURIER_WIDTH), tok(POOL_WIDTH), tok(MLSTM_WIDTH), ogate, tok(3 * D_MODEL),
        full(band), full(icnt), full(wpool), full(pscale)]
    in_specs += [_layer_spec(a, layer) for a in stacked]
    args += [mod, norm_g, yf, up, hm, qkvo, mg, band, icnt, wpool, pscale] + stacked
    return pl.pallas_call(
        functools.partial(_merge_ffn_kernel, tm=tm, add_pos=add_pos),
        name="merge_ffn",
        grid=(b, l // tm),
        in_specs=in_specs,
        out_specs=tok(D_MODEL),
        out_shape=jax.ShapeDtypeStruct((b, l, D_MODEL), F32),
        compiler_params=pltpu.CompilerParams(
            dimension_semantics=("arbitrary", "arbitrary"), vmem_limit_bytes=VMEM_LIMIT),
    )(*args)


def _repack_kernel(w_ref, o_ref):
    o_ref[:, 0:CAT_M] = w_ref[:, 0:OFF_G]
    o_ref[:, CAT_M:CAT_G] = w_ref[:, OFF_M:D_IN]
    tail = w_ref[:, OFF_G:OFF_G + LANES].astype(F32)
    lane = lax.broadcasted_iota(jnp.int32, tail.shape, 1)
    o_ref[:, CAT_G:CAT_W] = jnp.where(lane < N_GATE_COLS, tail, 0.0).astype(BF16)


def _repack_w_in(w_in):
    wp = jnp.pad(w_in, ((0, 0), (0, 0), (0, CAT_W - D_IN))).astype(BF16)
    tk = WEIGHT_ROW_SLAB
    blk = pl.BlockSpec((None, tk, CAT_W), lambda l, k: (l, k, 0))
    return pl.pallas_call(
        _repack_kernel,
        name="repack_w_in",
        grid=(DEPTH, D_MODEL // tk),
        in_specs=[blk],
        out_specs=blk,
        out_shape=jax.ShapeDtypeStruct((DEPTH, D_MODEL, CAT_W), BF16),
        compiler_params=pltpu.CompilerParams(
            dimension_semantics=("arbitrary", "arbitrary"), vmem_limit_bytes=VMEM_LIMIT),
    )(wp)


def _stacked_weights(w_in, w_proj_f, w_proj_p, w_proj_m, w_out, w_ffn_in, w_ffn_out):
    return dict(
        w_cat=_repack_w_in(w_in),
        wpf=w_proj_f.astype(BF16), wpp=w_proj_p.astype(BF16), wpm=w_proj_m.astype(BF16),
        wout=w_out.astype(BF16), wfi=w_ffn_in.astype(BF16), wfo=w_ffn_out.astype(BF16))


def _block(x, mod, lw, layer, norm_g, b_gate, wpool, pscale, width, tm, fold, init, state_out, hp, unroll, pos=None):
    b, l, _ = x.shape
    assert b % fold == 0 and (fold == 1 or mod.shape[0] == 1)
    folded = lambda a: a.reshape(b // fold, fold * l, a.shape[-1])
    unfolded = lambda a: a.reshape(b, l, a.shape[-1])
    tm_in = min(INPROJ_TILE, fold * l)
    uf, up, qkvo, grows, mg = _inproj(folded(x), mod, norm_g, lw["w_cat"], layer, b_gate, tm_in, min(MLSTM_CHUNK, l), pos)
    yf = _fourier(unfolded(uf))
    mouts = _mlstm(unfolded(qkvo), grows, init, state_out, hp, unroll)
    band, icnt = _pool_tables(width)
    x = _merge_ffn(folded(x), mod, norm_g, folded(yf), up, folded(mouts[0]), qkvo, mg,
                   jnp.asarray(band).astype(BF16), jnp.asarray(icnt), wpool, pscale, lw, layer, tm, pos)
    return unfolded(x), mouts[1:]


def kernel(x_prompt, x_sample, state_C, state_n, state_m, c, c_ctx, w_ada, b_ada, norm_g, w_in, b_gate, w_proj_f, w_pool, pool_scale, w_proj_p, w_proj_m, w_out, w_ffn_in, w_ffn_out):
    bp, lp, _ = x_prompt.shape
    bs, ls, _ = x_sample.shape
    cond = jnp.concatenate([c_ctx[None, :], c, jnp.zeros((COND_ROWS - 1 - bs, D_MODEL), F32)], axis=0)
    mod = _modulation(cond, w_ada, b_ada).reshape(DEPTH, COND_ROWS, 6, D_MODEL)
    lw = _stacked_weights(w_in, w_proj_f, w_proj_p, w_proj_m, w_out, w_ffn_in, w_ffn_out)
    group_eye = jnp.eye(len(POOL_WINDOWS), dtype=w_pool.dtype)[None, :, None, :, None]
    wpool = (w_pool[:, :, :, None, :] * group_eye).reshape(DEPTH, POOL_WIDTH, POOL_WIDTH).astype(BF16)
    layer_args = [(lw, i, norm_g[i], b_gate[i], wpool[i], pool_scale[i].reshape(1, POOL_WIDTH))
                  for i in range(DEPTH)]

    xp = x_prompt
    states = None
    for i in range(DEPTH):
        xp, states = _block(xp, mod[i, 0:1], *layer_args[i], lp, TOKEN_TILE, TOKEN_TILE // lp, None, (i, states),
                            HEADS, 1)
    new_c, new_n, new_m = states

    xs = x_sample
    pos = jnp.asarray(_pos_table(ls // GRID_W))
    n0 = state_n.reshape(bs, DEPTH, N_DIR, HEADS, 1, DH)
    m0 = jnp.broadcast_to(state_m[..., None, None], (bs, DEPTH, N_DIR, HEADS, 1, LANES))
    for i in range(DEPTH):
        init = (state_C, n0, m0, i)
        xs, _ = _block(xs, mod[i, 1:1 + bs], *layer_args[i], GRID_W, TOKEN_TILE, 1, init, None, HEADS, 4,
                       pos if i == 0 else None)

    return (xp, xs, new_c, new_n[:, :, :, :, 0, :], new_m[:, :, :, :, 0, 0])
```
